```python
import jax, jax.numpy as jnp
from jax import lax
import numpy as np

D_MODEL = 1024
BATCH = 4
SEQ = 8192
DEPTH = 2

CHUNK = 64
D_MIX = D_MODEL
D_MLSTM = D_MIX // 2
N_MLSTM_HEADS = 4
DH_MLSTM = D_MLSTM // N_MLSTM_HEADS
D_LRU = D_MIX - D_MLSTM
N_LRU_BLOCKS = 8
DB_LRU = D_LRU // N_LRU_BLOCKS
CONV_WIDTH = 4
LRU_C = 8.0
D_FF = 2816
N_EXPERTS = 8
TOP_K = 2
N_DENSE = (DEPTH + 1) // 2
N_MOE = DEPTH // 2
EPS = 1e-6
SPLITS = (D_MLSTM, 2 * D_MLSTM, 3 * D_MLSTM, 4 * D_MLSTM,
          4 * D_MLSTM + N_MLSTM_HEADS, 4 * D_MLSTM + 2 * N_MLSTM_HEADS,
          4 * D_MLSTM + 2 * N_MLSTM_HEADS + D_LRU)
D_IN = 4 * D_MLSTM + 2 * N_MLSTM_HEADS + 2 * D_LRU

kernel_name = 'hybrid_mlstm_rglru_moe_trunk'


def _rmsnorm(x, g):
    xf = x.astype(jnp.float32)
    xf = xf * lax.rsqrt(jnp.mean(xf * xf, axis=-1, keepdims=True) + EPS)
    return xf.astype(x.dtype) * g


def _causal_conv(x, w, b):
    S = x.shape[1]
    W = w.shape[0]
    xp = jnp.pad(x, ((0, 0), (W - 1, 0), (0, 0)))
    y = b
    for j in range(W):
        y = y + xp[:, j:j + S, :] * w[j]
    return y


def _mlstm_chunkwise(q, k, v, i_pre, f_pre):
    B, S, H, Dh = q.shape
    NC = S // CHUNK
    f32 = jnp.float32

    def to_chunks(t):
        t = t.astype(f32).reshape((B, NC, CHUNK, H) + t.shape[3:])
        return jnp.moveaxis(t, 3, 1)

    q = to_chunks(q)
    k = to_chunks(k) * (Dh ** -0.5)
    v = to_chunks(v)
    ig = to_chunks(i_pre)
    b = jnp.cumsum(jax.nn.log_sigmoid(to_chunks(f_pre)), axis=-1)
    b_last = b[..., -1]

    w_end = b_last[..., None] - b + ig
    m_loc = jnp.max(w_end, axis=-1)
    e_end = jnp.exp(w_end - m_loc[..., None])
    C_loc = jnp.einsum('bhcsd,bhcse->bhcde', v * e_end[..., None], k)
    n_loc = jnp.einsum('bhcs,bhcse->bhce', e_end, k)

    def step(carry, inp):
        C, n, m = carry
        Cl, nl, ml, bl = inp
        m_new = jnp.maximum(bl + m, ml)
        a = jnp.exp(bl + m - m_new)
        g = jnp.exp(ml - m_new)
        C_new = a[..., None, None] * C + g[..., None, None] * Cl
        n_new = a[..., None] * n + g[..., None] * nl
        return (C_new, n_new, m_new), (C, n, m)

    init = (jnp.zeros((B, H, Dh, Dh), f32), jnp.zeros((B, H, Dh), f32), jnp.zeros((B, H), f32))
    xs = (jnp.moveaxis(C_loc, 2, 0), jnp.moveaxis(n_loc, 2, 0),
          jnp.moveaxis(m_loc, 2, 0), jnp.moveaxis(b_last, 2, 0))
    _, (C_prev, n_prev, m_prev) = lax.scan(step, init, xs)
    C_prev = jnp.moveaxis(C_prev, 0, 2)
    n_prev = jnp.moveaxis(n_prev, 0, 2)
    m_prev = jnp.moveaxis(m_prev, 0, 2)

    causal = jnp.tril(jnp.ones((CHUNK, CHUNK), dtype=bool))
    Dmat = jnp.where(causal, b[..., :, None] - b[..., None, :] + ig[..., None, :], -jnp.inf)
    inter_log = b + m_prev[..., None]
    m_t = jnp.maximum(inter_log, jnp.max(Dmat, axis=-1))
    scores = jnp.einsum('bhctd,bhcsd->bhcts', q, k) * jnp.exp(Dmat - m_t[..., None])
    inter_scale = jnp.exp(inter_log - m_t)
    num = jnp.einsum('bhcts,bhcsd->bhctd', scores, v) \
        + inter_scale[..., None] * jnp.einsum('bhcde,bhcte->bhctd', C_prev, q)
    den = jnp.sum(scores, axis=-1) + inter_scale * jnp.einsum('bhce,bhcte->bhct', n_prev, q)
    h = num / jnp.maximum(jnp.abs(den), jnp.exp(-m_t))[..., None]
    return jnp.moveaxis(h, 1, 3).reshape(B, S, H * Dh)


def _rglru(xl, gl, w_conv, b_conv, w_a, b_a, w_x, b_x, lam):
    B, S, _ = xl.shape
    f32 = jnp.float32
    xc = _causal_conv(xl, w_conv, b_conv)
    xb = xc.reshape(B, S, N_LRU_BLOCKS, DB_LRU)
    r = jax.nn.sigmoid((jnp.einsum('bsnd,nde->bsne', xb, w_a).reshape(B, S, D_LRU) + b_a).astype(f32))
    i = jax.nn.sigmoid((jnp.einsum('bsnd,nde->bsne', xb, w_x).reshape(B, S, D_LRU) + b_x).astype(f32))
    log_a = -LRU_C * r * jax.nn.softplus(-lam.astype(f32))
    a = jnp.exp(log_a)
    u = jnp.sqrt(-jnp.expm1(2.0 * log_a)) * (i * xc.astype(f32))

    def combine(left, right):
        a1, b1 = left
        a2, b2 = right
        return a1 * a2, a2 * b1 + b2

    _, hs = lax.associative_scan(combine, (a, u), axis=1)
    return hs.astype(xl.dtype) * jax.nn.gelu(gl)


def _hybrid_mixer(h, w_in, w_conv_qk, b_conv_qk, b_gates, w_conv_lru, b_conv_lru,
                  w_lru_a, b_lru_a, w_lru_x, b_lru_x, lru_lambda, g_mix_out, w_out):
    B, S, _ = h.shape
    z = h @ w_in
    q, k, v, o, ig, fg, xl, gl = jnp.split(z, SPLITS, axis=-1)
    qk = jax.nn.silu(_causal_conv(jnp.concatenate([q, k], axis=-1), w_conv_qk, b_conv_qk))
    q, k = jnp.split(qk, 2, axis=-1)
    gates = jnp.concatenate([ig, fg], axis=-1) + b_gates
    ig, fg = jnp.split(gates, 2, axis=-1)
    shp = (B, S, N_MLSTM_HEADS, DH_MLSTM)
    hm = _mlstm_chunkwise(q.reshape(shp), k.reshape(shp), v.reshape(shp), ig, fg).astype(h.dtype)
    hm = _rmsnorm(hm.reshape(shp), g_mix_out[:D_MLSTM].reshape(N_MLSTM_HEADS, DH_MLSTM)).reshape(B, S, D_MLSTM)
    hm = hm * jax.nn.sigmoid(o)
    hl = _rglru(xl, gl, w_conv_lru, b_conv_lru, w_lru_a, b_lru_a, w_lru_x, b_lru_x, lru_lambda)
    hl = _rmsnorm(hl, g_mix_out[D_MLSTM:])
    return jnp.concatenate([hm, hl], axis=-1) @ w_out


def _swiglu(h, w_gate, w_up, w_down):
    return (jax.nn.silu(h @ w_gate) * (h @ w_up)) @ w_down


def _moe(h, w_router, b_router, w_g, w_u, w_d):
    logits = (h @ w_router + b_router).astype(jnp.float32)
    top_v, top_i = lax.top_k(logits, TOP_K)
    top_p = jax.nn.softmax(top_v, axis=-1)
    gates = jnp.sum(jax.nn.one_hot(top_i, N_EXPERTS, dtype=jnp.float32) * top_p[..., None], axis=-2).astype(h.dtype)
    y = jnp.zeros_like(h)
    for e in range(N_EXPERTS):
        y = y + gates[..., e:e + 1] * _swiglu(h, w_g[e], w_u[e], w_d[e])
    return y


def setup_inputs(seed: int = 0) -> dict:
    key = jax.random.key(seed)
    ks = jax.random.split(key, 32)
    f32 = jnp.float32

    def nrm(k, shape, scale):
        return jax.random.normal(k, shape, f32) * scale

    D, H = D_MODEL, N_MLSTM_HEADS
    a0 = jax.random.uniform(ks[20], (DEPTH, D_LRU), f32, 0.9, 0.999)
    b_in = nrm(ks[21], (DEPTH, H), 0.1)
    b_fg = 3.0 + 3.0 * jax.random.uniform(ks[22], (DEPTH, H), f32)
    return {
        'x': nrm(ks[0], (BATCH, SEQ, D), 1.0),
        'c': nrm(ks[1], (BATCH, D), 1.0),
        'w_ada': nrm(ks[2], (DEPTH, D, 6 * D), 0.02),
        'b_ada': nrm(ks[3], (DEPTH, 6 * D), 0.02),
        'g_norm_mix': 1.0 + nrm(ks[4], (DEPTH, D), 0.02),
        'g_norm_ffn': 1.0 + nrm(ks[5], (DEPTH, D), 0.02),
        'w_in': nrm(ks[6], (DEPTH, D, D_IN), D ** -0.5),
        'w_conv_qk': nrm(ks[7], (DEPTH, CONV_WIDTH, 2 * D_MLSTM), CONV_WIDTH ** -0.5),
        'b_conv_qk': nrm(ks[8], (DEPTH, 2 * D_MLSTM), 0.02),
        'b_gates': jnp.concatenate([b_in, b_fg], axis=-1),
        'w_conv_lru': nrm(ks[9], (DEPTH, CONV_WIDTH, D_LRU), CONV_WIDTH ** -0.5),
        'b_conv_lru': nrm(ks[10], (DEPTH, D_LRU), 0.02),
        'w_lru_a': nrm(ks[11], (DEPTH, N_LRU_BLOCKS, DB_LRU, DB_LRU), DB_LRU ** -0.5),
        'b_lru_a': nrm(ks[12], (DEPTH, D_LRU), 0.02),
        'w_lru_x': nrm(ks[13], (DEPTH, N_LRU_BLOCKS, DB_LRU, DB_LRU), DB_LRU ** -0.5),
        'b_lru_x': nrm(ks[14], (DEPTH, D_LRU), 0.02),
        'lru_lambda': jnp.log(a0) - jnp.log1p(-a0),
        'g_mix_out': 1.0 + nrm(ks[15], (DEPTH, D_MIX), 0.02),
        'w_out': nrm(ks[16], (DEPTH, D_MIX, D), D_MIX ** -0.5),
        'w_ff_gate': nrm(ks[17], (N_DENSE, D, D_FF), D ** -0.5),
        'w_ff_up': nrm(ks[18], (N_DENSE, D, D_FF), D ** -0.5),
        'w_ff_down': nrm(ks[19], (N_DENSE, D_FF, D), D_FF ** -0.5),
        'w_router': nrm(ks[23], (N_MOE, D, N_EXPERTS), D ** -0.5),
        'b_router': nrm(ks[24], (N_MOE, N_EXPERTS), 0.01),
        'w_exp_gate': nrm(ks[25], (N_MOE, N_EXPERTS, D, D_FF), D ** -0.5),
        'w_exp_up': nrm(ks[26], (N_MOE, N_EXPERTS, D, D_FF), D ** -0.5),
        'w_exp_down': nrm(ks[27], (N_MOE, N_EXPERTS, D_FF, D), D_FF ** -0.5),
        'g_final': 1.0 + nrm(ks[28], (D,), 0.02),
    }


def reference(x, c, w_ada, b_ada, g_norm_mix, g_norm_ffn, w_in, w_conv_qk, b_conv_qk, b_gates,
              w_conv_lru, b_conv_lru, w_lru_a, b_lru_a, w_lru_x, b_lru_x, lru_lambda, g_mix_out, w_out,
              w_ff_gate, w_ff_up, w_ff_down, w_router, b_router, w_exp_gate, w_exp_up, w_exp_down, g_final):
    cond = jax.nn.silu(c)
    for l in range(DEPTH):
        mod = cond @ w_ada[l] + b_ada[l]
        sh_m, sc_m, gt_m, sh_f, sc_f, gt_f = jnp.split(mod[:, None, :], 6, axis=-1)
        h = _rmsnorm(x, g_norm_mix[l]) * (1 + sc_m) + sh_m
        x = x + gt_m * _hybrid_mixer(h, w_in[l], w_conv_qk[l], b_conv_qk[l], b_gates[l],
                                      w_conv_lru[l], b_conv_lru[l], w_lru_a[l], b_lru_a[l],
                                      w_lru_x[l], b_lru_x[l], lru_lambda[l], g_mix_out[l], w_out[l])
        h = _rmsnorm(x, g_norm_ffn[l]) * (1 + sc_f) + sh_f
        j = l // 2
        if l % 2 == 0:
            y = _swiglu(h, w_ff_gate[j], w_ff_up[j], w_ff_down[j])
        else:
            y = _moe(h, w_router[j], b_router[j], w_exp_gate[j], w_exp_up[j], w_exp_down[j])
        x = x + gt_f * y
    return _rmsnorm(x, g_final)
```

```python
import functools

import jax
import jax.numpy as jnp
from jax import lax
from jax.experimental import pallas as pl
from jax.experimental.pallas import tpu as pltpu

F32 = jnp.float32
BF16 = jnp.bfloat16

D_MODEL = 1024
BATCH = 4
SEQ = 8192
TOKENS = BATCH * SEQ
DEPTH = 2
D_MLSTM = 512
N_HEADS = 4
DH = 128
D_LRU = 512
N_LRU_BLOCKS = 8
DB_LRU = 64
CONV_W = 4
LRU_C = 8.0
D_FF = 2816
N_EXPERTS = 8
EPS = 1e-6

LANES = 128
SUBLANES = 8
VMEM_LIMIT = 56 * 1024 * 1024

TM_PROJ = 512
L_MIX = 256
TM_FFN = 512
TF_FFN = 1408


def _sigmoid(x):
    return 1.0 / (1.0 + jnp.exp(-x))


def _mod_norm(x, g, scale, shift):
    ms = jnp.mean(x * x, axis=-1, keepdims=True)
    return (x * lax.rsqrt(ms + EPS)) * g * (1.0 + scale) + shift


def _shift_rows(x, tail, s):
    rx = pltpu.roll(x, s, 0)
    rt = pltpu.roll(tail, s, 0)
    row = lax.broadcasted_iota(jnp.int32, tail.shape, 0)
    head = jnp.where(row < s, rt, rx[:SUBLANES])
    return jnp.concatenate([head, rx[SUBLANES:]], axis=0)


def _causal_conv(x, tail, w, b):
    acc = b + w[CONV_W - 1:CONV_W] * x
    for s in range(1, CONV_W):
        acc = acc + w[CONV_W - 1 - s:CONV_W - s] * _shift_rows(x, tail, s)
    return acc


def _ada_kernel(c_ref, w_ref, b_ref, o_ref):
    c = c_ref[...]
    cs = c * _sigmoid(c)
    o_ref[...] = jnp.dot(cs, w_ref[...], preferred_element_type=F32) + b_ref[...]


def _ada(c, w_ada, b_ada):
    tn = 1536
    c8 = jnp.zeros((SUBLANES, D_MODEL), F32).at[:BATCH].set(c)
    out = pl.pallas_call(
        _ada_kernel,
        grid=(DEPTH, 6 * D_MODEL // tn),
        in_specs=[
            pl.BlockSpec((SUBLANES, D_MODEL), lambda l, n: (0, 0)),
            pl.BlockSpec((None, D_MODEL, tn), lambda l, n: (l, 0, n)),
            pl.BlockSpec((None, 1, tn), lambda l, n: (l, 0, n)),
        ],
        out_specs=pl.BlockSpec((None, SUBLANES, tn), lambda l, n: (l, 0, n)),
        out_shape=jax.ShapeDtypeStruct((DEPTH, SUBLANES, 6 * D_MODEL), F32),
        compiler_params=pltpu.CompilerParams(
            dimension_semantics=("arbitrary", "arbitrary"), vmem_limit_bytes=VMEM_LIMIT),
        name="ada",
    )(c8, w_ada, b_ada.reshape(DEPTH, 1, 6 * D_MODEL))
    return out[:, :BATCH].reshape(DEPTH, BATCH, 6, D_MODEL)


def _inproj_kernel(x_ref, mod_ref, g_ref, w1_ref, w2_ref, w3_ref, zq_ref, zl_ref, zg_ref):
    mod = mod_ref[...]
    h = _mod_norm(x_ref[...], g_ref[...], mod[1:2], mod[0:1]).astype(BF16)
    zq_ref[...] = jnp.dot(h, w1_ref[...], preferred_element_type=F32).astype(BF16)
    zl_ref[...] = jnp.dot(h, w2_ref[...], preferred_element_type=F32).astype(BF16)
    zg_ref[...] = jnp.dot(h, w3_ref[...], preferred_element_type=F32)


def _inproj(x, mod, g, w1, w2, w3):
    tm = TM_PROJ
    per_b = SEQ // tm
    return pl.pallas_call(
        _inproj_kernel,
        grid=(TOKENS // tm,),
        in_specs=[
            pl.BlockSpec((tm, D_MODEL), lambda i: (i, 0)),
            pl.BlockSpec((None, 6, D_MODEL), lambda i: (i // per_b, 0, 0)),
            pl.BlockSpec((1, D_MODEL), lambda i: (0, 0)),
            pl.BlockSpec(w1.shape, lambda i: (0, 0)),
            pl.BlockSpec(w2.shape, lambda i: (0, 0)),
            pl.BlockSpec(w3.shape, lambda i: (0, 0)),
        ],
        out_specs=[
            pl.BlockSpec((tm, 4 * D_MLSTM), lambda i: (i, 0)),
            pl.BlockSpec((tm, 2 * D_LRU), lambda i: (i, 0)),
            pl.BlockSpec((tm, LANES), lambda i: (i, 0)),
        ],
        out_shape=[
            jax.ShapeDtypeStruct((TOKENS, 4 * D_MLSTM), BF16),
            jax.ShapeDtypeStruct((TOKENS, 2 * D_LRU), BF16),
            jax.ShapeDtypeStruct((TOKENS, LANES), F32),
        ],
        compiler_params=pltpu.CompilerParams(
            dimension_semantics=("arbitrary",), vmem_limit_bytes=VMEM_LIMIT),
        name="inproj",
    )(x, mod, g, w1, w2, w3)


def _mlstm_kernel(zq_ref, zg_ref, wc_ref, bc_ref, bg_ref, gm_ref, tri_ref, hm_ref,
                  ct_ref, m_ref, tail_ref):
    L = L_MIX

    @pl.when(pl.program_id(1) == 0)
    def _():
        ct_ref[...] = jnp.zeros_like(ct_ref)
        m_ref[...] = jnp.zeros_like(m_ref)
        tail_ref[...] = jnp.zeros_like(tail_ref)

    qk_raw = zq_ref[:, :2 * D_MLSTM].astype(F32)
    qk = _causal_conv(qk_raw, tail_ref[...], wc_ref[...], bc_ref[...])
    tail_ref[...] = qk_raw[L - SUBLANES:, :]
    qk = qk * _sigmoid(qk)
    q = qk[:, :D_MLSTM].astype(BF16)
    k = (qk[:, D_MLSTM:] * (DH ** -0.5)).astype(BF16)

    g = zg_ref[...] + bg_ref[...]
    lf = jnp.minimum(g, 0.0) - jnp.log1p(jnp.exp(-jnp.abs(g)))
    tri = tri_ref[...]
    hi = lf.astype(BF16)
    r1 = lf - hi.astype(F32)
    mid = r1.astype(BF16)
    lo = (r1 - mid.astype(F32)).astype(BF16)
    bcum = (jnp.dot(tri, hi, preferred_element_type=F32)
            + jnp.dot(tri, mid, preferred_element_type=F32)
            + jnp.dot(tri, lo, preferred_element_type=F32))
    lane = lax.broadcasted_iota(jnp.int32, (L, LANES), 1)
    cols = jnp.where(lane < N_HEADS, g, bcum)
    rows = jnp.transpose(cols)

    t_idx = lax.broadcasted_iota(jnp.int32, (L, L), 0)
    s_idx = lax.broadcasted_iota(jnp.int32, (L, L), 1)
    causal = s_idx <= t_idx
    ones_col = (lax.broadcasted_iota(jnp.int32, (L, DH), 1) == 0).astype(BF16)
    gm = gm_ref[...]

    for h in range(N_HEADS):
        sl = slice(h * DH, (h + 1) * DH)
        ic = cols[:, h:h + 1]
        bc = cols[:, N_HEADS + h:N_HEADS + h + 1]
        ir = rows[h:h + 1, :]
        br = rows[N_HEADS + h:N_HEADS + h + 1, :]
        b_last = br[:, L - 1:L]
        m_prev = m_ref[h:h + 1, 0:1]

        dm = jnp.where(causal, bc - br + ir, -jnp.inf)
        inter_log = bc + m_prev
        m_t = jnp.maximum(inter_log, jnp.max(dm, axis=1, keepdims=True))
        qh = q[:, sl]
        kh = k[:, sl]
        vh = zq_ref[:, 2 * D_MLSTM + h * DH:2 * D_MLSTM + (h + 1) * DH]
        vaug = jnp.concatenate([vh, ones_col], axis=1)
        sc = lax.dot_general(qh, kh, (((1,), (1,)), ((), ())), preferred_element_type=F32)
        p = (sc * jnp.exp(dm - m_t)).astype(BF16)
        ct = ct_ref[h]
        out = (jnp.dot(p, vaug, preferred_element_type=F32)
               + jnp.exp(inter_log - m_t)
               * jnp.dot(qh, ct.astype(BF16), preferred_element_type=F32))
        num = out[:, :DH]
        den = out[:, DH:DH + 1]
        hh = num / jnp.maximum(jnp.abs(den), jnp.exp(-m_t))
        hn = hh * lax.rsqrt(jnp.mean(hh * hh, axis=-1, keepdims=True) + EPS) * gm[:, sl]
        og = zq_ref[:, 3 * D_MLSTM + h * DH:3 * D_MLSTM + (h + 1) * DH].astype(F32)
        hm_ref[:, sl] = (hn * _sigmoid(og)).astype(BF16)

        w_end = b_last - bc + ic
        m_loc = jnp.max(w_end, axis=0, keepdims=True)
        e_end = jnp.exp(w_end - m_loc)
        m_new = jnp.maximum(b_last + m_prev, m_loc)
        ev = (e_end * vaug.astype(F32)).astype(BF16)
        c_loc = lax.dot_general(kh, ev, (((0,), (0,)), ((), ())), preferred_element_type=F32)
        ct_ref[h] = jnp.exp(b_last + m_prev - m_new) * ct + jnp.exp(m_loc - m_new) * c_loc
        m_ref[h:h + 1, :] = jnp.broadcast_to(m_new, (1, LANES))


def _mlstm(zq, zg, w_conv, b_conv, b_gates, g_mix, tri):
    L = L_MIX
    nc = SEQ // L
    return pl.pallas_call(
        _mlstm_kernel,
        grid=(BATCH, nc),
        in_specs=[
            pl.BlockSpec((L, 4 * D_MLSTM), lambda b, j: (b * nc + j, 0)),
            pl.BlockSpec((L, LANES), lambda b, j: (b * nc + j, 0)),
            pl.BlockSpec((CONV_W, 2 * D_MLSTM), lambda b, j: (0, 0)),
            pl.BlockSpec((1, 2 * D_MLSTM), lambda b, j: (0, 0)),
            pl.BlockSpec((1, LANES), lambda b, j: (0, 0)),
            pl.BlockSpec((1, D_MLSTM), lambda b, j: (0, 0)),
            pl.BlockSpec((L, L), lambda b, j: (0, 0)),
        ],
        out_specs=pl.BlockSpec((L, D_MLSTM), lambda b, j: (b * nc + j, 0)),
        out_shape=jax.ShapeDtypeStruct((TOKENS, D_MLSTM), BF16),
        scratch_shapes=[
            pltpu.VMEM((N_HEADS, DH, 2 * DH), F32),
            pltpu.VMEM((SUBLANES, LANES), F32),
            pltpu.VMEM((SUBLANES, 2 * D_MLSTM), F32),
        ],
        compiler_params=pltpu.CompilerParams(
            dimension_semantics=("arbitrary", "arbitrary"), vmem_limit_bytes=VMEM_LIMIT),
        name="mlstm",
    )(zq, zg, w_conv, b_conv, b_gates, g_mix, tri)


def _lru_kernel(zl_ref, wc_ref, bc_ref, wax_ref, bax_ref, lam_ref, gm_ref, hl_ref,
                tail_ref, hc_ref):
    L = L_MIX

    @pl.when(pl.program_id(1) == 0)
    def _():
        tail_ref[...] = jnp.zeros_like(tail_ref)
        hc_ref[...] = jnp.zeros_like(hc_ref)

    xl = zl_ref[:, :D_LRU].astype(F32)
    gl = zl_ref[:, D_LRU:].astype(F32)
    xc = _causal_conv(xl, tail_ref[...], wc_ref[...], bc_ref[...])
    tail_ref[...] = xl[L - SUBLANES:, :]

    gates = jnp.dot(xc.astype(BF16), wax_ref[...], preferred_element_type=F32) + bax_ref[...]
    r = _sigmoid(gates[:, :D_LRU])
    i = _sigmoid(gates[:, D_LRU:])
    lam = lam_ref[...]
    sp = jnp.maximum(-lam, 0.0) + jnp.log1p(jnp.exp(-jnp.abs(lam)))
    log_a = -LRU_C * r * sp
    a = jnp.exp(log_a)
    y2 = 2.0 * log_a
    series = -y2 * (1.0 + y2 * (0.5 + y2 * (1.0 / 6.0 + y2 * (1.0 / 24.0))))
    u = jnp.sqrt(jnp.where(y2 > -0.01, series, 1.0 - a * a)) * (i * xc)

    r8 = lax.broadcasted_iota(jnp.int32, (L, D_LRU), 0) & (SUBLANES - 1)
    for s in (1, 2, 4):
        a_sh = pltpu.roll(a, s, 0)
        u_sh = pltpu.roll(u, s, 0)
        valid = r8 >= s
        u = jnp.where(valid, a * u_sh + u, u)
        a = jnp.where(valid, a * a_sh, a)
    hc = hc_ref[...]
    hs = []
    for gi in range(L // SUBLANES):
        sl = slice(gi * SUBLANES, (gi + 1) * SUBLANES)
        hg = u[sl] + a[sl] * hc
        hs.append(hg)
        hc = hg[SUBLANES - 1:SUBLANES]
    hc_ref[...] = hc
    hseq = jnp.concatenate(hs, axis=0)

    gelu = 0.5 * gl * (1.0 + jnp.tanh(0.7978845608028654 * (gl + 0.044715 * gl * gl * gl)))
    hl = hseq * gelu
    hl = hl * lax.rsqrt(jnp.mean(hl * hl, axis=-1, keepdims=True) + EPS) * gm_ref[...]
    hl_ref[...] = hl.astype(BF16)


def _lru(zl, w_conv, b_conv, wax, bax, lam, g_mix):
    L = L_MIX
    nc = SEQ // L
    return pl.pallas_call(
        _lru_kernel,
        grid=(BATCH, nc),
        in_specs=[
            pl.BlockSpec((L, 2 * D_LRU), lambda b, j: (b * nc + j, 0)),
            pl.BlockSpec((CONV_W, D_LRU), lambda b, j: (0, 0)),
            pl.BlockSpec((1, D_LRU), lambda b, j: (0, 0)),
            pl.BlockSpec((D_LRU, 2 * D_LRU), lambda b, j: (0, 0)),
            pl.BlockSpec((1, 2 * D_LRU), lambda b, j: (0, 0)),
            pl.BlockSpec((1, D_LRU), lambda b, j: (0, 0)),
            pl.BlockSpec((1, D_LRU), lambda b, j: (0, 0)),
        ],
        out_specs=pl.BlockSpec((L, D_LRU), lambda b, j: (b * nc + j, 0)),
        out_shape=jax.ShapeDtypeStruct((TOKENS, D_LRU), BF16),
        scratch_shapes=[
            pltpu.VMEM((SUBLANES, D_LRU), F32),
            pltpu.VMEM((1, D_LRU), F32),
        ],
        compiler_params=pltpu.CompilerParams(
            dimension_semantics=("arbitrary", "arbitrary"), vmem_limit_bytes=VMEM_LIMIT),
        name="rglru",
    )(zl, w_conv, b_conv, wax, bax, lam, g_mix)


def _outproj_kernel(hm_ref, hl_ref, x_ref, mod_ref, wm_ref, wl_ref, o_ref):
    y = (jnp.dot(hm_ref[...], wm_ref[...], preferred_element_type=F32)
         + jnp.dot(hl_ref[...], wl_ref[...], preferred_element_type=F32))
    o_ref[...] = x_ref[...] + mod_ref[2:3, :] * y


def _outproj(hm, hl, x, mod, wm, wl):
    tm = TM_PROJ
    per_b = SEQ // tm
    return pl.pallas_call(
        _outproj_kernel,
        grid=(TOKENS // tm,),
        in_specs=[
            pl.BlockSpec((tm, D_MLSTM), lambda i: (i, 0)),
            pl.BlockSpec((tm, D_LRU), lambda i: (i, 0)),
            pl.BlockSpec((tm, D_MODEL), lambda i: (i, 0)),
            pl.BlockSpec((None, 6, D_MODEL), lambda i: (i // per_b, 0, 0)),
            pl.BlockSpec((D_MLSTM, D_MODEL), lambda i: (0, 0)),
            pl.BlockSpec((D_LRU, D_MODEL), lambda i: (0, 0)),
        ],
        out_specs=pl.BlockSpec((tm, D_MODEL), lambda i: (i, 0)),
        out_shape=jax.ShapeDtypeStruct((TOKENS, D_MODEL), F32),
        compiler_params=pltpu.CompilerParams(
            dimension_semantics=("arbitrary",), vmem_limit_bytes=VMEM_LIMIT),
        name="outproj",
    )(hm, hl, x, mod, wm, wl)


def _ffn_kernel(*refs, moe, final):
    x_ref, mod_ref, g_ref = refs[:3]
    refs = refs[3:]
    if moe:
        wrh_ref, wrl_ref, br_ref = refs[:3]
        refs = refs[3:]
    wg_ref, wu_ref, wd_ref = refs[:3]
    refs = refs[3:]
    if final:
        gf_ref = refs[0]
        refs = refs[1:]
    o_ref, h2_ref, acc_ref = refs[:3]
    if moe:
        gate_ref = refs[3]

    e = pl.program_id(1)
    j = pl.program_id(2)
    tm = x_ref.shape[0]
    lane = lax.broadcasted_iota(jnp.int32, (tm, LANES), 1)

    @pl.when((e == 0) & (j == 0))
    def _():
        h = _mod_norm(x_ref[...], g_ref[...], mod_ref[4:5, :], mod_ref[3:4, :])
        hb = h.astype(BF16)
        h2_ref[...] = hb
        acc_ref[...] = jnp.zeros_like(acc_ref)
        if moe:
            hlo = (h - hb.astype(F32)).astype(BF16)
            wrh = wrh_ref[...]
            logits = (jnp.dot(hb, wrh, preferred_element_type=F32)
                      + jnp.dot(hlo, wrh, preferred_element_type=F32)
                      + jnp.dot(hb, wrl_ref[...], preferred_element_type=F32)) + br_ref[...]
            logits = jnp.where(lane < N_EXPERTS, logits, -jnp.inf)
            m1 = jnp.max(logits, axis=-1, keepdims=True)
            i1 = jnp.min(jnp.where(logits == m1, lane, LANES), axis=-1, keepdims=True)
            rest = jnp.where(lane == i1, -jnp.inf, logits)
            m2 = jnp.max(rest, axis=-1, keepdims=True)
            i2 = jnp.min(jnp.where(rest == m2, lane, LANES), axis=-1, keepdims=True)
            e2 = jnp.exp(m2 - m1)
            p1 = 1.0 / (1.0 + e2)
            gate_ref[...] = jnp.where(lane == i1, p1, 0.0) + jnp.where(lane == i2, e2 * p1, 0.0)

    hb = h2_ref[...]
    g = jnp.dot(hb, wg_ref[...], preferred_element_type=F32)
    u = jnp.dot(hb, wu_ref[...], preferred_element_type=F32)
    act = g * _sigmoid(g) * u
    if moe:
        act = act * jnp.sum(jnp.where(lane == e, gate_ref[...], 0.0), axis=-1, keepdims=True)
    acc_ref[...] += jnp.dot(act.astype(BF16), wd_ref[...], preferred_element_type=F32)

    @pl.when((e == pl.num_programs(1) - 1) & (j == pl.num_programs(2) - 1))
    def _():
        y = x_ref[...] + mod_ref[5:6, :] * acc_ref[...]
        if final:
            y = y * lax.rsqrt(jnp.mean(y * y, axis=-1, keepdims=True) + EPS) * gf_ref[...]
        o_ref[...] = y


def _ffn(x, mod, g, wg, wu, wd, router=None, g_final=None):
    tm, tf = TM_FFN, TF_FFN
    per_b = SEQ // tm
    n_e = wg.shape[0]
    moe = router is not None
    final = g_final is not None
    row = lambda shape: pl.BlockSpec(shape, lambda i, e, j: (0, 0))
    in_specs = [
        pl.BlockSpec((tm, D_MODEL), lambda i, e, j: (i, 0)),
        pl.BlockSpec((None, 6, D_MODEL), lambda i, e, j: (i // per_b, 0, 0)),
        row((1, D_MODEL)),
    ]
    args = [x, mod, g]
    if moe:
        in_specs += [row((D_MODEL, LANES)), row((D_MODEL, LANES)), row((1, LANES))]
        args += list(router)
    in_specs += [
        pl.BlockSpec((None, D_MODEL, tf), lambda i, e, j: (e, 0, j)),
        pl.BlockSpec((None, D_MODEL, tf), lambda i, e, j: (e, 0, j)),
        pl.BlockSpec((None, tf, D_MODEL), lambda i, e, j: (e, j, 0)),
    ]
    args += [wg, wu, wd]
    if final:
        in_specs.append(row((1, D_MODEL)))
        args.append(g_final)
    scratch = [pltpu.VMEM((tm, D_MODEL), BF16), pltpu.VMEM((tm, D_MODEL), F32)]
    if moe:
        scratch.append(pltpu.VMEM((tm, LANES), F32))
    return pl.pallas_call(
        functools.partial(_ffn_kernel, moe=moe, final=final),
        grid=(TOKENS // tm, n_e, D_FF // tf),
        in_specs=in_specs,
        out_specs=pl.BlockSpec((tm, D_MODEL), lambda i, e, j: (i, 0)),
        out_shape=jax.ShapeDtypeStruct((TOKENS, D_MODEL), F32),
        scratch_shapes=scratch,
        compiler_params=pltpu.CompilerParams(
            dimension_semantics=("arbitrary", "arbitrary", "arbitrary"),
            vmem_limit_bytes=VMEM_LIMIT),
        name="moe" if moe else "ffn",
    )(*args)


def _block_diag(w):
    eye = jnp.eye(N_LRU_BLOCKS, dtype=w.dtype)
    return jnp.einsum('nde,nm->ndme', w, eye).reshape(D_LRU, D_LRU)


def _pad_lanes(a):
    return jnp.zeros(a.shape[:-1] + (LANES,), a.dtype).at[..., :a.shape[-1]].set(a)


def kernel(x, c, w_ada, b_ada, g_norm_mix, g_norm_ffn, w_in, w_conv_qk, b_conv_qk, b_gates,
           w_conv_lru, b_conv_lru, w_lru_a, b_lru_a, w_lru_x, b_lru_x, lru_lambda, g_mix_out, w_out,
           w_ff_gate, w_ff_up, w_ff_down, w_router, b_router, w_exp_gate, w_exp_up, w_exp_down, g_final):
    mods = _ada(c, w_ada, b_ada)
    tri = (lax.broadcasted_iota(jnp.int32, (L_MIX, L_MIX), 1)
           <= lax.broadcasted_iota(jnp.int32, (L_MIX, L_MIX), 0)).astype(BF16)
    xt = x.reshape(TOKENS, D_MODEL)
    for l in range(DEPTH):
        mod = mods[l]
        wi = w_in[l]
        w1 = wi[:, :4 * D_MLSTM].astype(BF16)
        w3 = _pad_lanes(wi[:, 4 * D_MLSTM:4 * D_MLSTM + 2 * N_HEADS]).astype(BF16)
        w2 = wi[:, 4 * D_MLSTM + 2 * N_HEADS:].astype(BF16)
        zq, zl, zg = _inproj(xt, mod, g_norm_mix[l][None], w1, w2, w3)
        hm = _mlstm(zq, zg, w_conv_qk[l], b_conv_qk[l][None], _pad_lanes(b_gates[l][None]),
                    g_mix_out[l][None, :D_MLSTM], tri)
        wax = jnp.concatenate([_block_diag(w_lru_a[l]), _block_diag(w_lru_x[l])], axis=1).astype(BF16)
        bax = jnp.concatenate([b_lru_a[l], b_lru_x[l]])[None]
        hl = _lru(zl, w_conv_lru[l], b_conv_lru[l][None], wax, bax, lru_lambda[l][None],
                  g_mix_out[l][None, D_MLSTM:])
        wo = w_out[l].astype(BF16)
        xt = _outproj(hm, hl, xt, mod, wo[:D_MLSTM], wo[D_MLSTM:])
        jj = l // 2
        gf = g_final[None] if l == DEPTH - 1 else None
        if l % 2 == 0:
            xt = _ffn(xt, mod, g_norm_ffn[l][None], w_ff_gate[jj][None].astype(BF16),
                      w_ff_up[jj][None].astype(BF16), w_ff_down[jj][None].astype(BF16), g_final=gf)
        else:
            wr = _pad_lanes(w_router[jj])
            wrh = wr.astype(BF16)
            wrl = (wr - wrh.astype(F32)).astype(BF16)
            router = (wrh, wrl, _pad_lanes(b_router[jj][None]))
            xt = _ffn(xt, mod, g_norm_ffn[l][None], w_exp_gate[jj].astype(BF16),
                      w_exp_up[jj].astype(BF16), w_exp_down[jj].astype(BF16),
                      router=router, g_final=gf)
    return xt.reshape(BATCH, SEQ, D_MODEL)
```

```python
import functools

import jax
import jax.numpy as jnp
from jax import lax
from jax.experimental import pallas as pl
from jax.experimental.pallas import tpu as pltpu

F32 = jnp.float32
BF16 = jnp.bfloat16

D_MODEL = 1024
BATCH = 4
SEQ = 8192
TOKENS = BATCH * SEQ
DEPTH = 2
D_MLSTM = 512
N_HEADS = 4
DH = 128
D_LRU = 512
N_LRU_BLOCKS = 8
DB_LRU = 64
CONV_W = 4
LRU_C = 8.0
D_FF = 2816
N_EXPERTS = 8
EPS = 1e-6

LANES = 128
SUBLANES = 8
VMEM_LIMIT = 56 * 1024 * 1024

TM_PROJ = 512
L_MIX = 256
TM_FFN = 512
TM_ROUTE = 512
TM_ROWS = 512
TM_SLOT = 512
N_SLOTS = 2 * TOKENS + N_EXPERTS * TM_SLOT
TF_FFN = 1408


def _sigmoid(x):
    return 1.0 / (1.0 + jnp.exp(-x))


def _mod_norm(x, g, scale, shift):
    ms = jnp.mean(x * x, axis=-1, keepdims=True)
    return (x * lax.rsqrt(ms + EPS)) * g * (1.0 + scale) + shift


def _shift_rows(x, tail, s):
    rx = pltpu.roll(x, s, 0)
    rt = pltpu.roll(tail, s, 0)
    row = lax.broadcasted_iota(jnp.int32, tail.shape, 0)
    head = jnp.where(row < s, rt, rx[:SUBLANES])
    return jnp.concatenate([head, rx[SUBLANES:]], axis=0)


def _causal_conv(x, tail, w, b):
    acc = b + w[CONV_W - 1:CONV_W] * x
    for s in range(1, CONV_W):
        acc = acc + w[CONV_W - 1 - s:CONV_W - s] * _shift_rows(x, tail, s)
    return acc


def _ada_kernel(c_ref, w_ref, b_ref, o_ref):
    c = c_ref[...]
    cs = c * _sigmoid(c)
    o_ref[...] = jnp.dot(cs, w_ref[...], preferred_element_type=F32) + b_ref[...]


def _ada(c, w_ada, b_ada):
    tn = 1536
    c8 = jnp.zeros((SUBLANES, D_MODEL), F32).at[:BATCH].set(c)
    out = pl.pallas_call(
        _ada_kernel,
        grid=(DEPTH, 6 * D_MODEL // tn),
        in_specs=[
            pl.BlockSpec((SUBLANES, D_MODEL), lambda l, n: (0, 0)),
            pl.BlockSpec((None, D_MODEL, tn), lambda l, n: (l, 0, n)),
            pl.BlockSpec((None, 1, tn), lambda l, n: (l, 0, n)),
        ],
        out_specs=pl.BlockSpec((None, SUBLANES, tn), lambda l, n: (l, 0, n)),
        out_shape=jax.ShapeDtypeStruct((DEPTH, SUBLANES, 6 * D_MODEL), F32),
        compiler_params=pltpu.CompilerParams(
            dimension_semantics=("arbitrary", "arbitrary"), vmem_limit_bytes=VMEM_LIMIT),
        name="ada",
    )(c8, w_ada, b_ada.reshape(DEPTH, 1, 6 * D_MODEL))
    return out[:, :BATCH].reshape(DEPTH, BATCH, 6, D_MODEL)


def _inproj_kernel(x_ref, mod_ref, g_ref, w1_ref, w2_ref, w3_ref, zq_ref, zl_ref, zg_ref):
    mod = mod_ref[...]
    h = _mod_norm(x_ref[...], g_ref[...], mod[1:2], mod[0:1]).astype(BF16)
    zq_ref[...] = jnp.dot(h, w1_ref[...], preferred_element_type=F32).astype(BF16)
    zl_ref[...] = jnp.dot(h, w2_ref[...], preferred_element_type=F32).astype(BF16)
    zg_ref[...] = jnp.dot(h, w3_ref[...], preferred_element_type=F32)


def _inproj(x, mod, g, w1, w2, w3):
    tm = TM_PROJ
    per_b = SEQ // tm
    return pl.pallas_call(
        _inproj_kernel,
        grid=(TOKENS // tm,),
        in_specs=[
            pl.BlockSpec((tm, D_MODEL), lambda i: (i, 0)),
            pl.BlockSpec((None, 6, D_MODEL), lambda i: (i // per_b, 0, 0)),
            pl.BlockSpec((1, D_MODEL), lambda i: (0, 0)),
            pl.BlockSpec(w1.shape, lambda i: (0, 0)),
            pl.BlockSpec(w2.shape, lambda i: (0, 0)),
            pl.BlockSpec(w3.shape, lambda i: (0, 0)),
        ],
        out_specs=[
            pl.BlockSpec((tm, 4 * D_MLSTM), lambda i: (i, 0)),
            pl.BlockSpec((tm, 2 * D_LRU), lambda i: (i, 0)),
            pl.BlockSpec((tm, LANES), lambda i: (i, 0)),
        ],
        out_shape=[
            jax.ShapeDtypeStruct((TOKENS, 4 * D_MLSTM), BF16),
            jax.ShapeDtypeStruct((TOKENS, 2 * D_LRU), BF16),
            jax.ShapeDtypeStruct((TOKENS, LANES), F32),
        ],
        compiler_params=pltpu.CompilerParams(
            dimension_semantics=("arbitrary",), vmem_limit_bytes=VMEM_LIMIT),
        name="inproj",
    )(x, mod, g, w1, w2, w3)


def _mlstm_kernel(zq_ref, zg_ref, wc_ref, bc_ref, bg_ref, gm_ref, tri_ref, hm_ref,
                  ct_ref, m_ref, tail_ref):
    L = L_MIX

    @pl.when(pl.program_id(1) == 0)
    def _():
        ct_ref[...] = jnp.zeros_like(ct_ref)
        m_ref[...] = jnp.zeros_like(m_ref)
        tail_ref[...] = jnp.zeros_like(tail_ref)

    qk_raw = zq_ref[:, :2 * D_MLSTM].astype(F32)
    qk = _causal_conv(qk_raw, tail_ref[...], wc_ref[...], bc_ref[...])
    tail_ref[...] = qk_raw[L - SUBLANES:, :]
    qk = qk * _sigmoid(qk)
    q = qk[:, :D_MLSTM].astype(BF16)
    k = (qk[:, D_MLSTM:] * (DH ** -0.5)).astype(BF16)

    g = zg_ref[...] + bg_ref[...]
    lf = jnp.minimum(g, 0.0) - jnp.log1p(jnp.exp(-jnp.abs(g)))
    tri = tri_ref[...]
    hi = lf.astype(BF16)
    r1 = lf - hi.astype(F32)
    mid = r1.astype(BF16)
    lo = (r1 - mid.astype(F32)).astype(BF16)
    bcum = (jnp.dot(tri, hi, preferred_element_type=F32)
            + jnp.dot(tri, mid, preferred_element_type=F32)
            + jnp.dot(tri, lo, preferred_element_type=F32))
    lane = lax.broadcasted_iota(jnp.int32, (L, LANES), 1)
    cols = jnp.where(lane < N_HEADS, g, bcum)
    rows = jnp.transpose(cols)

    t_idx = lax.broadcasted_iota(jnp.int32, (L, L), 0)
    s_idx = lax.broadcasted_iota(jnp.int32, (L, L), 1)
    causal = s_idx <= t_idx
    ones_col = (lax.broadcasted_iota(jnp.int32, (L, DH), 1) == 0).astype(BF16)
    gm = gm_ref[...]

    for h in range(N_HEADS):
        sl = slice(h * DH, (h + 1) * DH)
        ic = cols[:, h:h + 1]
        bc = cols[:, N_HEADS + h:N_HEADS + h + 1]
        ir = rows[h:h + 1, :]
        br = rows[N_HEADS + h:N_HEADS + h + 1, :]
        b_last = br[:, L - 1:L]
        m_prev = m_ref[h:h + 1, 0:1]

        dm = jnp.where(causal, bc - br + ir, -jnp.inf)
        inter_log = bc + m_prev
        m_t = jnp.maximum(inter_log, jnp.max(dm, axis=1, keepdims=True))
        qh = q[:, sl]
        kh = k[:, sl]
        vh = zq_ref[:, 2 * D_MLSTM + h * DH:2 * D_MLSTM + (h + 1) * DH]
        vaug = jnp.concatenate([vh, ones_col], axis=1)
        sc = lax.dot_general(qh, kh, (((1,), (1,)), ((), ())), preferred_element_type=F32)
        p = (sc * jnp.exp(dm - m_t)).astype(BF16)
        ct = ct_ref[h]
        out = (jnp.dot(p, vaug, preferred_element_type=F32)
               + jnp.exp(inter_log - m_t)
               * jnp.dot(qh, ct.astype(BF16), preferred_element_type=F32))
        num = out[:, :DH]
        den = out[:, DH:DH + 1]
        hh = num / jnp.maximum(jnp.abs(den), jnp.exp(-m_t))
        hn = hh * lax.rsqrt(jnp.mean(hh * hh, axis=-1, keepdims=True) + EPS) * gm[:, sl]
        og = zq_ref[:, 3 * D_MLSTM + h * DH:3 * D_MLSTM + (h + 1) * DH].astype(F32)
        hm_ref[:, sl] = (hn * _sigmoid(og)).astype(BF16)

        w_end = b_last - bc + ic
        m_loc = jnp.max(w_end, axis=0, keepdims=True)
        e_end = jnp.exp(w_end - m_loc)
        m_new = jnp.maximum(b_last + m_prev, m_loc)
        ev = (e_end * vaug.astype(F32)).astype(BF16)
        c_loc = lax.dot_general(kh, ev, (((0,), (0,)), ((), ())), preferred_element_type=F32)
        ct_ref[h] = jnp.exp(b_last + m_prev - m_new) * ct + jnp.exp(m_loc - m_new) * c_loc
        m_ref[h:h + 1, :] = jnp.broadcast_to(m_new, (1, LANES))


def _mlstm(zq, zg, w_conv, b_conv, b_gates, g_mix, tri):
    L = L_MIX
    nc = SEQ // L
    return pl.pallas_call(
        _mlstm_kernel,
        grid=(BATCH, nc),
        in_specs=[
            pl.BlockSpec((L, 4 * D_MLSTM), lambda b, j: (b * nc + j, 0)),
            pl.BlockSpec((L, LANES), lambda b, j: (b * nc + j, 0)),
            pl.BlockSpec((CONV_W, 2 * D_MLSTM), lambda b, j: (0, 0)),
            pl.BlockSpec((1, 2 * D_MLSTM), lambda b, j: (0, 0)),
            pl.BlockSpec((1, LANES), lambda b, j: (0, 0)),
            pl.BlockSpec((1, D_MLSTM), lambda b, j: (0, 0)),
            pl.BlockSpec((L, L), lambda b, j: (0, 0)),
        ],
        out_specs=pl.BlockSpec((L, D_MLSTM), lambda b, j: (b * nc + j, 0)),
        out_shape=jax.ShapeDtypeStruct((TOKENS, D_MLSTM), BF16),
        scratch_shapes=[
            pltpu.VMEM((N_HEADS, DH, 2 * DH), F32),
            pltpu.VMEM((SUBLANES, LANES), F32),
            pltpu.VMEM((SUBLANES, 2 * D_MLSTM), F32),
        ],
        compiler_params=pltpu.CompilerParams(
            dimension_semantics=("arbitrary", "arbitrary"), vmem_limit_bytes=VMEM_LIMIT),
        name="mlstm",
    )(zq, zg, w_conv, b_conv, b_gates, g_mix, tri)


def _lru_kernel(zl_ref, wc_ref, bc_ref, wax_ref, bax_ref, lam_ref, gm_ref, hl_ref,
                tail_ref, hc_ref):
    L = L_MIX

    @pl.when(pl.program_id(1) == 0)
    def _():
        tail_ref[...] = jnp.zeros_like(tail_ref)
        hc_ref[...] = jnp.zeros_like(hc_ref)

    xl = zl_ref[:, :D_LRU].astype(F32)
    gl = zl_ref[:, D_LRU:].astype(F32)
    xc = _causal_conv(xl, tail_ref[...], wc_ref[...], bc_ref[...])
    tail_ref[...] = xl[L - SUBLANES:, :]

    gates = jnp.dot(xc.astype(BF16), wax_ref[...], preferred_element_type=F32) + bax_ref[...]
    r = _sigmoid(gates[:, :D_LRU])
    i = _sigmoid(gates[:, D_LRU:])
    lam = lam_ref[...]
    sp = jnp.maximum(-lam, 0.0) + jnp.log1p(jnp.exp(-jnp.abs(lam)))
    log_a = -LRU_C * r * sp
    a = jnp.exp(log_a)
    y2 = 2.0 * log_a
    series = -y2 * (1.0 + y2 * (0.5 + y2 * (1.0 / 6.0 + y2 * (1.0 / 24.0))))
    u = jnp.sqrt(jnp.where(y2 > -0.01, series, 1.0 - a * a)) * (i * xc)

    r8 = lax.broadcasted_iota(jnp.int32, (L, D_LRU), 0) & (SUBLANES - 1)
    for s in (1, 2, 4):
        a_sh = pltpu.roll(a, s, 0)
        u_sh = pltpu.roll(u, s, 0)
        valid = r8 >= s
        u = jnp.where(valid, a * u_sh + u, u)
        a = jnp.where(valid, a * a_sh, a)
    hc = hc_ref[...]
    hs = []
    for gi in range(L // SUBLANES):
        sl = slice(gi * SUBLANES, (gi + 1) * SUBLANES)
        hg = u[sl] + a[sl] * hc
        hs.append(hg)
        hc = hg[SUBLANES - 1:SUBLANES]
    hc_ref[...] = hc
    hseq = jnp.concatenate(hs, axis=0)

    gelu = 0.5 * gl * (1.0 + jnp.tanh(0.7978845608028654 * (gl + 0.044715 * gl * gl * gl)))
    hl = hseq * gelu
    hl = hl * lax.rsqrt(jnp.mean(hl * hl, axis=-1, keepdims=True) + EPS) * gm_ref[...]
    hl_ref[...] = hl.astype(BF16)


def _lru(zl, w_conv, b_conv, wax, bax, lam, g_mix):
    L = L_MIX
    nc = SEQ // L
    return pl.pallas_call(
        _lru_kernel,
        grid=(BATCH, nc),
        in_specs=[
            pl.BlockSpec((L, 2 * D_LRU), lambda b, j: (b * nc + j, 0)),
            pl.BlockSpec((CONV_W, D_LRU), lambda b, j: (0, 0)),
            pl.BlockSpec((1, D_LRU), lambda b, j: (0, 0)),
            pl.BlockSpec((D_LRU, 2 * D_LRU), lambda b, j: (0, 0)),
            pl.BlockSpec((1, 2 * D_LRU), lambda b, j: (0, 0)),
            pl.BlockSpec((1, D_LRU), lambda b, j: (0, 0)),
            pl.BlockSpec((1, D_LRU), lambda b, j: (0, 0)),
        ],
        out_specs=pl.BlockSpec((L, D_LRU), lambda b, j: (b * nc + j, 0)),
        out_shape=jax.ShapeDtypeStruct((TOKENS, D_LRU), BF16),
        scratch_shapes=[
            pltpu.VMEM((SUBLANES, D_LRU), F32),
            pltpu.VMEM((1, D_LRU), F32),
        ],
        compiler_params=pltpu.CompilerParams(
            dimension_semantics=("arbitrary", "arbitrary"), vmem_limit_bytes=VMEM_LIMIT),
        name="rglru",
    )(zl, w_conv, b_conv, wax, bax, lam, g_mix)


def _outproj_kernel(hm_ref, hl_ref, x_ref, mod_ref, wm_ref, wl_ref, o_ref):
    y = (jnp.dot(hm_ref[...], wm_ref[...], preferred_element_type=F32)
         + jnp.dot(hl_ref[...], wl_ref[...], preferred_element_type=F32))
    o_ref[...] = x_ref[...] + mod_ref[2:3, :] * y


def _outproj(hm, hl, x, mod, wm, wl):
    tm = TM_PROJ
    per_b = SEQ // tm
    return pl.pallas_call(
        _outproj_kernel,
        grid=(TOKENS // tm,),
        in_specs=[
            pl.BlockSpec((tm, D_MLSTM), lambda i: (i, 0)),
            pl.BlockSpec((tm, D_LRU), lambda i: (i, 0)),
            pl.BlockSpec((tm, D_MODEL), lambda i: (i, 0)),
            pl.BlockSpec((None, 6, D_MODEL), lambda i: (i // per_b, 0, 0)),
            pl.BlockSpec((D_MLSTM, D_MODEL), lambda i: (0, 0)),
            pl.BlockSpec((D_LRU, D_MODEL), lambda i: (0, 0)),
        ],
        out_specs=pl.BlockSpec((tm, D_MODEL), lambda i: (i, 0)),
        out_shape=jax.ShapeDtypeStruct((TOKENS, D_MODEL), F32),
        compiler_params=pltpu.CompilerParams(
            dimension_semantics=("arbitrary",), vmem_limit_bytes=VMEM_LIMIT),
        name="outproj",
    )(hm, hl, x, mod, wm, wl)


def _ffn_kernel(x_ref, mod_ref, g_ref, wg_ref, wu_ref, wd_ref, o_ref, h2_ref, acc_ref):
    j = pl.program_id(1)

    @pl.when(j == 0)
    def _():
        h = _mod_norm(x_ref[...], g_ref[...], mod_ref[4:5, :], mod_ref[3:4, :])
        h2_ref[...] = h.astype(BF16)
        acc_ref[...] = jnp.zeros_like(acc_ref)

    hb = h2_ref[...]
    g = jnp.dot(hb, wg_ref[...], preferred_element_type=F32)
    u = jnp.dot(hb, wu_ref[...], preferred_element_type=F32)
    act = g * _sigmoid(g) * u
    acc_ref[...] += jnp.dot(act.astype(BF16), wd_ref[...], preferred_element_type=F32)

    @pl.when(j == pl.num_programs(1) - 1)
    def _():
        o_ref[...] = x_ref[...] + mod_ref[5:6, :] * acc_ref[...]


def _ffn(x, mod, g, wg, wu, wd):
    tm, tf = TM_FFN, TF_FFN
    per_b = SEQ // tm
    return pl.pallas_call(
        _ffn_kernel,
        grid=(TOKENS // tm, D_FF // tf),
        in_specs=[
            pl.BlockSpec((tm, D_MODEL), lambda i, j: (i, 0)),
            pl.BlockSpec((None, 6, D_MODEL), lambda i, j: (i // per_b, 0, 0)),
            pl.BlockSpec((1, D_MODEL), lambda i, j: (0, 0)),
            pl.BlockSpec((D_MODEL, tf), lambda i, j: (0, j)),
            pl.BlockSpec((D_MODEL, tf), lambda i, j: (0, j)),
            pl.BlockSpec((tf, D_MODEL), lambda i, j: (j, 0)),
        ],
        out_specs=pl.BlockSpec((tm, D_MODEL), lambda i, j: (i, 0)),
        out_shape=jax.ShapeDtypeStruct((TOKENS, D_MODEL), F32),
        scratch_shapes=[pltpu.VMEM((tm, D_MODEL), BF16), pltpu.VMEM((tm, D_MODEL), F32)],
        compiler_params=pltpu.CompilerParams(
            dimension_semantics=("arbitrary", "arbitrary"), vmem_limit_bytes=VMEM_LIMIT),
        name="ffn",
    )(x, mod, g, wg, wu, wd)


def _router_kernel(x_ref, mod_ref, g_ref, wrh_ref, wrl_ref, br_ref, stri_ref,
                   h_ref, meta_ref, cnt_ref, carry_ref):
    @pl.when(pl.program_id(0) == 0)
    def _():
        carry_ref[...] = jnp.zeros_like(carry_ref)

    tm = x_ref.shape[0]
    lane = lax.broadcasted_iota(jnp.int32, (tm, LANES), 1)
    h = _mod_norm(x_ref[...], g_ref[...], mod_ref[4:5, :], mod_ref[3:4, :])
    h_ref[...] = h
    hb = h.astype(BF16)
    hlo = (h - hb.astype(F32)).astype(BF16)
    wrh = wrh_ref[...]
    logits = (jnp.dot(hb, wrh, preferred_element_type=F32)
              + jnp.dot(hlo, wrh, preferred_element_type=F32)
              + jnp.dot(hb, wrl_ref[...], preferred_element_type=F32)) + br_ref[...]
    logits = jnp.where(lane < N_EXPERTS, logits, -jnp.inf)
    m1 = jnp.max(logits, axis=-1, keepdims=True)
    i1 = jnp.min(jnp.where(logits == m1, lane, LANES), axis=-1, keepdims=True)
    rest = jnp.where(lane == i1, -jnp.inf, logits)
    m2 = jnp.max(rest, axis=-1, keepdims=True)
    i2 = jnp.min(jnp.where(rest == m2, lane, LANES), axis=-1, keepdims=True)
    e2 = jnp.exp(m2 - m1)
    p1 = 1.0 / (1.0 + e2)
    p2 = e2 * p1
    ind = jnp.where((lane == i1) | (lane == i2), 1.0, 0.0)
    carry = carry_ref[0:1, :]
    rank = jnp.dot(stri_ref[...], ind.astype(BF16), preferred_element_type=F32) + carry
    r1 = jnp.sum(jnp.where(lane == i1, rank, 0.0), axis=-1, keepdims=True)
    r2 = jnp.sum(jnp.where(lane == i2, rank, 0.0), axis=-1, keepdims=True)
    vals = (i1.astype(F32), i2.astype(F32), r1, r2, p1, p2)
    meta = jnp.zeros((tm, LANES), F32)
    for n, v in enumerate(vals):
        meta = jnp.where(lane == n, v, meta)
    meta_ref[...] = meta
    carry = carry + jnp.sum(ind, axis=0, keepdims=True)
    carry_ref[0:1, :] = carry
    cnt_ref[...] = jnp.broadcast_to(carry, cnt_ref.shape)


def _router(x, mod, g, wrh, wrl, br, stri):
    tm = TM_ROUTE
    per_b = SEQ // tm
    row = lambda shape: pl.BlockSpec(shape, lambda i: (0, 0))
    return pl.pallas_call(
        _router_kernel,
        grid=(TOKENS // tm,),
        in_specs=[
            pl.BlockSpec((tm, D_MODEL), lambda i: (i, 0)),
            pl.BlockSpec((None, 6, D_MODEL), lambda i: (i // per_b, 0, 0)),
            row((1, D_MODEL)), row((D_MODEL, LANES)), row((D_MODEL, LANES)), row((1, LANES)),
            row((tm, tm)),
        ],
        out_specs=[
            pl.BlockSpec((tm, D_MODEL), lambda i: (i, 0)),
            pl.BlockSpec((tm, LANES), lambda i: (i, 0)),
            row((SUBLANES, LANES)),
        ],
        out_shape=[
            jax.ShapeDtypeStruct((TOKENS, D_MODEL), F32),
            jax.ShapeDtypeStruct((TOKENS, LANES), F32),
            jax.ShapeDtypeStruct((SUBLANES, LANES), F32),
        ],
        scratch_shapes=[pltpu.VMEM((SUBLANES, LANES), F32)],
        compiler_params=pltpu.CompilerParams(
            dimension_semantics=("arbitrary",), vmem_limit_bytes=VMEM_LIMIT),
        name="router",
    )(x, mod, g, wrh, wrl, br, stri)


def _row_copy(src_ref, src_row, dst_ref, dst_row, sem):
    return pltpu.make_async_copy(src_ref.at[pl.ds(src_row, 1)], dst_ref.at[pl.ds(dst_row, 1)], sem)


def _dispatch_kernel(pos_ref, h_ref, xs_in_ref, xs_ref, sem):
    del xs_in_ref
    tm = pos_ref.shape[1]
    t0 = pl.program_id(0) * tm

    def start(r, carry):
        for k in range(2):
            _row_copy(h_ref, t0 + r, xs_ref, pos_ref[k, r], sem).start()
        return carry

    def wait(r, carry):
        for k in range(2):
            _row_copy(h_ref, t0 + r, xs_ref, pos_ref[k, r], sem).wait()
        return carry

    lax.fori_loop(0, tm, start, 0)
    lax.fori_loop(0, tm, wait, 0)


def _dispatch(pos, h):
    tm = TM_ROWS
    xs0 = jnp.zeros((N_SLOTS, D_MODEL), F32)
    return pl.pallas_call(
        _dispatch_kernel,
        grid=(TOKENS // tm,),
        in_specs=[
            pl.BlockSpec((2, tm), lambda i: (0, i), memory_space=pltpu.SMEM),
            pl.BlockSpec(memory_space=pl.ANY),
            pl.BlockSpec(memory_space=pl.ANY),
        ],
        out_specs=pl.BlockSpec(memory_space=pl.ANY),
        out_shape=jax.ShapeDtypeStruct((N_SLOTS, D_MODEL), F32),
        scratch_shapes=[pltpu.SemaphoreType.DMA],
        input_output_aliases={2: 0},
        compiler_params=pltpu.CompilerParams(
            dimension_semantics=("arbitrary",), has_side_effects=True),
        name="dispatch",
    )(pos, h, xs0)


def _experts_kernel(te_ref, tv_ref, xs_ref, wg_ref, wu_ref, wd_ref, o_ref, h2_ref, acc_ref):
    del te_ref
    j = pl.program_id(1)
    last = pl.num_programs(1) - 1
    valid = tv_ref[pl.program_id(0)] != 0

    @pl.when(valid)
    def _():
        @pl.when(j == 0)
        def _():
            h2_ref[...] = xs_ref[...].astype(BF16)
            acc_ref[...] = jnp.zeros_like(acc_ref)

        hb = h2_ref[...]
        g = jnp.dot(hb, wg_ref[...], preferred_element_type=F32)
        u = jnp.dot(hb, wu_ref[...], preferred_element_type=F32)
        act = g * _sigmoid(g) * u
        acc_ref[...] += jnp.dot(act.astype(BF16), wd_ref[...], preferred_element_type=F32)

        @pl.when(j == last)
        def _():
            o_ref[...] = acc_ref[...]

    @pl.when(jnp.logical_not(valid) & (j == last))
    def _():
        o_ref[...] = jnp.zeros_like(o_ref)


def _experts(tile_e, tile_v, xs, wg, wu, wd):
    tm, tf = TM_SLOT, TF_FFN
    return pl.pallas_call(
        _experts_kernel,
        grid_spec=pltpu.PrefetchScalarGridSpec(
            num_scalar_prefetch=2,
            grid=(N_SLOTS // tm, D_FF // tf),
            in_specs=[
                pl.BlockSpec((tm, D_MODEL), lambda i, j, te, tv: (i, 0)),
                pl.BlockSpec((None, D_MODEL, tf), lambda i, j, te, tv: (te[i], 0, j)),
                pl.BlockSpec((None, D_MODEL, tf), lambda i, j, te, tv: (te[i], 0, j)),
                pl.BlockSpec((None, tf, D_MODEL), lambda i, j, te, tv: (te[i], j, 0)),
            ],
            out_specs=pl.BlockSpec((tm, D_MODEL), lambda i, j, te, tv: (i, 0)),
            scratch_shapes=[pltpu.VMEM((tm, D_MODEL), BF16), pltpu.VMEM((tm, D_MODEL), F32)],
        ),
        out_shape=jax.ShapeDtypeStruct((N_SLOTS, D_MODEL), F32),
        compiler_params=pltpu.CompilerParams(
            dimension_semantics=("arbitrary", "arbitrary"), vmem_limit_bytes=VMEM_LIMIT),
        name="experts",
    )(tile_e, tile_v, xs, wg, wu, wd)


def _combine_kernel(pos_ref, x_ref, mod_ref, meta_ref, gf_ref, ys_ref, o_ref, g1_ref, g2_ref, sem):
    tm = x_ref.shape[0]
    bufs = (g1_ref, g2_ref)

    def start(r, carry):
        for k in range(2):
            _row_copy(ys_ref, pos_ref[k, r], bufs[k], r, sem).start()
        return carry

    def wait(r, carry):
        for k in range(2):
            _row_copy(ys_ref, pos_ref[k, r], bufs[k], r, sem).wait()
        return carry

    lax.fori_loop(0, tm, start, 0)
    lax.fori_loop(0, tm, wait, 0)
    meta = meta_ref[...]
    y = meta[:, 4:5] * g1_ref[...] + meta[:, 5:6] * g2_ref[...]
    y = x_ref[...] + mod_ref[5:6, :] * y
    o_ref[...] = y * lax.rsqrt(jnp.mean(y * y, axis=-1, keepdims=True) + EPS) * gf_ref[...]


def _combine(pos, x, mod, meta, g_final, ys):
    tm = TM_ROWS
    per_b = SEQ // tm
    return pl.pallas_call(
        _combine_kernel,
        grid=(TOKENS // tm,),
        in_specs=[
            pl.BlockSpec((2, tm), lambda i: (0, i), memory_space=pltpu.SMEM),
            pl.BlockSpec((tm, D_MODEL), lambda i: (i, 0)),
            pl.BlockSpec((None, 6, D_MODEL), lambda i: (i // per_b, 0, 0)),
            pl.BlockSpec((tm, LANES), lambda i: (i, 0)),
            pl.BlockSpec((1, D_MODEL), lambda i: (0, 0)),
            pl.BlockSpec(memory_space=pl.ANY),
        ],
        out_specs=pl.BlockSpec((tm, D_MODEL), lambda i: (i, 0)),
        out_shape=jax.ShapeDtypeStruct((TOKENS, D_MODEL), F32),
        scratch_shapes=[pltpu.VMEM((tm, D_MODEL), F32), pltpu.VMEM((tm, D_MODEL), F32),
                        pltpu.SemaphoreType.DMA],
        compiler_params=pltpu.CompilerParams(
            dimension_semantics=("arbitrary",), vmem_limit_bytes=VMEM_LIMIT),
        name="combine",
    )(pos, x, mod, meta, g_final, ys)


def _moe(x, mod, g, w_router, b_router, wg, wu, wd, g_final):
    wr = _pad_lanes(w_router)
    wrh = wr.astype(BF16)
    wrl = (wr - wrh.astype(F32)).astype(BF16)
    stri = (lax.broadcasted_iota(jnp.int32, (TM_ROUTE, TM_ROUTE), 1)
            < lax.broadcasted_iota(jnp.int32, (TM_ROUTE, TM_ROUTE), 0)).astype(BF16)
    h, meta, cnt = _router(x, mod, g, wrh, wrl, _pad_lanes(b_router[None]), stri)
    counts = cnt[0, :N_EXPERTS].astype(jnp.int32)
    padded = (counts + TM_SLOT - 1) // TM_SLOT * TM_SLOT
    ends = jnp.cumsum(padded)
    offs = ends - padded
    experts = meta[:, 0:2].astype(jnp.int32)
    ranks = meta[:, 2:4].astype(jnp.int32)
    onehot = experts[:, :, None] == jnp.arange(N_EXPERTS, dtype=jnp.int32)
    pos = (jnp.sum(jnp.where(onehot, offs, 0), axis=-1) + ranks).T
    tile_start = jnp.arange(N_SLOTS // TM_SLOT, dtype=jnp.int32) * TM_SLOT
    tile_e = jnp.minimum(jnp.sum(tile_start[:, None] >= ends[None, :], axis=1), N_EXPERTS - 1)
    tile_v = (tile_start < ends[-1]).astype(jnp.int32)
    xs = _dispatch(pos, h)
    ys = _experts(tile_e.astype(jnp.int32), tile_v, xs, wg, wu, wd)
    return _combine(pos, x, mod, meta, g_final, ys)


def _block_diag(w):
    eye = jnp.eye(N_LRU_BLOCKS, dtype=w.dtype)
    return jnp.einsum('nde,nm->ndme', w, eye).reshape(D_LRU, D_LRU)


def _pad_lanes(a):
    return jnp.zeros(a.shape[:-1] + (LANES,), a.dtype).at[..., :a.shape[-1]].set(a)


def kernel(x, c, w_ada, b_ada, g_norm_mix, g_norm_ffn, w_in, w_conv_qk, b_conv_qk, b_gates,
           w_conv_lru, b_conv_lru, w_lru_a, b_lru_a, w_lru_x, b_lru_x, lru_lambda, g_mix_out, w_out,
           w_ff_gate, w_ff_up, w_ff_down, w_router, b_router, w_exp_gate, w_exp_up, w_exp_down, g_final):
    assert DEPTH == 2
    mods = _ada(c, w_ada, b_ada)
    tri = (lax.broadcasted_iota(jnp.int32, (L_MIX, L_MIX), 1)
           <= lax.broadcasted_iota(jnp.int32, (L_MIX, L_MIX), 0)).astype(BF16)
    xt = x.reshape(TOKENS, D_MODEL)
    for l in range(DEPTH):
        mod = mods[l]
        wi = w_in[l]
        w1 = wi[:, :4 * D_MLSTM].astype(BF16)
        w3 = _pad_lanes(wi[:, 4 * D_MLSTM:4 * D_MLSTM + 2 * N_HEADS]).astype(BF16)
        w2 = wi[:, 4 * D_MLSTM + 2 * N_HEADS:].astype(BF16)
        zq, zl, zg = _inproj(xt, mod, g_norm_mix[l][None], w1, w2, w3)
        hm = _mlstm(zq, zg, w_conv_qk[l], b_conv_qk[l][None], _pad_lanes(b_gates[l][None]),
                    g_mix_out[l][None, :D_MLSTM], tri)
        wax = jnp.concatenate([_block_diag(w_lru_a[l]), _block_diag(w_lru_x[l])], axis=1).astype(BF16)
        bax = jnp.concatenate([b_lru_a[l], b_lru_x[l]])[None]
        hl = _lru(zl, w_conv_lru[l], b_conv_lru[l][None], wax, bax, lru_lambda[l][None],
                  g_mix_out[l][None, D_MLSTM:])
        wo = w_out[l].astype(BF16)
        xt = _outproj(hm, hl, xt, mod, wo[:D_MLSTM], wo[D_MLSTM:])
        jj = l // 2
        if l % 2 == 0:
            xt = _ffn(xt, mod, g_norm_ffn[l][None], w_ff_gate[jj].astype(BF16),
                      w_ff_up[jj].astype(BF16), w_ff_down[jj].astype(BF16))
        else:
            xt = _moe(xt, mod, g_norm_ffn[l][None], w_router[jj], b_router[jj],
                      w_exp_gate[jj].astype(BF16), w_exp_up[jj].astype(BF16),
                      w_exp_down[jj].astype(BF16), g_final[None])
    return xt.reshape(BATCH, SEQ, D_MODEL)
```

```python
import functools

import jax
import jax.numpy as jnp
from jax import lax
from jax.experimental import pallas as pl
from jax.experimental.pallas import tpu as pltpu

F32 = jnp.float32
BF16 = jnp.bfloat16

D_MODEL = 1024
BATCH = 4
SEQ = 8192
TOKENS = BATCH * SEQ
DEPTH = 2
D_MLSTM = 512
N_HEADS = 4
DH = 128
D_LRU = 512
N_LRU_BLOCKS = 8
DB_LRU = 64
CONV_W = 4
LRU_C = 8.0
D_FF = 2816
N_EXPERTS = 8
EPS = 1e-6

LANES = 128
SUBLANES = 8
VMEM_LIMIT = 56 * 1024 * 1024

TM_PROJ = 512
L_MIX = 256
TM_FFN = 512
TM_ROUTE = 256
TM_SLOT = 512
N_SLOTS = 2 * TOKENS + (TOKENS // TM_ROUTE) * N_EXPERTS * SUBLANES + N_EXPERTS * TM_SLOT
TF_FFN = 1408


def _sigmoid(x):
    return 1.0 / (1.0 + jnp.exp(-x))


def _mod_norm(x, g, scale, shift):
    ms = jnp.mean(x * x, axis=-1, keepdims=True)
    return (x * lax.rsqrt(ms + EPS)) * g * (1.0 + scale) + shift


def _shift_rows(x, tail, s):
    rx = pltpu.roll(x, s, 0)
    rt = pltpu.roll(tail, s, 0)
    row = lax.broadcasted_iota(jnp.int32, tail.shape, 0)
    head = jnp.where(row < s, rt, rx[:SUBLANES])
    return jnp.concatenate([head, rx[SUBLANES:]], axis=0)


def _causal_conv(x, tail, w, b):
    acc = b + w[CONV_W - 1:CONV_W] * x
    for s in range(1, CONV_W):
        acc = acc + w[CONV_W - 1 - s:CONV_W - s] * _shift_rows(x, tail, s)
    return acc


def _ada_kernel(c_ref, w_ref, b_ref, o_ref):
    c = c_ref[...]
    cs = c * _sigmoid(c)
    o_ref[...] = jnp.dot(cs, w_ref[...], preferred_element_type=F32) + b_ref[...]


def _ada(c, w_ada, b_ada):
    tn = 1536
    c8 = jnp.zeros((SUBLANES, D_MODEL), F32).at[:BATCH].set(c)
    out = pl.pallas_call(
        _ada_kernel,
        grid=(DEPTH, 6 * D_MODEL // tn),
        in_specs=[
            pl.BlockSpec((SUBLANES, D_MODEL), lambda l, n: (0, 0)),
            pl.BlockSpec((None, D_MODEL, tn), lambda l, n: (l, 0, n)),
            pl.BlockSpec((None, 1, tn), lambda l, n: (l, 0, n)),
        ],
        out_specs=pl.BlockSpec((None, SUBLANES, tn), lambda l, n: (l, 0, n)),
        out_shape=jax.ShapeDtypeStruct((DEPTH, SUBLANES, 6 * D_MODEL), F32),
        compiler_params=pltpu.CompilerParams(
            dimension_semantics=("arbitrary", "arbitrary"), vmem_limit_bytes=VMEM_LIMIT),
        name="ada",
    )(c8, w_ada, b_ada.reshape(DEPTH, 1, 6 * D_MODEL))
    return out[:, :BATCH].reshape(DEPTH, BATCH, 6, D_MODEL)


def _inproj_kernel(x_ref, mod_ref, g_ref, w1_ref, w2_ref, w3_ref, zq_ref, zl_ref, zg_ref):
    mod = mod_ref[...]
    h = _mod_norm(x_ref[...], g_ref[...], mod[1:2], mod[0:1]).astype(BF16)
    zq_ref[...] = jnp.dot(h, w1_ref[...], preferred_element_type=F32).astype(BF16)
    zl_ref[...] = jnp.dot(h, w2_ref[...], preferred_element_type=F32).astype(BF16)
    zg_ref[...] = jnp.dot(h, w3_ref[...], preferred_element_type=F32)


def _inproj(x, mod, g, w1, w2, w3):
    tm = TM_PROJ
    per_b = SEQ // tm
    return pl.pallas_call(
        _inproj_kernel,
        grid=(TOKENS // tm,),
        in_specs=[
            pl.BlockSpec((tm, D_MODEL), lambda i: (i, 0)),
            pl.BlockSpec((None, 6, D_MODEL), lambda i: (i // per_b, 0, 0)),
            pl.BlockSpec((1, D_MODEL), lambda i: (0, 0)),
            pl.BlockSpec(w1.shape, lambda i: (0, 0)),
            pl.BlockSpec(w2.shape, lambda i: (0, 0)),
            pl.BlockSpec(w3.shape, lambda i: (0, 0)),
        ],
        out_specs=[
            pl.BlockSpec((tm, 4 * D_MLSTM), lambda i: (i, 0)),
            pl.BlockSpec((tm, 2 * D_LRU), lambda i: (i, 0)),
            pl.BlockSpec((tm, LANES), lambda i: (i, 0)),
        ],
        out_shape=[
            jax.ShapeDtypeStruct((TOKENS, 4 * D_MLSTM), BF16),
            jax.ShapeDtypeStruct((TOKENS, 2 * D_LRU), BF16),
            jax.ShapeDtypeStruct((TOKENS, LANES), F32),
        ],
        compiler_params=pltpu.CompilerParams(
            dimension_semantics=("arbitrary",), vmem_limit_bytes=VMEM_LIMIT),
        name="inproj",
    )(x, mod, g, w1, w2, w3)


def _mlstm_kernel(zq_ref, zg_ref, wc_ref, bc_ref, bg_ref, gm_ref, tri_ref, hm_ref,
                  ct_ref, m_ref, tail_ref):
    L = L_MIX

    @pl.when(pl.program_id(1) == 0)
    def _():
        ct_ref[...] = jnp.zeros_like(ct_ref)
        m_ref[...] = jnp.zeros_like(m_ref)
        tail_ref[...] = jnp.zeros_like(tail_ref)

    qk_raw = zq_ref[:, :2 * D_MLSTM].astype(F32)
    qk = _causal_conv(qk_raw, tail_ref[...], wc_ref[...], bc_ref[...])
    tail_ref[...] = qk_raw[L - SUBLANES:, :]
    qk = qk * _sigmoid(qk)
    q = qk[:, :D_MLSTM].astype(BF16)
    k = (qk[:, D_MLSTM:] * (DH ** -0.5)).astype(BF16)

    g = zg_ref[...] + bg_ref[...]
    lf = jnp.minimum(g, 0.0) - jnp.log1p(jnp.exp(-jnp.abs(g)))
    tri = tri_ref[...]
    hi = lf.astype(BF16)
    r1 = lf - hi.astype(F32)
    mid = r1.astype(BF16)
    lo = (r1 - mid.astype(F32)).astype(BF16)
    bcum = (jnp.dot(tri, hi, preferred_element_type=F32)
            + jnp.dot(tri, mid, preferred_element_type=F32)
            + jnp.dot(tri, lo, preferred_element_type=F32))
    lane = lax.broadcasted_iota(jnp.int32, (L, LANES), 1)
    cols = jnp.where(lane < N_HEADS, g, bcum)
    rows = jnp.transpose(cols)

    t_idx = lax.broadcasted_iota(jnp.int32, (L, L), 0)
    s_idx = lax.broadcasted_iota(jnp.int32, (L, L), 1)
    causal = s_idx <= t_idx
    ones_col = (lax.broadcasted_iota(jnp.int32, (L, DH), 1) == 0).astype(BF16)
    gm = gm_ref[...]

    for h in range(N_HEADS):
        sl = slice(h * DH, (h + 1) * DH)
        ic = cols[:, h:h + 1]
        bc = cols[:, N_HEADS + h:N_HEADS + h + 1]
        ir = rows[h:h + 1, :]
        br = rows[N_HEADS + h:N_HEADS + h + 1, :]
        b_last = br[:, L - 1:L]
        m_prev = m_ref[h:h + 1, 0:1]

        dm = jnp.where(causal, bc - br + ir, -jnp.inf)
        inter_log = bc + m_prev
        m_t = jnp.maximum(inter_log, jnp.max(dm, axis=1, keepdims=True))
        qh = q[:, sl]
        kh = k[:, sl]
        vh = zq_ref[:, 2 * D_MLSTM + h * DH:2 * D_MLSTM + (h + 1) * DH]
        vaug = jnp.concatenate([vh, ones_col], axis=1)
        sc = lax.dot_general(qh, kh, (((1,), (1,)), ((), ())), preferred_element_type=F32)
        p = (sc * jnp.exp(dm - m_t)).astype(BF16)
        ct = ct_ref[h]
        out = (jnp.dot(p, vaug, preferred_element_type=F32)
               + jnp.exp(inter_log - m_t)
               * jnp.dot(qh, ct.astype(BF16), preferred_element_type=F32))
        num = out[:, :DH]
        den = out[:, DH:DH + 1]
        hh = num / jnp.maximum(jnp.abs(den), jnp.exp(-m_t))
        hn = hh * lax.rsqrt(jnp.mean(hh * hh, axis=-1, keepdims=True) + EPS) * gm[:, sl]
        og = zq_ref[:, 3 * D_MLSTM + h * DH:3 * D_MLSTM + (h + 1) * DH].astype(F32)
        hm_ref[:, sl] = (hn * _sigmoid(og)).astype(BF16)

        w_end = b_last - bc + ic
        m_loc = jnp.max(w_end, axis=0, keepdims=True)
        e_end = jnp.exp(w_end - m_loc)
        m_new = jnp.maximum(b_last + m_prev, m_loc)
        ev = (e_end * vaug.astype(F32)).astype(BF16)
        c_loc = lax.dot_general(kh, ev, (((0,), (0,)), ((), ())), preferred_element_type=F32)
        ct_ref[h] = jnp.exp(b_last + m_prev - m_new) * ct + jnp.exp(m_loc - m_new) * c_loc
        m_ref[h:h + 1, :] = jnp.broadcast_to(m_new, (1, LANES))


def _mlstm(zq, zg, w_conv, b_conv, b_gates, g_mix, tri):
    L = L_MIX
    nc = SEQ // L
    return pl.pallas_call(
        _mlstm_kernel,
        grid=(BATCH, nc),
        in_specs=[
            pl.BlockSpec((L, 4 * D_MLSTM), lambda b, j: (b * nc + j, 0)),
            pl.BlockSpec((L, LANES), lambda b, j: (b * nc + j, 0)),
            pl.BlockSpec((CONV_W, 2 * D_MLSTM), lambda b, j: (0, 0)),
            pl.BlockSpec((1, 2 * D_MLSTM), lambda b, j: (0, 0)),
            pl.BlockSpec((1, LANES), lambda b, j: (0, 0)),
            pl.BlockSpec((1, D_MLSTM), lambda b, j: (0, 0)),
            pl.BlockSpec((L, L), lambda b, j: (0, 0)),
        ],
        out_specs=pl.BlockSpec((L, D_MLSTM), lambda b, j: (b * nc + j, 0)),
        out_shape=jax.ShapeDtypeStruct((TOKENS, D_MLSTM), BF16),
        scratch_shapes=[
            pltpu.VMEM((N_HEADS, DH, 2 * DH), F32),
            pltpu.VMEM((SUBLANES, LANES), F32),
            pltpu.VMEM((SUBLANES, 2 * D_MLSTM), F32),
        ],
        compiler_params=pltpu.CompilerParams(
            dimension_semantics=("arbitrary", "arbitrary"), vmem_limit_bytes=VMEM_LIMIT),
        name="mlstm",
    )(zq, zg, w_conv, b_conv, b_gates, g_mix, tri)


def _lru_kernel(zl_ref, wc_ref, bc_ref, wax_ref, bax_ref, lam_ref, gm_ref, hl_ref,
                tail_ref, hc_ref):
    L = L_MIX

    @pl.when(pl.program_id(1) == 0)
    def _():
        tail_ref[...] = jnp.zeros_like(tail_ref)
        hc_ref[...] = jnp.zeros_like(hc_ref)

    xl = zl_ref[:, :D_LRU].astype(F32)
    gl = zl_ref[:, D_LRU:].astype(F32)
    xc = _causal_conv(xl, tail_ref[...], wc_ref[...], bc_ref[...])
    tail_ref[...] = xl[L - SUBLANES:, :]

    gates = jnp.dot(xc.astype(BF16), wax_ref[...], preferred_element_type=F32) + bax_ref[...]
    r = _sigmoid(gates[:, :D_LRU])
    i = _sigmoid(gates[:, D_LRU:])
    lam = lam_ref[...]
    sp = jnp.maximum(-lam, 0.0) + jnp.log1p(jnp.exp(-jnp.abs(lam)))
    log_a = -LRU_C * r * sp
    a = jnp.exp(log_a)
    y2 = 2.0 * log_a
    series = -y2 * (1.0 + y2 * (0.5 + y2 * (1.0 / 6.0 + y2 * (1.0 / 24.0))))
    u = jnp.sqrt(jnp.where(y2 > -0.01, series, 1.0 - a * a)) * (i * xc)

    r8 = lax.broadcasted_iota(jnp.int32, (L, D_LRU), 0) & (SUBLANES - 1)
    for s in (1, 2, 4):
        a_sh = pltpu.roll(a, s, 0)
        u_sh = pltpu.roll(u, s, 0)
        valid = r8 >= s
        u = jnp.where(valid, a * u_sh + u, u)
        a = jnp.where(valid, a * a_sh, a)
    hc = hc_ref[...]
    hs = []
    for gi in range(L // SUBLANES):
        sl = slice(gi * SUBLANES, (gi + 1) * SUBLANES)
        hg = u[sl] + a[sl] * hc
        hs.append(hg)
        hc = hg[SUBLANES - 1:SUBLANES]
    hc_ref[...] = hc
    hseq = jnp.concatenate(hs, axis=0)

    gelu = 0.5 * gl * (1.0 + jnp.tanh(0.7978845608028654 * (gl + 0.044715 * gl * gl * gl)))
    hl = hseq * gelu
    hl = hl * lax.rsqrt(jnp.mean(hl * hl, axis=-1, keepdims=True) + EPS) * gm_ref[...]
    hl_ref[...] = hl.astype(BF16)


def _lru(zl, w_conv, b_conv, wax, bax, lam, g_mix):
    L = L_MIX
    nc = SEQ // L
    return pl.pallas_call(
        _lru_kernel,
        grid=(BATCH, nc),
        in_specs=[
            pl.BlockSpec((L, 2 * D_LRU), lambda b, j: (b * nc + j, 0)),
            pl.BlockSpec((CONV_W, D_LRU), lambda b, j: (0, 0)),
            pl.BlockSpec((1, D_LRU), lambda b, j: (0, 0)),
            pl.BlockSpec((D_LRU, 2 * D_LRU), lambda b, j: (0, 0)),
            pl.BlockSpec((1, 2 * D_LRU), lambda b, j: (0, 0)),
            pl.BlockSpec((1, D_LRU), lambda b, j: (0, 0)),
            pl.BlockSpec((1, D_LRU), lambda b, j: (0, 0)),
        ],
        out_specs=pl.BlockSpec((L, D_LRU), lambda b, j: (b * nc + j, 0)),
        out_shape=jax.ShapeDtypeStruct((TOKENS, D_LRU), BF16),
        scratch_shapes=[
            pltpu.VMEM((SUBLANES, D_LRU), F32),
            pltpu.VMEM((1, D_LRU), F32),
        ],
        compiler_params=pltpu.CompilerParams(
            dimension_semantics=("arbitrary", "arbitrary"), vmem_limit_bytes=VMEM_LIMIT),
        name="rglru",
    )(zl, w_conv, b_conv, wax, bax, lam, g_mix)


def _outproj_kernel(hm_ref, hl_ref, x_ref, mod_ref, wm_ref, wl_ref, o_ref):
    y = (jnp.dot(hm_ref[...], wm_ref[...], preferred_element_type=F32)
         + jnp.dot(hl_ref[...], wl_ref[...], preferred_element_type=F32))
    o_ref[...] = x_ref[...] + mod_ref[2:3, :] * y


def _outproj(hm, hl, x, mod, wm, wl):
    tm = TM_PROJ
    per_b = SEQ // tm
    return pl.pallas_call(
        _outproj_kernel,
        grid=(TOKENS // tm,),
        in_specs=[
            pl.BlockSpec((tm, D_MLSTM), lambda i: (i, 0)),
            pl.BlockSpec((tm, D_LRU), lambda i: (i, 0)),
            pl.BlockSpec((tm, D_MODEL), lambda i: (i, 0)),
            pl.BlockSpec((None, 6, D_MODEL), lambda i: (i // per_b, 0, 0)),
            pl.BlockSpec((D_MLSTM, D_MODEL), lambda i: (0, 0)),
            pl.BlockSpec((D_LRU, D_MODEL), lambda i: (0, 0)),
        ],
        out_specs=pl.BlockSpec((tm, D_MODEL), lambda i: (i, 0)),
        out_shape=jax.ShapeDtypeStruct((TOKENS, D_MODEL), F32),
        compiler_params=pltpu.CompilerParams(
            dimension_semantics=("arbitrary",), vmem_limit_bytes=VMEM_LIMIT),
        name="outproj",
    )(hm, hl, x, mod, wm, wl)


def _ffn_kernel(x_ref, mod_ref, g_ref, wg_ref, wu_ref, wd_ref, o_ref, h2_ref, acc_ref):
    j = pl.program_id(1)

    @pl.when(j == 0)
    def _():
        h = _mod_norm(x_ref[...], g_ref[...], mod_ref[4:5, :], mod_ref[3:4, :])
        h2_ref[...] = h.astype(BF16)
        acc_ref[...] = jnp.zeros_like(acc_ref)

    hb = h2_ref[...]
    g = jnp.dot(hb, wg_ref[...], preferred_element_type=F32)
    u = jnp.dot(hb, wu_ref[...], preferred_element_type=F32)
    act = g * _sigmoid(g) * u
    acc_ref[...] += jnp.dot(act.astype(BF16), wd_ref[...], preferred_element_type=F32)

    @pl.when(j == pl.num_programs(1) - 1)
    def _():
        o_ref[...] = x_ref[...] + mod_ref[5:6, :] * acc_ref[...]


def _ffn(x, mod, g, wg, wu, wd):
    tm, tf = TM_FFN, TF_FFN
    per_b = SEQ // tm
    return pl.pallas_call(
        _ffn_kernel,
        grid=(TOKENS // tm, D_FF // tf),
        in_specs=[
            pl.BlockSpec((tm, D_MODEL), lambda i, j: (i, 0)),
            pl.BlockSpec((None, 6, D_MODEL), lambda i, j: (i // per_b, 0, 0)),
            pl.BlockSpec((1, D_MODEL), lambda i, j: (0, 0)),
            pl.BlockSpec((D_MODEL, tf), lambda i, j: (0, j)),
            pl.BlockSpec((D_MODEL, tf), lambda i, j: (0, j)),
            pl.BlockSpec((tf, D_MODEL), lambda i, j: (j, 0)),
        ],
        out_specs=pl.BlockSpec((tm, D_MODEL), lambda i, j: (i, 0)),
        out_shape=jax.ShapeDtypeStruct((TOKENS, D_MODEL), F32),
        scratch_shapes=[pltpu.VMEM((tm, D_MODEL), BF16), pltpu.VMEM((tm, D_MODEL), F32)],
        compiler_params=pltpu.CompilerParams(
            dimension_semantics=("arbitrary", "arbitrary"), vmem_limit_bytes=VMEM_LIMIT),
        name="ffn",
    )(x, mod, g, wg, wu, wd)


def _router_kernel(x_ref, mod_ref, g_ref, wrh_ref, wrl_ref, br_ref, stri_ref, upper_ref,
                   h_ref, meta_ref, tab_ref, carry_ref):
    @pl.when(pl.program_id(0) == 0)
    def _():
        carry_ref[...] = jnp.zeros_like(carry_ref)

    tm = x_ref.shape[0]
    lane = lax.broadcasted_iota(jnp.int32, (tm, LANES), 1)
    h = _mod_norm(x_ref[...], g_ref[...], mod_ref[4:5, :], mod_ref[3:4, :])
    hb = h.astype(BF16)
    h_ref[...] = hb
    hlo = (h - hb.astype(F32)).astype(BF16)
    wrh = wrh_ref[...]
    logits = (jnp.dot(hb, wrh, preferred_element_type=F32)
              + jnp.dot(hlo, wrh, preferred_element_type=F32)
              + jnp.dot(hb, wrl_ref[...], preferred_element_type=F32)) + br_ref[...]
    logits = jnp.where(lane < N_EXPERTS, logits, -jnp.inf)
    m1 = jnp.max(logits, axis=-1, keepdims=True)
    i1 = jnp.min(jnp.where(logits == m1, lane, LANES), axis=-1, keepdims=True)
    rest = jnp.where(lane == i1, -jnp.inf, logits)
    m2 = jnp.max(rest, axis=-1, keepdims=True)
    i2 = jnp.min(jnp.where(rest == m2, lane, LANES), axis=-1, keepdims=True)
    e2 = jnp.exp(m2 - m1)
    p1 = 1.0 / (1.0 + e2)
    p2 = e2 * p1
    ind = jnp.where((lane == i1) | (lane == i2), 1.0, 0.0)
    n_e = jnp.floor((jnp.sum(ind, axis=0, keepdims=True) + (SUBLANES - 1)) * (1.0 / SUBLANES)) * SUBLANES
    lrank = jnp.dot(stri_ref[...], ind.astype(BF16), preferred_element_type=F32)
    loff = jnp.dot(jnp.broadcast_to(n_e, (SUBLANES, LANES)).astype(BF16), upper_ref[...],
                   preferred_element_type=F32)[0:1, :]
    row = lrank + loff
    d1 = jnp.sum(jnp.where(lane == i1, row, 0.0), axis=-1, keepdims=True)
    d2 = jnp.sum(jnp.where(lane == i2, row, 0.0), axis=-1, keepdims=True)
    vals = (i1.astype(F32), i2.astype(F32), d1, d2, p1, p2)
    meta = jnp.zeros((tm, LANES), F32)
    for n, v in enumerate(vals):
        meta = jnp.where(lane == n, v, meta)
    meta_ref[...] = meta
    carry = carry_ref[0:1, :]
    srow = lax.broadcasted_iota(jnp.int32, (SUBLANES, LANES), 0)
    tab_ref[...] = jnp.where(srow == 0, n_e, jnp.where(srow == 1, carry, jnp.where(srow == 2, loff, 0.0)))
    carry_ref[0:1, :] = carry + n_e


def _router(x, mod, g, wrh, wrl, br, stri, upper):
    tm = TM_ROUTE
    per_b = SEQ // tm
    row = lambda shape: pl.BlockSpec(shape, lambda i: (0, 0))
    return pl.pallas_call(
        _router_kernel,
        grid=(TOKENS // tm,),
        in_specs=[
            pl.BlockSpec((tm, D_MODEL), lambda i: (i, 0)),
            pl.BlockSpec((None, 6, D_MODEL), lambda i: (i // per_b, 0, 0)),
            row((1, D_MODEL)), row((D_MODEL, LANES)), row((D_MODEL, LANES)), row((1, LANES)),
            row((tm, tm)), row((LANES, LANES)),
        ],
        out_specs=[
            pl.BlockSpec((tm, D_MODEL), lambda i: (i, 0)),
            pl.BlockSpec((tm, LANES), lambda i: (i, 0)),
            pl.BlockSpec((None, SUBLANES, LANES), lambda i: (i, 0, 0)),
        ],
        out_shape=[
            jax.ShapeDtypeStruct((TOKENS, D_MODEL), BF16),
            jax.ShapeDtypeStruct((TOKENS, LANES), F32),
            jax.ShapeDtypeStruct((TOKENS // tm, SUBLANES, LANES), F32),
        ],
        scratch_shapes=[pltpu.VMEM((SUBLANES, LANES), F32)],
        compiler_params=pltpu.CompilerParams(
            dimension_semantics=("arbitrary",), vmem_limit_bytes=VMEM_LIMIT),
        name="router",
    )(x, mod, g, wrh, wrl, br, stri, upper)


BLOCK_ROWS = 2 * TM_ROUTE + LANES
RANGE_PIECES = tuple(1 << b for b in range(TM_ROUTE.bit_length() - 1, 2, -1))


def _range_copies(n_ref, loc_ref, slot_ref, block_ref, slots_ref, sem, to_slots, visit):
    base = pl.program_id(0) * N_EXPERTS
    for e in range(N_EXPERTS):
        n = n_ref[base + e]
        loc = pl.multiple_of(loc_ref[base + e], SUBLANES)
        slot = pl.multiple_of(slot_ref[base + e], SUBLANES)
        for piece in RANGE_PIECES:
            has = (n & piece) != 0
            src = block_ref.at[pl.ds(loc, piece)]
            dst = slots_ref.at[pl.ds(slot, piece)]
            if not to_slots:
                src, dst = dst, src

            @pl.when(has)
            def _():
                visit(pltpu.make_async_copy(src, dst, sem))

            step = jnp.where(has, piece, 0)
            loc = pl.multiple_of(loc + step, SUBLANES)
            slot = pl.multiple_of(slot + step, SUBLANES)


def _choice_onehots(meta, width):
    col = lax.broadcasted_iota(jnp.int32, (meta.shape[0], width), 1)
    d1 = meta[:, 2:3].astype(jnp.int32)
    d2 = meta[:, 3:4].astype(jnp.int32)
    return col == d1, col == d2


def _dispatch_kernel(n_ref, loc_ref, slot_ref, h_ref, meta_ref, xs_in_ref, xs_ref, block_ref, sem):
    del xs_in_ref
    a1, a2 = _choice_onehots(meta_ref[...], BLOCK_ROWS)
    sel = (a1 | a2).astype(BF16)
    block_ref[...] = lax.dot_general(sel, h_ref[...], (((0,), (0,)), ((), ())),
                                     preferred_element_type=F32)
    args = (n_ref, loc_ref, slot_ref, block_ref, xs_ref, sem, True)
    _range_copies(*args, lambda cp: cp.start())
    _range_copies(*args, lambda cp: cp.wait())


def _dispatch(tabs, h, meta):
    tm = TM_ROUTE
    xs0 = jnp.zeros((N_SLOTS, D_MODEL), F32)
    return pl.pallas_call(
        _dispatch_kernel,
        grid_spec=pltpu.PrefetchScalarGridSpec(
            num_scalar_prefetch=3,
            grid=(TOKENS // tm,),
            in_specs=[
                pl.BlockSpec((tm, D_MODEL), lambda i, *_: (i, 0)),
                pl.BlockSpec((tm, LANES), lambda i, *_: (i, 0)),
                pl.BlockSpec(memory_space=pl.ANY),
            ],
            out_specs=pl.BlockSpec(memory_space=pl.ANY),
            scratch_shapes=[pltpu.VMEM((BLOCK_ROWS, D_MODEL), F32), pltpu.SemaphoreType.DMA],
        ),
        out_shape=jax.ShapeDtypeStruct((N_SLOTS, D_MODEL), F32),
        input_output_aliases={5: 0},
        compiler_params=pltpu.CompilerParams(
            dimension_semantics=("arbitrary",), vmem_limit_bytes=VMEM_LIMIT, has_side_effects=True),
        name="dispatch",
    )(*tabs, h, meta, xs0)


def _experts_kernel(te_ref, tv_ref, xs_ref, wg_ref, wu_ref, wd_ref, o_ref, h2_ref, acc_ref):
    del te_ref
    j = pl.program_id(1)
    last = pl.num_programs(1) - 1
    valid = tv_ref[pl.program_id(0)] != 0

    @pl.when(valid)
    def _():
        @pl.when(j == 0)
        def _():
            h2_ref[...] = xs_ref[...].astype(BF16)
            acc_ref[...] = jnp.zeros_like(acc_ref)

        hb = h2_ref[...]
        g = jnp.dot(hb, wg_ref[...], preferred_element_type=F32)
        u = jnp.dot(hb, wu_ref[...], preferred_element_type=F32)
        act = g * _sigmoid(g) * u
        acc_ref[...] += jnp.dot(act.astype(BF16), wd_ref[...], preferred_element_type=F32)

        @pl.when(j == last)
        def _():
            o_ref[...] = acc_ref[...]

    @pl.when(jnp.logical_not(valid) & (j == last))
    def _():
        o_ref[...] = jnp.zeros_like(o_ref)


def _experts(tile_e, tile_v, xs, wg, wu, wd):
    tm, tf = TM_SLOT, TF_FFN
    return pl.pallas_call(
        _experts_kernel,
        grid_spec=pltpu.PrefetchScalarGridSpec(
            num_scalar_prefetch=2,
            grid=(N_SLOTS // tm, D_FF // tf),
            in_specs=[
                pl.BlockSpec((tm, D_MODEL), lambda i, j, te, tv: (i, 0)),
                pl.BlockSpec((None, D_MODEL, tf), lambda i, j, te, tv: (te[i], 0, j)),
                pl.BlockSpec((None, D_MODEL, tf), lambda i, j, te, tv: (te[i], 0, j)),
                pl.BlockSpec((None, tf, D_MODEL), lambda i, j, te, tv: (te[i], j, 0)),
            ],
            out_specs=pl.BlockSpec((tm, D_MODEL), lambda i, j, te, tv: (i, 0)),
            scratch_shapes=[pltpu.VMEM((tm, D_MODEL), BF16), pltpu.VMEM((tm, D_MODEL), F32)],
        ),
        out_shape=jax.ShapeDtypeStruct((N_SLOTS, D_MODEL), F32),
        compiler_params=pltpu.CompilerParams(
            dimension_semantics=("arbitrary", "arbitrary"), vmem_limit_bytes=VMEM_LIMIT),
        name="experts",
    )(tile_e, tile_v, xs, wg, wu, wd)


def _combine_kernel(n_ref, loc_ref, slot_ref, x_ref, mod_ref, meta_ref, gf_ref, ys_ref, o_ref,
                    block_ref, sem):
    @pl.when(pl.program_id(0) == 0)
    def _():
        block_ref[...] = jnp.zeros_like(block_ref)

    args = (n_ref, loc_ref, slot_ref, block_ref, ys_ref, sem, False)
    _range_copies(*args, lambda cp: cp.start())
    _range_copies(*args, lambda cp: cp.wait())
    meta = meta_ref[...]
    a1, a2 = _choice_onehots(meta, BLOCK_ROWS)
    blk = block_ref[...].astype(BF16)
    y = (meta[:, 4:5] * jnp.dot(a1.astype(BF16), blk, preferred_element_type=F32)
         + meta[:, 5:6] * jnp.dot(a2.astype(BF16), blk, preferred_element_type=F32))
    y = x_ref[...] + mod_ref[5:6, :] * y
    o_ref[...] = y * lax.rsqrt(jnp.mean(y * y, axis=-1, keepdims=True) + EPS) * gf_ref[...]


def _combine(tabs, x, mod, meta, g_final, ys):
    tm = TM_ROUTE
    per_b = SEQ // tm
    return pl.pallas_call(
        _combine_kernel,
        grid_spec=pltpu.PrefetchScalarGridSpec(
            num_scalar_prefetch=3,
            grid=(TOKENS // tm,),
            in_specs=[
                pl.BlockSpec((tm, D_MODEL), lambda i, *_: (i, 0)),
                pl.BlockSpec((None, 6, D_MODEL), lambda i, *_: (i // per_b, 0, 0)),
                pl.BlockSpec((tm, LANES), lambda i, *_: (i, 0)),
                pl.BlockSpec((1, D_MODEL), lambda i, *_: (0, 0)),
                pl.BlockSpec(memory_space=pl.ANY),
            ],
            out_specs=pl.BlockSpec((tm, D_MODEL), lambda i, *_: (i, 0)),
            scratch_shapes=[pltpu.VMEM((BLOCK_ROWS, D_MODEL), F32), pltpu.SemaphoreType.DMA],
        ),
        out_shape=jax.ShapeDtypeStruct((TOKENS, D_MODEL), F32),
        compiler_params=pltpu.CompilerParams(
            dimension_semantics=("arbitrary",), vmem_limit_bytes=VMEM_LIMIT),
        name="combine",
    )(*tabs, x, mod, meta, g_final, ys)


def _moe(x, mod, g, w_router, b_router, wg, wu, wd, g_final):
    wr = _pad_lanes(w_router)
    wrh = wr.astype(BF16)
    wrl = (wr - wrh.astype(F32)).astype(BF16)
    stri = (lax.broadcasted_iota(jnp.int32, (TM_ROUTE, TM_ROUTE), 1)
            < lax.broadcasted_iota(jnp.int32, (TM_ROUTE, TM_ROUTE), 0)).astype(BF16)
    upper = (lax.broadcasted_iota(jnp.int32, (LANES, LANES), 0)
             < lax.broadcasted_iota(jnp.int32, (LANES, LANES), 1)).astype(BF16)
    h, meta, tab = _router(x, mod, g, wrh, wrl, _pad_lanes(b_router[None]), stri, upper)
    tab = tab[:, :3, :N_EXPERTS].astype(jnp.int32)
    n_ce, before_ce, loc_ce = tab[:, 0], tab[:, 1], tab[:, 2]
    counts = before_ce[-1] + n_ce[-1]
    padded = (counts + TM_SLOT - 1) // TM_SLOT * TM_SLOT
    ends = jnp.cumsum(padded)
    slot_ce = (ends - padded)[None, :] + before_ce
    tabs = (n_ce.reshape(-1), loc_ce.reshape(-1), slot_ce.reshape(-1))
    tile_start = jnp.arange(N_SLOTS // TM_SLOT, dtype=jnp.int32) * TM_SLOT
    tile_e = jnp.minimum(jnp.sum(tile_start[:, None] >= ends[None, :], axis=1), N_EXPERTS - 1)
    tile_v = (tile_start < ends[-1]).astype(jnp.int32)
    xs = _dispatch(tabs, h, meta)
    ys = _experts(tile_e.astype(jnp.int32), tile_v, xs, wg, wu, wd)
    return _combine(tabs, x, mod, meta, g_final, ys)


def _block_diag(w):
    eye = jnp.eye(N_LRU_BLOCKS, dtype=w.dtype)
    return jnp.einsum('nde,nm->ndme', w, eye).reshape(D_LRU, D_LRU)


def _pad_lanes(a):
    return jnp.zeros(a.shape[:-1] + (LANES,), a.dtype).at[..., :a.shape[-1]].set(a)


def kernel(x, c, w_ada, b_ada, g_norm_mix, g_norm_ffn, w_in, w_conv_qk, b_conv_qk, b_gates,
           w_conv_lru, b_conv_lru, w_lru_a, b_lru_a, w_lru_x, b_lru_x, lru_lambda, g_mix_out, w_out,
           w_ff_gate, w_ff_up, w_ff_down, w_router, b_router, w_exp_gate, w_exp_up, w_exp_down, g_final):
    assert DEPTH == 2
    mods = _ada(c, w_ada, b_ada)
    tri = (lax.broadcasted_iota(jnp.int32, (L_MIX, L_MIX), 1)
           <= lax.broadcasted_iota(jnp.int32, (L_MIX, L_MIX), 0)).astype(BF16)
    xt = x.reshape(TOKENS, D_MODEL)
    for l in range(DEPTH):
        mod = mods[l]
        wi = w_in[l]
        w1 = wi[:, :4 * D_MLSTM].astype(BF16)
        w3 = _pad_lanes(wi[:, 4 * D_MLSTM:4 * D_MLSTM + 2 * N_HEADS]).astype(BF16)
        w2 = wi[:, 4 * D_MLSTM + 2 * N_HEADS:].astype(BF16)
        zq, zl, zg = _inproj(xt, mod, g_norm_mix[l][None], w1, w2, w3)
        hm = _mlstm(zq, zg, w_conv_qk[l], b_conv_qk[l][None], _pad_lanes(b_gates[l][None]),
                    g_mix_out[l][None, :D_MLSTM], tri)
        wax = jnp.concatenate([_block_diag(w_lru_a[l]), _block_diag(w_lru_x[l])], axis=1).astype(BF16)
        bax = jnp.concatenate([b_lru_a[l], b_lru_x[l]])[None]
        hl = _lru(zl, w_conv_lru[l], b_conv_lru[l][None], wax, bax, lru_lambda[l][None],
                  g_mix_out[l][None, D_MLSTM:])
        wo = w_out[l].astype(BF16)
        xt = _outproj(hm, hl, xt, mod, wo[:D_MLSTM], wo[D_MLSTM:])
        jj = l // 2
        if l % 2 == 0:
            xt = _ffn(xt, mod, g_norm_ffn[l][None], w_ff_gate[jj].astype(BF16),
                      w_ff_up[jj].astype(BF16), w_ff_down[jj].astype(BF16))
        else:
            xt = _moe(xt, mod, g_norm_ffn[l][None], w_router[jj], b_router[jj],
                      w_exp_gate[jj].astype(BF16), w_exp_up[jj].astype(BF16),
                      w_exp_down[jj].astype(BF16), g_final[None])
    return xt.reshape(BATCH, SEQ, D_MODEL)
```

```python
import functools

import jax
import jax.numpy as jnp
from jax import lax
from jax.experimental import pallas as pl
from jax.experimental.pallas import tpu as pltpu

F32 = jnp.float32
BF16 = jnp.bfloat16

D_MODEL = 1024
BATCH = 4
SEQ = 8192
TOKENS = BATCH * SEQ
DEPTH = 2
D_MLSTM = 512
N_HEADS = 4
DH = 128
D_LRU = 512
N_LRU_BLOCKS = 8
DB_LRU = 64
CONV_W = 4
LRU_C = 8.0
D_FF = 2816
N_EXPERTS = 8
EPS = 1e-6

LANES = 128
SUBLANES = 8
VMEM_LIMIT = 56 * 1024 * 1024

TM_PROJ = 512
L_MIX = 256
TM_FFN = 512
TM_ROUTE = 256
TM_SLOT = 512
N_SLOTS = 2 * TOKENS + (TOKENS // TM_ROUTE) * N_EXPERTS * SUBLANES + N_EXPERTS * TM_SLOT
TF_FFN = 1408


def _sigmoid(x):
    return 1.0 / (1.0 + jnp.exp(-x))


def _mod_norm(x, g, scale, shift):
    ms = jnp.mean(x * x, axis=-1, keepdims=True)
    return (x * lax.rsqrt(ms + EPS)) * g * (1.0 + scale) + shift


def _causal_conv(x, ext_ref, w, b):
    L = x.shape[0]
    ext_ref[SUBLANES:, :] = x
    acc = b + w[CONV_W - 1:CONV_W] * x
    for s in range(1, CONV_W):
        acc = acc + w[CONV_W - 1 - s:CONV_W - s] * ext_ref[SUBLANES - s:SUBLANES - s + L, :]
    ext_ref[:SUBLANES, :] = x[L - SUBLANES:, :]
    return acc


def _ada_kernel(c_ref, w_ref, b_ref, o_ref):
    c = c_ref[...]
    cs = c * _sigmoid(c)
    o_ref[...] = jnp.dot(cs, w_ref[...], preferred_element_type=F32) + b_ref[...]


def _ada(c, w_ada, b_ada):
    tn = 1536
    c8 = jnp.zeros((SUBLANES, D_MODEL), F32).at[:BATCH].set(c)
    out = pl.pallas_call(
        _ada_kernel,
        grid=(DEPTH, 6 * D_MODEL // tn),
        in_specs=[
            pl.BlockSpec((SUBLANES, D_MODEL), lambda l, n: (0, 0)),
            pl.BlockSpec((None, D_MODEL, tn), lambda l, n: (l, 0, n)),
            pl.BlockSpec((None, 1, tn), lambda l, n: (l, 0, n)),
        ],
        out_specs=pl.BlockSpec((None, SUBLANES, tn), lambda l, n: (l, 0, n)),
        out_shape=jax.ShapeDtypeStruct((DEPTH, SUBLANES, 6 * D_MODEL), F32),
        compiler_params=pltpu.CompilerParams(
            dimension_semantics=("arbitrary", "arbitrary"), vmem_limit_bytes=VMEM_LIMIT),
        name="ada",
    )(c8, w_ada, b_ada.reshape(DEPTH, 1, 6 * D_MODEL))
    return out[:, :BATCH].reshape(DEPTH, BATCH, 6, D_MODEL)


def _inproj_kernel(x_ref, mod_ref, g_ref, w1_ref, w2_ref, w3_ref, zq_ref, zl_ref, zg_ref):
    mod = mod_ref[...]
    h = _mod_norm(x_ref[...], g_ref[...], mod[1:2], mod[0:1]).astype(BF16)
    zq_ref[...] = jnp.dot(h, w1_ref[...], preferred_element_type=F32).astype(BF16)
    zl_ref[...] = jnp.dot(h, w2_ref[...], preferred_element_type=F32).astype(BF16)
    zg_ref[...] = jnp.dot(h, w3_ref[...], preferred_element_type=F32)


def _inproj(x, mod, g, w1, w2, w3):
    tm = TM_PROJ
    per_b = SEQ // tm
    return pl.pallas_call(
        _inproj_kernel,
        grid=(TOKENS // tm,),
        in_specs=[
            pl.BlockSpec((tm, D_MODEL), lambda i: (i, 0)),
            pl.BlockSpec((None, 6, D_MODEL), lambda i: (i // per_b, 0, 0)),
            pl.BlockSpec((1, D_MODEL), lambda i: (0, 0)),
            pl.BlockSpec(w1.shape, lambda i: (0, 0)),
            pl.BlockSpec(w2.shape, lambda i: (0, 0)),
            pl.BlockSpec(w3.shape, lambda i: (0, 0)),
        ],
        out_specs=[
            pl.BlockSpec((tm, 4 * D_MLSTM), lambda i: (i, 0)),
            pl.BlockSpec((tm, 2 * D_LRU), lambda i: (i, 0)),
            pl.BlockSpec((tm, LANES), lambda i: (i, 0)),
        ],
        out_shape=[
            jax.ShapeDtypeStruct((TOKENS, 4 * D_MLSTM), BF16),
            jax.ShapeDtypeStruct((TOKENS, 2 * D_LRU), BF16),
            jax.ShapeDtypeStruct((TOKENS, LANES), F32),
        ],
        compiler_params=pltpu.CompilerParams(
            dimension_semantics=("arbitrary",), vmem_limit_bytes=VMEM_LIMIT),
        name="inproj",
    )(x, mod, g, w1, w2, w3)


def _mlstm_kernel(zq_ref, zg_ref, wc_ref, bc_ref, bg_ref, gm_ref, tri_ref, hm_ref,
                  ct_ref, m_ref, ext_ref):
    L = L_MIX

    @pl.when(pl.program_id(1) == 0)
    def _():
        ct_ref[...] = jnp.zeros_like(ct_ref)
        m_ref[...] = jnp.zeros_like(m_ref)
        ext_ref[:SUBLANES, :] = jnp.zeros((SUBLANES, ext_ref.shape[1]), F32)

    qk = _causal_conv(zq_ref[:, :2 * D_MLSTM].astype(F32), ext_ref, wc_ref[...], bc_ref[...])
    qk = qk * _sigmoid(qk)
    q = qk[:, :D_MLSTM].astype(BF16)
    k = (qk[:, D_MLSTM:] * (DH ** -0.5)).astype(BF16)

    g = zg_ref[...] + bg_ref[...]
    lf = jnp.minimum(g, 0.0) - jnp.log1p(jnp.exp(-jnp.abs(g)))
    tri = tri_ref[...]
    hi = lf.astype(BF16)
    r1 = lf - hi.astype(F32)
    mid = r1.astype(BF16)
    lo = (r1 - mid.astype(F32)).astype(BF16)
    bcum = (jnp.dot(tri, hi, preferred_element_type=F32)
            + jnp.dot(tri, mid, preferred_element_type=F32)
            + jnp.dot(tri, lo, preferred_element_type=F32))
    lane = lax.broadcasted_iota(jnp.int32, (L, LANES), 1)
    cols = jnp.where(lane < N_HEADS, g, bcum)
    rows = jnp.transpose(cols)

    t_idx = lax.broadcasted_iota(jnp.int32, (L, L), 0)
    s_idx = lax.broadcasted_iota(jnp.int32, (L, L), 1)
    causal = s_idx <= t_idx
    ones_col = (lax.broadcasted_iota(jnp.int32, (L, DH), 1) == 0).astype(BF16)
    gm = gm_ref[...]

    heads = range(N_HEADS)
    sls = [slice(h * DH, (h + 1) * DH) for h in heads]
    ic = [cols[:, h:h + 1] for h in heads]
    bc = [cols[:, N_HEADS + h:N_HEADS + h + 1] for h in heads]
    ir = [rows[h:h + 1, :] for h in heads]
    br = [rows[N_HEADS + h:N_HEADS + h + 1, :] for h in heads]
    b_last = [br[h][:, L - 1:L] for h in heads]
    m_prev = [m_ref[h:h + 1, 0:1] for h in heads]
    qh = [q[:, sls[h]] for h in heads]
    kh = [k[:, sls[h]] for h in heads]
    vaug = [jnp.concatenate([zq_ref[:, 2 * D_MLSTM + h * DH:2 * D_MLSTM + (h + 1) * DH], ones_col], axis=1)
            for h in heads]
    ct = [ct_ref[h] for h in heads]

    dm = [jnp.where(causal, bc[h] - br[h] + ir[h], -jnp.inf) for h in heads]
    inter_log = [bc[h] + m_prev[h] for h in heads]
    sc = [lax.dot_general(qh[h], kh[h], (((1,), (1,)), ((), ())), preferred_element_type=F32) for h in heads]
    inter = [jnp.dot(qh[h], ct[h].astype(BF16), preferred_element_type=F32) for h in heads]
    m_t = [jnp.maximum(inter_log[h], jnp.max(dm[h], axis=1, keepdims=True)) for h in heads]

    w_end = [b_last[h] - bc[h] + ic[h] for h in heads]
    m_loc = [jnp.max(w_end[h], axis=0, keepdims=True) for h in heads]
    m_new = [jnp.maximum(b_last[h] + m_prev[h], m_loc[h]) for h in heads]
    ev = [(jnp.exp(w_end[h] - m_loc[h]) * vaug[h].astype(F32)).astype(BF16) for h in heads]
    c_loc = [lax.dot_general(kh[h], ev[h], (((0,), (0,)), ((), ())), preferred_element_type=F32)
             for h in heads]

    p = [(sc[h] * jnp.exp(dm[h] - m_t[h])).astype(BF16) for h in heads]
    out = [jnp.dot(p[h], vaug[h], preferred_element_type=F32) + jnp.exp(inter_log[h] - m_t[h]) * inter[h]
           for h in heads]
    for h in heads:
        ct_ref[h] = (jnp.exp(b_last[h] + m_prev[h] - m_new[h]) * ct[h]
                     + jnp.exp(m_loc[h] - m_new[h]) * c_loc[h])
        m_ref[h:h + 1, :] = jnp.broadcast_to(m_new[h], (1, LANES))
    hh = [out[h][:, :DH] / jnp.maximum(jnp.abs(out[h][:, DH:DH + 1]), jnp.exp(-m_t[h])) for h in heads]
    hn = [hh[h] * lax.rsqrt(jnp.mean(hh[h] * hh[h], axis=-1, keepdims=True) + EPS) * gm[:, sls[h]]
          for h in heads]
    for h in heads:
        og = zq_ref[:, 3 * D_MLSTM + h * DH:3 * D_MLSTM + (h + 1) * DH].astype(F32)
        hm_ref[:, sls[h]] = (hn[h] * _sigmoid(og)).astype(BF16)


def _mlstm(zq, zg, w_conv, b_conv, b_gates, g_mix, tri):
    L = L_MIX
    nc = SEQ // L
    return pl.pallas_call(
        _mlstm_kernel,
        grid=(BATCH, nc),
        in_specs=[
            pl.BlockSpec((L, 4 * D_MLSTM), lambda b, j: (b * nc + j, 0)),
            pl.BlockSpec((L, LANES), lambda b, j: (b * nc + j, 0)),
            pl.BlockSpec((CONV_W, 2 * D_MLSTM), lambda b, j: (0, 0)),
            pl.BlockSpec((1, 2 * D_MLSTM), lambda b, j: (0, 0)),
            pl.BlockSpec((1, LANES), lambda b, j: (0, 0)),
            pl.BlockSpec((1, D_MLSTM), lambda b, j: (0, 0)),
            pl.BlockSpec((L, L), lambda b, j: (0, 0)),
        ],
        out_specs=pl.BlockSpec((L, D_MLSTM), lambda b, j: (b * nc + j, 0)),
        out_shape=jax.ShapeDtypeStruct((TOKENS, D_MLSTM), BF16),
        scratch_shapes=[
            pltpu.VMEM((N_HEADS, DH, 2 * DH), F32),
            pltpu.VMEM((SUBLANES, LANES), F32),
            pltpu.VMEM((SUBLANES + L, 2 * D_MLSTM), F32),
        ],
        compiler_params=pltpu.CompilerParams(
            dimension_semantics=("arbitrary", "arbitrary"), vmem_limit_bytes=VMEM_LIMIT),
        name="mlstm",
    )(zq, zg, w_conv, b_conv, b_gates, g_mix, tri)


def _lru_kernel(zl_ref, wc_ref, bc_ref, wax_ref, bax_ref, lam_ref, gm_ref, hl_ref,
                ext_ref, hc_ref):
    L = L_MIX

    @pl.when(pl.program_id(1) == 0)
    def _():
        ext_ref[:SUBLANES, :] = jnp.zeros((SUBLANES, ext_ref.shape[1]), F32)
        hc_ref[...] = jnp.zeros_like(hc_ref)

    gl = zl_ref[:, D_LRU:].astype(F32)
    xc = _causal_conv(zl_ref[:, :D_LRU].astype(F32), ext_ref, wc_ref[...], bc_ref[...])

    gates = jnp.dot(xc.astype(BF16), wax_ref[...], preferred_element_type=F32) + bax_ref[...]
    r = _sigmoid(gates[:, :D_LRU])
    i = _sigmoid(gates[:, D_LRU:])
    lam = lam_ref[...]
    sp = jnp.maximum(-lam, 0.0) + jnp.log1p(jnp.exp(-jnp.abs(lam)))
    log_a = -LRU_C * r * sp
    a = jnp.exp(log_a)
    y2 = 2.0 * log_a
    series = -y2 * (1.0 + y2 * (0.5 + y2 * (1.0 / 6.0 + y2 * (1.0 / 24.0))))
    u = jnp.sqrt(jnp.where(y2 > -0.01, series, 1.0 - a * a)) * (i * xc)

    r8 = lax.broadcasted_iota(jnp.int32, (L, D_LRU), 0) & (SUBLANES - 1)
    for s in (1, 2, 4):
        a_sh = pltpu.roll(a, s, 0)
        u_sh = pltpu.roll(u, s, 0)
        valid = r8 >= s
        u = jnp.where(valid, a * u_sh + u, u)
        a = jnp.where(valid, a * a_sh, a)
    hc = hc_ref[...]
    hs = []
    for gi in range(L // SUBLANES):
        sl = slice(gi * SUBLANES, (gi + 1) * SUBLANES)
        hg = u[sl] + a[sl] * hc
        hs.append(hg)
        hc = hg[SUBLANES - 1:SUBLANES]
    hc_ref[...] = hc
    hseq = jnp.concatenate(hs, axis=0)

    gelu = 0.5 * gl * (1.0 + jnp.tanh(0.7978845608028654 * (gl + 0.044715 * gl * gl * gl)))
    hl = hseq * gelu
    hl = hl * lax.rsqrt(jnp.mean(hl * hl, axis=-1, keepdims=True) + EPS) * gm_ref[...]
    hl_ref[...] = hl.astype(BF16)


def _lru(zl, w_conv, b_conv, wax, bax, lam, g_mix):
    L = L_MIX
    nc = SEQ // L
    return pl.pallas_call(
        _lru_kernel,
        grid=(BATCH, nc),
        in_specs=[
            pl.BlockSpec((L, 2 * D_LRU), lambda b, j: (b * nc + j, 0)),
            pl.BlockSpec((CONV_W, D_LRU), lambda b, j: (0, 0)),
            pl.BlockSpec((1, D_LRU), lambda b, j: (0, 0)),
            pl.BlockSpec((D_LRU, 2 * D_LRU), lambda b, j: (0, 0)),
            pl.BlockSpec((1, 2 * D_LRU), lambda b, j: (0, 0)),
            pl.BlockSpec((1, D_LRU), lambda b, j: (0, 0)),
            pl.BlockSpec((1, D_LRU), lambda b, j: (0, 0)),
        ],
        out_specs=pl.BlockSpec((L, D_LRU), lambda b, j: (b * nc + j, 0)),
        out_shape=jax.ShapeDtypeStruct((TOKENS, D_LRU), BF16),
        scratch_shapes=[
            pltpu.VMEM((SUBLANES + L, D_LRU), F32),
            pltpu.VMEM((1, D_LRU), F32),
        ],
        compiler_params=pltpu.CompilerParams(
            dimension_semantics=("arbitrary", "arbitrary"), vmem_limit_bytes=VMEM_LIMIT),
        name="rglru",
    )(zl, w_conv, b_conv, wax, bax, lam, g_mix)


def _outproj_kernel(hm_ref, hl_ref, x_ref, mod_ref, wm_ref, wl_ref, o_ref):
    y = (jnp.dot(hm_ref[...], wm_ref[...], preferred_element_type=F32)
         + jnp.dot(hl_ref[...], wl_ref[...], preferred_element_type=F32))
    o_ref[...] = x_ref[...] + mod_ref[2:3, :] * y


def _outproj(hm, hl, x, mod, wm, wl):
    tm = TM_PROJ
    per_b = SEQ // tm
    return pl.pallas_call(
        _outproj_kernel,
        grid=(TOKENS // tm,),
        in_specs=[
            pl.BlockSpec((tm, D_MLSTM), lambda i: (i, 0)),
            pl.BlockSpec((tm, D_LRU), lambda i: (i, 0)),
            pl.BlockSpec((tm, D_MODEL), lambda i: (i, 0)),
            pl.BlockSpec((None, 6, D_MODEL), lambda i: (i // per_b, 0, 0)),
            pl.BlockSpec((D_MLSTM, D_MODEL), lambda i: (0, 0)),
            pl.BlockSpec((D_LRU, D_MODEL), lambda i: (0, 0)),
        ],
        out_specs=pl.BlockSpec((tm, D_MODEL), lambda i: (i, 0)),
        out_shape=jax.ShapeDtypeStruct((TOKENS, D_MODEL), F32),
        compiler_params=pltpu.CompilerParams(
            dimension_semantics=("arbitrary",), vmem_limit_bytes=VMEM_LIMIT),
        name="outproj",
    )(hm, hl, x, mod, wm, wl)


def _ffn_kernel(x_ref, mod_ref, g_ref, wg_ref, wu_ref, wd_ref, o_ref, h2_ref, acc_ref):
    j = pl.program_id(1)

    @pl.when(j == 0)
    def _():
        h = _mod_norm(x_ref[...], g_ref[...], mod_ref[4:5, :], mod_ref[3:4, :])
        h2_ref[...] = h.astype(BF16)
        acc_ref[...] = jnp.zeros_like(acc_ref)

    hb = h2_ref[...]
    g = jnp.dot(hb, wg_ref[...], preferred_element_type=F32)
    u = jnp.dot(hb, wu_ref[...], preferred_element_type=F32)
    act = g * _sigmoid(g) * u
    acc_ref[...] += jnp.dot(act.astype(BF16), wd_ref[...], preferred_element_type=F32)

    @pl.when(j == pl.num_programs(1) - 1)
    def _():
        o_ref[...] = x_ref[...] + mod_ref[5:6, :] * acc_ref[...]


def _ffn(x, mod, g, wg, wu, wd):
    tm, tf = TM_FFN, TF_FFN
    per_b = SEQ // tm
    return pl.pallas_call(
        _ffn_kernel,
        grid=(TOKENS // tm, D_FF // tf),
        in_specs=[
            pl.BlockSpec((tm, D_MODEL), lambda i, j: (i, 0)),
            pl.BlockSpec((None, 6, D_MODEL), lambda i, j: (i // per_b, 0, 0)),
            pl.BlockSpec((1, D_MODEL), lambda i, j: (0, 0)),
            pl.BlockSpec((D_MODEL, tf), lambda i, j: (0, j)),
            pl.BlockSpec((D_MODEL, tf), lambda i, j: (0, j)),
            pl.BlockSpec((tf, D_MODEL), lambda i, j: (j, 0)),
        ],
        out_specs=pl.BlockSpec((tm, D_MODEL), lambda i, j: (i, 0)),
        out_shape=jax.ShapeDtypeStruct((TOKENS, D_MODEL), F32),
        scratch_shapes=[pltpu.VMEM((tm, D_MODEL), BF16), pltpu.VMEM((tm, D_MODEL), F32)],
        compiler_params=pltpu.CompilerParams(
            dimension_semantics=("arbitrary", "arbitrary"), vmem_limit_bytes=VMEM_LIMIT),
        name="ffn",
    )(x, mod, g, wg, wu, wd)


def _router_kernel(x_ref, mod_ref, g_ref, wrh_ref, wrl_ref, br_ref, stri_ref, upper_ref,
                   h_ref, meta_ref, tab_ref, carry_ref):
    @pl.when(pl.program_id(0) == 0)
    def _():
        carry_ref[...] = jnp.zeros_like(carry_ref)

    tm = x_ref.shape[0]
    lane = lax.broadcasted_iota(jnp.int32, (tm, LANES), 1)
    h = _mod_norm(x_ref[...], g_ref[...], mod_ref[4:5, :], mod_ref[3:4, :])
    hb = h.astype(BF16)
    h_ref[...] = hb
    hlo = (h - hb.astype(F32)).astype(BF16)
    wrh = wrh_ref[...]
    logits = (jnp.dot(hb, wrh, preferred_element_type=F32)
              + jnp.dot(hlo, wrh, preferred_element_type=F32)
              + jnp.dot(hb, wrl_ref[...], preferred_element_type=F32)) + br_ref[...]
    logits = jnp.where(lane < N_EXPERTS, logits, -jnp.inf)
    m1 = jnp.max(logits, axis=-1, keepdims=True)
    i1 = jnp.min(jnp.where(logits == m1, lane, LANES), axis=-1, keepdims=True)
    rest = jnp.where(lane == i1, -jnp.inf, logits)
    m2 = jnp.max(rest, axis=-1, keepdims=True)
    i2 = jnp.min(jnp.where(rest == m2, lane, LANES), axis=-1, keepdims=True)
    e2 = jnp.exp(m2 - m1)
    p1 = 1.0 / (1.0 + e2)
    p2 = e2 * p1
    ind = jnp.where((lane == i1) | (lane == i2), 1.0, 0.0)
    n_e = jnp.floor((jnp.sum(ind, axis=0, keepdims=True) + (SUBLANES - 1)) * (1.0 / SUBLANES)) * SUBLANES
    lrank = jnp.dot(stri_ref[...], ind.astype(BF16), preferred_element_type=F32)
    loff = jnp.dot(jnp.broadcast_to(n_e, (SUBLANES, LANES)).astype(BF16), upper_ref[...],
                   preferred_element_type=F32)[0:1, :]
    row = lrank + loff
    d1 = jnp.sum(jnp.where(lane == i1, row, 0.0), axis=-1, keepdims=True)
    d2 = jnp.sum(jnp.where(lane == i2, row, 0.0), axis=-1, keepdims=True)
    vals = (i1.astype(F32), i2.astype(F32), d1, d2, p1, p2)
    meta = jnp.zeros((tm, LANES), F32)
    for n, v in enumerate(vals):
        meta = jnp.where(lane == n, v, meta)
    meta_ref[...] = meta
    carry = carry_ref[0:1, :]
    srow = lax.broadcasted_iota(jnp.int32, (SUBLANES, LANES), 0)
    tab_ref[...] = jnp.where(srow == 0, n_e, jnp.where(srow == 1, carry, jnp.where(srow == 2, loff, 0.0)))
    carry_ref[0:1, :] = carry + n_e


def _router(x, mod, g, wrh, wrl, br, stri, upper):
    tm = TM_ROUTE
    per_b = SEQ // tm
    row = lambda shape: pl.BlockSpec(shape, lambda i: (0, 0))
    return pl.pallas_call(
        _router_kernel,
        grid=(TOKENS // tm,),
        in_specs=[
            pl.BlockSpec((tm, D_MODEL), lambda i: (i, 0)),
            pl.BlockSpec((None, 6, D_MODEL), lambda i: (i // per_b, 0, 0)),
            row((1, D_MODEL)), row((D_MODEL, LANES)), row((D_MODEL, LANES)), row((1, LANES)),
            row((tm, tm)), row((LANES, LANES)),
        ],
        out_specs=[
            pl.BlockSpec((tm, D_MODEL), lambda i: (i, 0)),
            pl.BlockSpec((tm, LANES), lambda i: (i, 0)),
            pl.BlockSpec((None, SUBLANES, LANES), lambda i: (i, 0, 0)),
        ],
        out_shape=[
            jax.ShapeDtypeStruct((TOKENS, D_MODEL), BF16),
            jax.ShapeDtypeStruct((TOKENS, LANES), F32),
            jax.ShapeDtypeStruct((TOKENS // tm, SUBLANES, LANES), F32),
        ],
        scratch_shapes=[pltpu.VMEM((SUBLANES, LANES), F32)],
        compiler_params=pltpu.CompilerParams(
            dimension_semantics=("arbitrary",), vmem_limit_bytes=VMEM_LIMIT),
        name="router",
    )(x, mod, g, wrh, wrl, br, stri, upper)


BLOCK_ROWS = 2 * TM_ROUTE + LANES
RANGE_PIECES = tuple(1 << b for b in range(TM_ROUTE.bit_length() - 1, 2, -1))
assert TM_SLOT <= 2 * RANGE_PIECES[0]


def _range_pieces(n, loc, slot, block_ref, slots_ref, sem, to_slots, visit):
    loc = pl.multiple_of(loc, SUBLANES)
    slot = pl.multiple_of(slot, SUBLANES)
    for piece in RANGE_PIECES:
        has = (n & piece) != 0
        src = block_ref.at[pl.ds(loc, piece)]
        dst = slots_ref.at[pl.ds(slot, piece)]
        if not to_slots:
            src, dst = dst, src

        @pl.when(has)
        def _():
            visit(pltpu.make_async_copy(src, dst, sem))

        step = jnp.where(has, piece, 0)
        loc = pl.multiple_of(loc + step, SUBLANES)
        slot = pl.multiple_of(slot + step, SUBLANES)


def _chunk_copies(chunk, n_ref, loc_ref, slot_ref, block_ref, slots_ref, sem, to_slots, visit):
    for e in range(N_EXPERTS):
        i = chunk * N_EXPERTS + e
        _range_pieces(n_ref[i], loc_ref[i], slot_ref[i], block_ref, slots_ref, sem, to_slots, visit)


def _start(cp):
    cp.start()


def _wait(cp):
    cp.wait()


def _choice_onehots(meta, width):
    col = lax.broadcasted_iota(jnp.int32, (meta.shape[0], width), 1)
    d1 = meta[:, 2:3].astype(jnp.int32)
    d2 = meta[:, 3:4].astype(jnp.int32)
    return col == d1, col == d2


def _zero_copies(padn_ref, pads_ref, used_ref, zero_ref, xs_ref, sem, visit):
    for e in range(N_EXPERTS):
        _range_pieces(padn_ref[e], 0, pads_ref[e], zero_ref, xs_ref, sem, True, visit)
    for t in range(2 * TOKENS // TM_SLOT, N_SLOTS // TM_SLOT):
        @pl.when(t * TM_SLOT >= used_ref[0])
        def _():
            visit(pltpu.make_async_copy(zero_ref, xs_ref.at[pl.ds(t * TM_SLOT, TM_SLOT)], sem))


def _dispatch_kernel(n_ref, loc_ref, slot_ref, padn_ref, pads_ref, used_ref, h_ref, meta_ref, xs_ref,
                     block_ref, zero_ref, sem, zsem):
    i = pl.program_id(0)
    last = pl.num_programs(0) - 1
    cur = i % 2
    zeros = (padn_ref, pads_ref, used_ref, zero_ref, xs_ref, zsem)

    @pl.when(i == 0)
    def _():
        zero_ref[...] = jnp.zeros_like(zero_ref)
        _zero_copies(*zeros, _start)

    a1, a2 = _choice_onehots(meta_ref[...], BLOCK_ROWS)
    sel = (a1 | a2).astype(BF16)
    block_ref[cur] = lax.dot_general(sel, h_ref[...], (((0,), (0,)), ((), ())),
                                     preferred_element_type=F32)
    tabs = (n_ref, loc_ref, slot_ref)
    _chunk_copies(i, *tabs, block_ref.at[cur], xs_ref, sem.at[cur], True, _start)

    @pl.when(i > 0)
    def _():
        _chunk_copies(i - 1, *tabs, block_ref.at[1 - cur], xs_ref, sem.at[1 - cur], True, _wait)

    @pl.when(i == last)
    def _():
        _chunk_copies(i, *tabs, block_ref.at[cur], xs_ref, sem.at[cur], True, _wait)
        _zero_copies(*zeros, _wait)


def _dispatch(tabs, pad_tabs, h, meta):
    tm = TM_ROUTE
    return pl.pallas_call(
        _dispatch_kernel,
        grid_spec=pltpu.PrefetchScalarGridSpec(
            num_scalar_prefetch=6,
            grid=(TOKENS // tm,),
            in_specs=[
                pl.BlockSpec((tm, D_MODEL), lambda i, *_: (i, 0)),
                pl.BlockSpec((tm, LANES), lambda i, *_: (i, 0)),
            ],
            out_specs=pl.BlockSpec(memory_space=pl.ANY),
            scratch_shapes=[pltpu.VMEM((2, BLOCK_ROWS, D_MODEL), F32),
                            pltpu.VMEM((TM_SLOT, D_MODEL), F32),
                            pltpu.SemaphoreType.DMA((2,)), pltpu.SemaphoreType.DMA],
        ),
        out_shape=jax.ShapeDtypeStruct((N_SLOTS, D_MODEL), F32),
        compiler_params=pltpu.CompilerParams(
            dimension_semantics=("arbitrary",), vmem_limit_bytes=VMEM_LIMIT),
        name="dispatch",
    )(*tabs, *pad_tabs, h, meta)


def _experts_kernel(te_ref, tv_ref, ts_ref, xs_ref, wg_ref, wu_ref, wd_ref, o_ref, h2_ref, acc_ref):
    del te_ref, ts_ref
    j = pl.program_id(1)
    last = pl.num_programs(1) - 1
    valid = tv_ref[pl.program_id(0)] != 0

    @pl.when(valid)
    def _():
        @pl.when(j == 0)
        def _():
            h2_ref[...] = xs_ref[...].astype(BF16)
            acc_ref[...] = jnp.zeros_like(acc_ref)

        hb = h2_ref[...]
        g = jnp.dot(hb, wg_ref[...], preferred_element_type=F32)
        u = jnp.dot(hb, wu_ref[...], preferred_element_type=F32)
        act = g * _sigmoid(g) * u
        acc_ref[...] += jnp.dot(act.astype(BF16), wd_ref[...], preferred_element_type=F32)

        @pl.when(j == last)
        def _():
            o_ref[...] = acc_ref[...]

    @pl.when(jnp.logical_not(valid) & (j == last))
    def _():
        o_ref[...] = jnp.zeros_like(o_ref)


def _experts(tile_e, tile_v, tile_src, xs, wg, wu, wd):
    tm, tf = TM_SLOT, TF_FFN
    return pl.pallas_call(
        _experts_kernel,
        grid_spec=pltpu.PrefetchScalarGridSpec(
            num_scalar_prefetch=3,
            grid=(N_SLOTS // tm, D_FF // tf),
            in_specs=[
                pl.BlockSpec((tm, D_MODEL), lambda i, j, te, tv, ts: (ts[i], 0)),
                pl.BlockSpec((None, D_MODEL, tf), lambda i, j, te, tv, ts: (te[i], 0, j)),
                pl.BlockSpec((None, D_MODEL, tf), lambda i, j, te, tv, ts: (te[i], 0, j)),
                pl.BlockSpec((None, tf, D_MODEL), lambda i, j, te, tv, ts: (te[i], j, 0)),
            ],
            out_specs=pl.BlockSpec((tm, D_MODEL), lambda i, j, te, tv, ts: (i, 0)),
            scratch_shapes=[pltpu.VMEM((tm, D_MODEL), BF16), pltpu.VMEM((tm, D_MODEL), F32)],
        ),
        out_shape=jax.ShapeDtypeStruct((N_SLOTS, D_MODEL), F32),
        compiler_params=pltpu.CompilerParams(
            dimension_semantics=("arbitrary", "arbitrary"), vmem_limit_bytes=VMEM_LIMIT),
        name="experts",
    )(tile_e, tile_v, tile_src, xs, wg, wu, wd)


def _combine_kernel(n_ref, loc_ref, slot_ref, x_ref, mod_ref, meta_ref, gf_ref, ys_ref, o_ref,
                    block_ref, sem):
    i = pl.program_id(0)
    cur = i % 2
    tabs = (n_ref, loc_ref, slot_ref)

    @pl.when(i == 0)
    def _():
        block_ref[...] = jnp.zeros_like(block_ref)
        _chunk_copies(0, *tabs, block_ref.at[0], ys_ref, sem.at[0], False, _start)

    @pl.when(i + 1 < pl.num_programs(0))
    def _():
        _chunk_copies(i + 1, *tabs, block_ref.at[1 - cur], ys_ref, sem.at[1 - cur], False, _start)

    _chunk_copies(i, *tabs, block_ref.at[cur], ys_ref, sem.at[cur], False, _wait)
    meta = meta_ref[...]
    a1, a2 = _choice_onehots(meta, BLOCK_ROWS)
    blk = block_ref[cur].astype(BF16)
    y = (meta[:, 4:5] * jnp.dot(a1.astype(BF16), blk, preferred_element_type=F32)
         + meta[:, 5:6] * jnp.dot(a2.astype(BF16), blk, preferred_element_type=F32))
    y = x_ref[...] + mod_ref[5:6, :] * y
    o_ref[...] = y * lax.rsqrt(jnp.mean(y * y, axis=-1, keepdims=True) + EPS) * gf_ref[...]


def _combine(tabs, x, mod, meta, g_final, ys):
    tm = TM_ROUTE
    per_b = SEQ // tm
    return pl.pallas_call(
        _combine_kernel,
        grid_spec=pltpu.PrefetchScalarGridSpec(
            num_scalar_prefetch=3,
            grid=(TOKENS // tm,),
            in_specs=[
                pl.BlockSpec((tm, D_MODEL), lambda i, *_: (i, 0)),
                pl.BlockSpec((None, 6, D_MODEL), lambda i, *_: (i // per_b, 0, 0)),
                pl.BlockSpec((tm, LANES), lambda i, *_: (i, 0)),
                pl.BlockSpec((1, D_MODEL), lambda i, *_: (0, 0)),
                pl.BlockSpec(memory_space=pl.ANY),
            ],
            out_specs=pl.BlockSpec((tm, D_MODEL), lambda i, *_: (i, 0)),
            scratch_shapes=[pltpu.VMEM((2, BLOCK_ROWS, D_MODEL), F32), pltpu.SemaphoreType.DMA((2,))],
        ),
        out_shape=jax.ShapeDtypeStruct((TOKENS, D_MODEL), F32),
        compiler_params=pltpu.CompilerParams(
            dimension_semantics=("arbitrary",), vmem_limit_bytes=VMEM_LIMIT),
        name="combine",
    )(*tabs, x, mod, meta, g_final, ys)


def _moe(x, mod, g, w_router, b_router, wg, wu, wd, g_final):
    wr = _pad_lanes(w_router)
    wrh = wr.astype(BF16)
    wrl = (wr - wrh.astype(F32)).astype(BF16)
    stri = (lax.broadcasted_iota(jnp.int32, (TM_ROUTE, TM_ROUTE), 1)
            < lax.broadcasted_iota(jnp.int32, (TM_ROUTE, TM_ROUTE), 0)).astype(BF16)
    upper = (lax.broadcasted_iota(jnp.int32, (LANES, LANES), 0)
             < lax.broadcasted_iota(jnp.int32, (LANES, LANES), 1)).astype(BF16)
    h, meta, tab = _router(x, mod, g, wrh, wrl, _pad_lanes(b_router[None]), stri, upper)
    tab = tab[:, :3, :N_EXPERTS].astype(jnp.int32)
    n_ce, before_ce, loc_ce = tab[:, 0], tab[:, 1], tab[:, 2]
    counts = before_ce[-1] + n_ce[-1]
    padded = (counts + TM_SLOT - 1) // TM_SLOT * TM_SLOT
    ends = jnp.cumsum(padded)
    slot_ce = (ends - padded)[None, :] + before_ce
    tabs = (n_ce.reshape(-1), loc_ce.reshape(-1), slot_ce.reshape(-1))
    pad_tabs = (padded - counts, ends - padded + counts, ends[-1:])
    tile = jnp.arange(N_SLOTS // TM_SLOT, dtype=jnp.int32)
    tile_e = jnp.minimum(jnp.sum(tile[:, None] * TM_SLOT >= ends[None, :], axis=1), N_EXPERTS - 1)
    tile_v = (tile * TM_SLOT < ends[-1]).astype(jnp.int32)
    tile_src = jnp.minimum(tile, ends[-1] // TM_SLOT - 1)
    xs = _dispatch(tabs, pad_tabs, h, meta)
    ys = _experts(tile_e.astype(jnp.int32), tile_v, tile_src, xs, wg, wu, wd)
    return _combine(tabs, x, mod, meta, g_final, ys)


def _block_diag(w):
    eye = jnp.eye(N_LRU_BLOCKS, dtype=w.dtype)
    return jnp.einsum('nde,nm->ndme', w, eye).reshape(D_LRU, D_LRU)


def _pad_lanes(a):
    return jnp.zeros(a.shape[:-1] + (LANES,), a.dtype).at[..., :a.shape[-1]].set(a)


def kernel(x, c, w_ada, b_ada, g_norm_mix, g_norm_ffn, w_in, w_conv_qk, b_conv_qk, b_gates,
           w_conv_lru, b_conv_lru, w_lru_a, b_lru_a, w_lru_x, b_lru_x, lru_lambda, g_mix_out, w_out,
           w_ff_gate, w_ff_up, w_ff_down, w_router, b_router, w_exp_gate, w_exp_up, w_exp_down, g_final):
    assert DEPTH == 2
    mods = _ada(c, w_ada, b_ada)
    tri = (lax.broadcasted_iota(jnp.int32, (L_MIX, L_MIX), 1)
           <= lax.broadcasted_iota(jnp.int32, (L_MIX, L_MIX), 0)).astype(BF16)
    xt = x.reshape(TOKENS, D_MODEL)
    for l in range(DEPTH):
        mod = mods[l]
        wi = w_in[l]
        w1 = wi[:, :4 * D_MLSTM].astype(BF16)
        w3 = _pad_lanes(wi[:, 4 * D_MLSTM:4 * D_MLSTM + 2 * N_HEADS]).astype(BF16)
        w2 = wi[:, 4 * D_MLSTM + 2 * N_HEADS:].astype(BF16)
        zq, zl, zg = _inproj(xt, mod, g_norm_mix[l][None], w1, w2, w3)
        hm = _mlstm(zq, zg, w_conv_qk[l], b_conv_qk[l][None], _pad_lanes(b_gates[l][None]),
                    g_mix_out[l][None, :D_MLSTM], tri)
        wax = jnp.concatenate([_block_diag(w_lru_a[l]), _block_diag(w_lru_x[l])], axis=1).astype(BF16)
        bax = jnp.concatenate([b_lru_a[l], b_lru_x[l]])[None]
        hl = _lru(zl, w_conv_lru[l], b_conv_lru[l][None], wax, bax, lru_lambda[l][None],
                  g_mix_out[l][None, D_MLSTM:])
        wo = w_out[l].astype(BF16)
        xt = _outproj(hm, hl, xt, mod, wo[:D_MLSTM], wo[D_MLSTM:])
        jj = l // 2
        if l % 2 == 0:
            xt = _ffn(xt, mod, g_norm_ffn[l][None], w_ff_gate[jj].astype(BF16),
                      w_ff_up[jj].astype(BF16), w_ff_down[jj].astype(BF16))
        else:
            xt = _moe(xt, mod, g_norm_ffn[l][None], w_router[jj], b_router[jj],
                      w_exp_gate[jj].astype(BF16), w_exp_up[jj].astype(BF16),
                      w_exp_down[jj].astype(BF16), g_final[None])
    return xt.reshape(BATCH, SEQ, D_MODEL)
```

```python
import functools

import jax
import jax.numpy as jnp
from jax import lax
from jax.experimental import pallas as pl
from jax.experimental.pallas import tpu as pltpu

F32 = jnp.float32
BF16 = jnp.bfloat16

D_MODEL = 1024
BATCH = 4
SEQ = 8192
TOKENS = BATCH * SEQ
DEPTH = 2
D_MLSTM = 512
N_HEADS = 4
DH = 128
D_LRU = 512
N_LRU_BLOCKS = 8
DB_LRU = 64
CONV_W = 4
LRU_C = 8.0
D_FF = 2816
N_EXPERTS = 8
EPS = 1e-6

LANES = 128
SUBLANES = 8
VMEM_LIMIT = 56 * 1024 * 1024

L_MIX = 256
PROJ_CHUNK = 256
TM_FFN = 512
TM_ROUTE = 256
TM_SLOT = 512
N_SLOTS = 2 * TOKENS + (TOKENS // TM_ROUTE) * N_EXPERTS * SUBLANES + N_EXPERTS * TM_SLOT
TF_FFN = 1408


def _sigmoid(x):
    return 1.0 / (1.0 + jnp.exp(-x))


def _mod_norm(x, g, scale, shift):
    ms = jnp.mean(x * x, axis=-1, keepdims=True)
    return (x * lax.rsqrt(ms + EPS)) * g * (1.0 + scale) + shift


def _causal_conv(x, ext_ref, w, b):
    L = x.shape[0]
    ext_ref[SUBLANES:, :] = x
    acc = b + w[CONV_W - 1:CONV_W] * x
    for s in range(1, CONV_W):
        acc = acc + w[CONV_W - 1 - s:CONV_W - s] * ext_ref[SUBLANES - s:SUBLANES - s + L, :]
    ext_ref[:SUBLANES, :] = x[L - SUBLANES:, :]
    return acc


def _zero_tile(z):
    bits = lax.bitcast_convert_type(z[-SUBLANES:, -LANES:], jnp.uint32)
    return ((bits >> 16) >> 16).astype(F32)


def _after(v, zeros):
    if not zeros:
        return v
    tile = v[:SUBLANES, :LANES]
    while zeros:
        tile = tile + zeros.pop()
    head = tile if v.shape[1] == LANES else jnp.concatenate([tile, v[:SUBLANES, LANES:]], axis=1)
    return jnp.concatenate([head, v[SUBLANES:]], axis=0)


def _ada_kernel(c_ref, w_ref, b_ref, o_ref):
    c = c_ref[...]
    cs = c * _sigmoid(c)
    o_ref[...] = jnp.dot(cs, w_ref[...], preferred_element_type=F32) + b_ref[...]


def _ada(c, w_ada, b_ada):
    tn = 1536
    c8 = jnp.zeros((SUBLANES, D_MODEL), F32).at[:BATCH].set(c)
    out = pl.pallas_call(
        _ada_kernel,
        grid=(DEPTH, 6 * D_MODEL // tn),
        in_specs=[
            pl.BlockSpec((SUBLANES, D_MODEL), lambda l, n: (0, 0)),
            pl.BlockSpec((None, D_MODEL, tn), lambda l, n: (l, 0, n)),
            pl.BlockSpec((None, 1, tn), lambda l, n: (l, 0, n)),
        ],
        out_specs=pl.BlockSpec((None, SUBLANES, tn), lambda l, n: (l, 0, n)),
        out_shape=jax.ShapeDtypeStruct((DEPTH, SUBLANES, 6 * D_MODEL), F32),
        compiler_params=pltpu.CompilerParams(
            dimension_semantics=("arbitrary", "arbitrary"), vmem_limit_bytes=VMEM_LIMIT),
        name="ada",
    )(c8, w_ada, b_ada.reshape(DEPTH, 1, 6 * D_MODEL))
    return out[:, :BATCH].reshape(DEPTH, BATCH, 6, D_MODEL)


def _inproj_lru_kernel(x_ref, mod_ref, g_ref, w_ref, wc_ref, bc_ref, wax_ref, bax_ref, lam_ref, gm_ref,
                       zq_ref, zg_ref, hl_ref, zl_ref, ext_ref, hc_ref):
    L = L_MIX
    j = pl.program_id(1)
    slot = j % 2

    @pl.when((pl.program_id(0) == 0) & (j == 0))
    def _():
        zl_ref[...] = jnp.zeros_like(zl_ref)
        ext_ref[...] = jnp.zeros_like(ext_ref)
        hc_ref[...] = jnp.zeros_like(hc_ref)

    keep = j != 1
    ext_ref[:SUBLANES, :] = jnp.where(keep, ext_ref[:SUBLANES, :], 0.0)
    hc_ref[...] = jnp.where(keep, hc_ref[...], 0.0)
    xc = _causal_conv(zl_ref[1 - slot, :, :D_LRU].astype(F32), ext_ref, wc_ref[...], bc_ref[...])

    mod = mod_ref[...]
    h = _mod_norm(x_ref[...], g_ref[...], mod[1:2], mod[0:1]).astype(BF16)
    n_q, n_l = 4 * D_MLSTM, 2 * D_LRU
    todo = list(range(0, n_q + n_l + LANES, PROJ_CHUNK))
    zeros = []

    def project(n):
        for lo in todo[:n]:
            hi = min(lo + PROJ_CHUNK, n_q + n_l + LANES)
            z = jnp.dot(h, w_ref[:, lo:hi], preferred_element_type=F32)
            zeros.append(_zero_tile(z))
            if hi <= n_q:
                zq_ref[:, lo:hi] = z.astype(BF16)
            elif hi <= n_q + n_l:
                zl_ref[slot, :, lo - n_q:hi - n_q] = z.astype(BF16)
            else:
                zg_ref[...] = z
        del todo[:n]

    project(2)
    xc = _after(xc, zeros)
    gates = jnp.dot(xc.astype(BF16), wax_ref[...], preferred_element_type=F32) + bax_ref[...]
    r = _sigmoid(gates[:, :D_LRU])
    i = _sigmoid(gates[:, D_LRU:])
    project(2)
    r = _after(r, zeros)
    lam = lam_ref[...]
    sp = jnp.maximum(-lam, 0.0) + jnp.log1p(jnp.exp(-jnp.abs(lam)))
    log_a = -LRU_C * r * sp
    a = jnp.exp(log_a)
    y2 = 2.0 * log_a
    series = -y2 * (1.0 + y2 * (0.5 + y2 * (1.0 / 6.0 + y2 * (1.0 / 24.0))))
    u = jnp.sqrt(jnp.where(y2 > -0.01, series, 1.0 - a * a)) * (i * xc)
    project(2)
    u = _after(u, zeros)

    r8 = lax.broadcasted_iota(jnp.int32, (L, D_LRU), 0) & (SUBLANES - 1)
    for s in (1, 2, 4):
        a_sh = pltpu.roll(a, s, 0)
        u_sh = pltpu.roll(u, s, 0)
        valid = r8 >= s
        u = jnp.where(valid, a * u_sh + u, u)
        a = jnp.where(valid, a * a_sh, a)
        project(2)
        u = _after(u, zeros)
    hc = hc_ref[...]
    hs = []
    for gi in range(L // SUBLANES):
        sl = slice(gi * SUBLANES, (gi + 1) * SUBLANES)
        hg = u[sl] + a[sl] * hc
        hs.append(hg)
        hc = hg[SUBLANES - 1:SUBLANES]
    hc_ref[...] = hc
    hseq = jnp.concatenate(hs, axis=0)
    project(len(todo))
    hseq = _after(hseq, zeros)

    gl = zl_ref[1 - slot, :, D_LRU:].astype(F32)
    gelu = 0.5 * gl * (1.0 + jnp.tanh(0.7978845608028654 * (gl + 0.044715 * gl * gl * gl)))
    hl = hseq * gelu
    hl = hl * lax.rsqrt(jnp.mean(hl * hl, axis=-1, keepdims=True) + EPS) * gm_ref[...]
    hl_ref[...] = hl.astype(BF16)


def _inproj_lru(x, mod, g, w, w_conv, b_conv, wax, bax, lam, g_mix):
    L = L_MIX
    nc = SEQ // L
    proj = lambda b, j: (b * nc + jnp.minimum(j, nc - 1), 0)
    mixed = lambda b, j: (b * nc + jnp.maximum(j - 1, 0), 0)
    const = lambda shape: pl.BlockSpec(shape, lambda b, j: (0, 0))
    return pl.pallas_call(
        _inproj_lru_kernel,
        grid=(BATCH, nc + 1),
        in_specs=[
            pl.BlockSpec((L, D_MODEL), proj),
            pl.BlockSpec((None, 6, D_MODEL), lambda b, j: (b, 0, 0)),
            const((1, D_MODEL)), const(w.shape),
            const((CONV_W, D_LRU)), const((1, D_LRU)), const((D_LRU, 2 * D_LRU)),
            const((1, 2 * D_LRU)), const((1, D_LRU)), const((1, D_LRU)),
        ],
        out_specs=[
            pl.BlockSpec((L, 4 * D_MLSTM), proj),
            pl.BlockSpec((L, LANES), proj),
            pl.BlockSpec((L, D_LRU), mixed),
        ],
        out_shape=[
            jax.ShapeDtypeStruct((TOKENS, 4 * D_MLSTM), BF16),
            jax.ShapeDtypeStruct((TOKENS, LANES), F32),
            jax.ShapeDtypeStruct((TOKENS, D_LRU), BF16),
        ],
        scratch_shapes=[
            pltpu.VMEM((2, L, 2 * D_LRU), BF16),
            pltpu.VMEM((SUBLANES + L, D_LRU), F32),
            pltpu.VMEM((1, D_LRU), F32),
        ],
        compiler_params=pltpu.CompilerParams(
            dimension_semantics=("arbitrary", "arbitrary"), vmem_limit_bytes=VMEM_LIMIT),
        name="inproj_rglru",
    )(x, mod, g, w, w_conv, b_conv, wax, bax, lam, g_mix)


def _mlstm_outproj_kernel(zq_ref, zg_ref, wc_ref, bc_ref, bg_ref, gm_ref, tri_ref,
                          hl_ref, x_ref, mod_ref, wm_ref, wl_ref, o_ref,
                          ct_ref, m_ref, ext_ref, hm_ref):
    L = L_MIX
    j = pl.program_id(1)
    slot = j % 2

    @pl.when((pl.program_id(0) == 0) & (j == 0))
    def _():
        hm_ref[...] = jnp.zeros_like(hm_ref)

    @pl.when(j == 0)
    def _():
        ct_ref[...] = jnp.zeros_like(ct_ref)
        m_ref[...] = jnp.zeros_like(m_ref)
        ext_ref[:SUBLANES, :] = jnp.zeros((SUBLANES, ext_ref.shape[1]), F32)

    todo = list(range(0, D_MODEL, PROJ_CHUNK))
    zeros = []

    def project(n):
        for lo in todo[:n]:
            cs = slice(lo, lo + PROJ_CHUNK)
            y = (jnp.dot(hm_ref[1 - slot], wm_ref[:, cs], preferred_element_type=F32)
                 + jnp.dot(hl_ref[...], wl_ref[:, cs], preferred_element_type=F32))
            zeros.append(_zero_tile(y))
            o_ref[:, cs] = x_ref[:, cs] + mod_ref[2:3, cs] * y
        del todo[:n]

    project(1)
    qk = _causal_conv(zq_ref[:, :2 * D_MLSTM].astype(F32), ext_ref, wc_ref[...], bc_ref[...])
    qk = _after(qk, zeros)
    project(1)
    qk = qk * _sigmoid(qk)
    q = _after(qk[:, :D_MLSTM], zeros).astype(BF16)
    k = (qk[:, D_MLSTM:] * (DH ** -0.5)).astype(BF16)
    project(1)

    g = zg_ref[...] + bg_ref[...]
    lf = jnp.minimum(g, 0.0) - jnp.log1p(jnp.exp(-jnp.abs(g)))
    tri = tri_ref[...]
    hi = lf.astype(BF16)
    r1 = lf - hi.astype(F32)
    mid = r1.astype(BF16)
    lo = (r1 - mid.astype(F32)).astype(BF16)
    bcum = (jnp.dot(tri, hi, preferred_element_type=F32)
            + jnp.dot(tri, mid, preferred_element_type=F32)
            + jnp.dot(tri, lo, preferred_element_type=F32))
    lane = lax.broadcasted_iota(jnp.int32, (L, LANES), 1)
    cols = _after(jnp.where(lane < N_HEADS, g, bcum), zeros)
    project(len(todo))
    rows = jnp.transpose(cols)

    t_idx = lax.broadcasted_iota(jnp.int32, (L, L), 0)
    s_idx = lax.broadcasted_iota(jnp.int32, (L, L), 1)
    causal = s_idx <= t_idx
    ones_col = (lax.broadcasted_iota(jnp.int32, (L, DH), 1) == 0).astype(BF16)
    gm = gm_ref[...]

    heads = range(N_HEADS)
    sls = [slice(h * DH, (h + 1) * DH) for h in heads]
    ic = [cols[:, h:h + 1] for h in heads]
    bc = [cols[:, N_HEADS + h:N_HEADS + h + 1] for h in heads]
    ir = [rows[h:h + 1, :] for h in heads]
    br = [rows[N_HEADS + h:N_HEADS + h + 1, :] for h in heads]
    b_last = [br[h][:, L - 1:L] for h in heads]
    m_prev = [m_ref[h:h + 1, 0:1] for h in heads]
    qh = [q[:, sls[h]] for h in heads]
    kh = [k[:, sls[h]] for h in heads]
    vaug = [jnp.concatenate([zq_ref[:, 2 * D_MLSTM + h * DH:2 * D_MLSTM + (h + 1) * DH], ones_col], axis=1)
            for h in heads]
    ct = [ct_ref[h] for h in heads]

    dm = [jnp.where(causal, bc[h] - br[h] + ir[h], -jnp.inf) for h in heads]
    dm[0] = _after(dm[0], zeros)
    inter_log = [bc[h] + m_prev[h] for h in heads]
    sc = [lax.dot_general(qh[h], kh[h], (((1,), (1,)), ((), ())), preferred_element_type=F32) for h in heads]
    inter = [jnp.dot(qh[h], ct[h].astype(BF16), preferred_element_type=F32) for h in heads]
    m_t = [jnp.maximum(inter_log[h], jnp.max(dm[h], axis=1, keepdims=True)) for h in heads]

    w_end = [b_last[h] - bc[h] + ic[h] for h in heads]
    m_loc = [jnp.max(w_end[h], axis=0, keepdims=True) for h in heads]
    m_new = [jnp.maximum(b_last[h] + m_prev[h], m_loc[h]) for h in heads]
    ev = [(jnp.exp(w_end[h] - m_loc[h]) * vaug[h].astype(F32)).astype(BF16) for h in heads]
    c_loc = [lax.dot_general(kh[h], ev[h], (((0,), (0,)), ((), ())), preferred_element_type=F32)
             for h in heads]

    p = [(sc[h] * jnp.exp(dm[h] - m_t[h])).astype(BF16) for h in heads]
    out = [jnp.dot(p[h], vaug[h], preferred_element_type=F32) + jnp.exp(inter_log[h] - m_t[h]) * inter[h]
           for h in heads]
    for h in heads:
        ct_ref[h] = (jnp.exp(b_last[h] + m_prev[h] - m_new[h]) * ct[h]
                     + jnp.exp(m_loc[h] - m_new[h]) * c_loc[h])
        m_ref[h:h + 1, :] = jnp.broadcast_to(m_new[h], (1, LANES))
    hh = [out[h][:, :DH] / jnp.maximum(jnp.abs(out[h][:, DH:DH + 1]), jnp.exp(-m_t[h])) for h in heads]
    hn = [hh[h] * lax.rsqrt(jnp.mean(hh[h] * hh[h], axis=-1, keepdims=True) + EPS) * gm[:, sls[h]]
          for h in heads]
    for h in heads:
        og = zq_ref[:, 3 * D_MLSTM + h * DH:3 * D_MLSTM + (h + 1) * DH].astype(F32)
        hm_ref[slot, :, sls[h]] = (hn[h] * _sigmoid(og)).astype(BF16)


def _mlstm_outproj(zq, zg, w_conv, b_conv, b_gates, g_mix, tri, hl, x, mod, wm, wl):
    L = L_MIX
    nc = SEQ // L
    mixed = lambda b, j: (b * nc + jnp.minimum(j, nc - 1), 0)
    projected = lambda b, j: (b * nc + jnp.maximum(j - 1, 0), 0)
    const = lambda shape: pl.BlockSpec(shape, lambda b, j: (0, 0))
    return pl.pallas_call(
        _mlstm_outproj_kernel,
        grid=(BATCH, nc + 1),
        in_specs=[
            pl.BlockSpec((L, 4 * D_MLSTM), mixed),
            pl.BlockSpec((L, LANES), mixed),
            const((CONV_W, 2 * D_MLSTM)), const((1, 2 * D_MLSTM)), const((1, LANES)), const((1, D_MLSTM)),
            const((L, L)),
            pl.BlockSpec((L, D_LRU), projected),
            pl.BlockSpec((L, D_MODEL), projected),
            pl.BlockSpec((None, 6, D_MODEL), lambda b, j: (b, 0, 0)),
            const((D_MLSTM, D_MODEL)), const((D_LRU, D_MODEL)),
        ],
        out_specs=pl.BlockSpec((L, D_MODEL), projected),
        out_shape=jax.ShapeDtypeStruct((TOKENS, D_MODEL), F32),
        scratch_shapes=[
            pltpu.VMEM((N_HEADS, DH, 2 * DH), F32),
            pltpu.VMEM((SUBLANES, LANES), F32),
            pltpu.VMEM((SUBLANES + L, 2 * D_MLSTM), F32),
            pltpu.VMEM((2, L, D_MLSTM), BF16),
        ],
        compiler_params=pltpu.CompilerParams(
            dimension_semantics=("arbitrary", "arbitrary"), vmem_limit_bytes=VMEM_LIMIT),
        name="mlstm_outproj",
    )(zq, zg, w_conv, b_conv, b_gates, g_mix, tri, hl, x, mod, wm, wl)


def _ffn_kernel(x_ref, mod_ref, g_ref, wg_ref, wu_ref, wd_ref, o_ref, h2_ref, acc_ref):
    j = pl.program_id(1)

    @pl.when(j == 0)
    def _():
        h = _mod_norm(x_ref[...], g_ref[...], mod_ref[4:5, :], mod_ref[3:4, :])
        h2_ref[...] = h.astype(BF16)
        acc_ref[...] = jnp.zeros_like(acc_ref)

    hb = h2_ref[...]
    g = jnp.dot(hb, wg_ref[...], preferred_element_type=F32)
    u = jnp.dot(hb, wu_ref[...], preferred_element_type=F32)
    act = g * _sigmoid(g) * u
    acc_ref[...] += jnp.dot(act.astype(BF16), wd_ref[...], preferred_element_type=F32)

    @pl.when(j == pl.num_programs(1) - 1)
    def _():
        o_ref[...] = x_ref[...] + mod_ref[5:6, :] * acc_ref[...]


def _ffn(x, mod, g, wg, wu, wd):
    tm, tf = TM_FFN, TF_FFN
    per_b = SEQ // tm
    return pl.pallas_call(
        _ffn_kernel,
        grid=(TOKENS // tm, D_FF // tf),
        in_specs=[
            pl.BlockSpec((tm, D_MODEL), lambda i, j: (i, 0)),
            pl.BlockSpec((None, 6, D_MODEL), lambda i, j: (i // per_b, 0, 0)),
            pl.BlockSpec((1, D_MODEL), lambda i, j: (0, 0)),
            pl.BlockSpec((D_MODEL, tf), lambda i, j: (0, j)),
            pl.BlockSpec((D_MODEL, tf), lambda i, j: (0, j)),
            pl.BlockSpec((tf, D_MODEL), lambda i, j: (j, 0)),
        ],
        out_specs=pl.BlockSpec((tm, D_MODEL), lambda i, j: (i, 0)),
        out_shape=jax.ShapeDtypeStruct((TOKENS, D_MODEL), F32),
        scratch_shapes=[pltpu.VMEM((tm, D_MODEL), BF16), pltpu.VMEM((tm, D_MODEL), F32)],
        compiler_params=pltpu.CompilerParams(
            dimension_semantics=("arbitrary", "arbitrary"), vmem_limit_bytes=VMEM_LIMIT),
        name="ffn",
    )(x, mod, g, wg, wu, wd)


def _router_kernel(x_ref, mod_ref, g_ref, wrh_ref, wrl_ref, br_ref, stri_ref, upper_ref,
                   h_ref, meta_ref, tab_ref, carry_ref):
    @pl.when(pl.program_id(0) == 0)
    def _():
        carry_ref[...] = jnp.zeros_like(carry_ref)

    tm = x_ref.shape[0]
    lane = lax.broadcasted_iota(jnp.int32, (tm, LANES), 1)
    h = _mod_norm(x_ref[...], g_ref[...], mod_ref[4:5, :], mod_ref[3:4, :])
    hb = h.astype(BF16)
    h_ref[...] = hb
    hlo = (h - hb.astype(F32)).astype(BF16)
    wrh = wrh_ref[...]
    logits = (jnp.dot(hb, wrh, preferred_element_type=F32)
              + jnp.dot(hlo, wrh, preferred_element_type=F32)
              + jnp.dot(hb, wrl_ref[...], preferred_element_type=F32)) + br_ref[...]
    logits = jnp.where(lane < N_EXPERTS, logits, -jnp.inf)
    m1 = jnp.max(logits, axis=-1, keepdims=True)
    i1 = jnp.min(jnp.where(logits == m1, lane, LANES), axis=-1, keepdims=True)
    rest = jnp.where(lane == i1, -jnp.inf, logits)
    m2 = jnp.max(rest, axis=-1, keepdims=True)
    i2 = jnp.min(jnp.where(rest == m2, lane, LANES), axis=-1, keepdims=True)
    e2 = jnp.exp(m2 - m1)
    p1 = 1.0 / (1.0 + e2)
    p2 = e2 * p1
    ind = jnp.where((lane == i1) | (lane == i2), 1.0, 0.0)
    n_e = jnp.floor((jnp.sum(ind, axis=0, keepdims=True) + (SUBLANES - 1)) * (1.0 / SUBLANES)) * SUBLANES
    lrank = jnp.dot(stri_ref[...], ind.astype(BF16), preferred_element_type=F32)
    loff = jnp.dot(jnp.broadcast_to(n_e, (SUBLANES, LANES)).astype(BF16), upper_ref[...],
                   preferred_element_type=F32)[0:1, :]
    row = lrank + loff
    d1 = jnp.sum(jnp.where(lane == i1, row, 0.0), axis=-1, keepdims=True)
    d2 = jnp.sum(jnp.where(lane == i2, row, 0.0), axis=-1, keepdims=True)
    vals = (i1.astype(F32), i2.astype(F32), d1, d2, p1, p2)
    meta = jnp.zeros((tm, LANES), F32)
    for n, v in enumerate(vals):
        meta = jnp.where(lane == n, v, meta)
    meta_ref[...] = meta
    carry = carry_ref[0:1, :]
    srow = lax.broadcasted_iota(jnp.int32, (SUBLANES, LANES), 0)
    tab_ref[...] = jnp.where(srow == 0, n_e, jnp.where(srow == 1, carry, jnp.where(srow == 2, loff, 0.0)))
    carry_ref[0:1, :] = carry + n_e


def _router(x, mod, g, wrh, wrl, br, stri, upper):
    tm = TM_ROUTE
    per_b = SEQ // tm
    row = lambda shape: pl.BlockSpec(shape, lambda i: (0, 0))
    return pl.pallas_call(
        _router_kernel,
        grid=(TOKENS // tm,),
        in_specs=[
            pl.BlockSpec((tm, D_MODEL), lambda i: (i, 0)),
            pl.BlockSpec((None, 6, D_MODEL), lambda i: (i // per_b, 0, 0)),
            row((1, D_MODEL)), row((D_MODEL, LANES)), row((D_MODEL, LANES)), row((1, LANES)),
            row((tm, tm)), row((LANES, LANES)),
        ],
        out_specs=[
            pl.BlockSpec((tm, D_MODEL), lambda i: (i, 0)),
            pl.BlockSpec((tm, LANES), lambda i: (i, 0)),
            pl.BlockSpec((None, SUBLANES, LANES), lambda i: (i, 0, 0)),
        ],
        out_shape=[
            jax.ShapeDtypeStruct((TOKENS, D_MODEL), BF16),
            jax.ShapeDtypeStruct((TOKENS, LANES), F32),
            jax.ShapeDtypeStruct((TOKENS // tm, SUBLANES, LANES), F32),
        ],
        scratch_shapes=[pltpu.VMEM((SUBLANES, LANES), F32)],
        compiler_params=pltpu.CompilerParams(
            dimension_semantics=("arbitrary",), vmem_limit_bytes=VMEM_LIMIT),
        name="router",
    )(x, mod, g, wrh, wrl, br, stri, upper)


BLOCK_ROWS = 2 * TM_ROUTE + LANES
RANGE_PIECES = tuple(1 << b for b in range(TM_ROUTE.bit_length() - 1, 2, -1))
assert TM_SLOT <= 2 * RANGE_PIECES[0]


def _range_pieces(n, loc, slot, block_ref, slots_ref, sem, to_slots, visit):
    loc = pl.multiple_of(loc, SUBLANES)
    slot = pl.multiple_of(slot, SUBLANES)
    for piece in RANGE_PIECES:
        has = (n & piece) != 0
        src = block_ref.at[pl.ds(loc, piece)]
        dst = slots_ref.at[pl.ds(slot, piece)]
        if not to_slots:
            src, dst = dst, src

        @pl.when(has)
        def _():
            visit(pltpu.make_async_copy(src, dst, sem))

        step = jnp.where(has, piece, 0)
        loc = pl.multiple_of(loc + step, SUBLANES)
        slot = pl.multiple_of(slot + step, SUBLANES)


def _chunk_copies(chunk, n_ref, loc_ref, slot_ref, block_ref, slots_ref, sem, to_slots, visit):
    for e in range(N_EXPERTS):
        i = chunk * N_EXPERTS + e
        _range_pieces(n_ref[i], loc_ref[i], slot_ref[i], block_ref, slots_ref, sem, to_slots, visit)


def _start(cp):
    cp.start()


def _wait(cp):
    cp.wait()


def _choice_onehots(meta, width):
    col = lax.broadcasted_iota(jnp.int32, (meta.shape[0], width), 1)
    d1 = meta[:, 2:3].astype(jnp.int32)
    d2 = meta[:, 3:4].astype(jnp.int32)
    return col == d1, col == d2


def _zero_copies(padn_ref, pads_ref, used_ref, zero_ref, xs_ref, sem, visit):
    for e in range(N_EXPERTS):
        _range_pieces(padn_ref[e], 0, pads_ref[e], zero_ref, xs_ref, sem, True, visit)
    for t in range(2 * TOKENS // TM_SLOT, N_SLOTS // TM_SLOT):
        @pl.when(t * TM_SLOT >= used_ref[0])
        def _():
            visit(pltpu.make_async_copy(zero_ref, xs_ref.at[pl.ds(t * TM_SLOT, TM_SLOT)], sem))


def _dispatch_kernel(n_ref, loc_ref, slot_ref, padn_ref, pads_ref, used_ref, h_ref, meta_ref, xs_ref,
                     block_ref, zero_ref, sem, zsem):
    i = pl.program_id(0)
    last = pl.num_programs(0) - 1
    cur = i % 2
    zeros = (padn_ref, pads_ref, used_ref, zero_ref, xs_ref, zsem)

    @pl.when(i == 0)
    def _():
        zero_ref[...] = jnp.zeros_like(zero_ref)
        _zero_copies(*zeros, _start)

    a1, a2 = _choice_onehots(meta_ref[...], BLOCK_ROWS)
    sel = (a1 | a2).astype(BF16)
    block_ref[cur] = lax.dot_general(sel, h_ref[...], (((0,), (0,)), ((), ())),
                                     preferred_element_type=F32)
    tabs = (n_ref, loc_ref, slot_ref)
    _chunk_copies(i, *tabs, block_ref.at[cur], xs_ref, sem.at[cur], True, _start)

    @pl.when(i > 0)
    def _():
        _chunk_copies(i - 1, *tabs, block_ref.at[1 - cur], xs_ref, sem.at[1 - cur], True, _wait)

    @pl.when(i == last)
    def _():
        _chunk_copies(i, *tabs, block_ref.at[cur], xs_ref, sem.at[cur], True, _wait)
        _zero_copies(*zeros, _wait)


def _dispatch(tabs, pad_tabs, h, meta):
    tm = TM_ROUTE
    return pl.pallas_call(
        _dispatch_kernel,
        grid_spec=pltpu.PrefetchScalarGridSpec(
            num_scalar_prefetch=6,
            grid=(TOKENS // tm,),
            in_specs=[
                pl.BlockSpec((tm, D_MODEL), lambda i, *_: (i, 0)),
                pl.BlockSpec((tm, LANES), lambda i, *_: (i, 0)),
            ],
            out_specs=pl.BlockSpec(memory_space=pl.ANY),
            scratch_shapes=[pltpu.VMEM((2, BLOCK_ROWS, D_MODEL), F32),
                            pltpu.VMEM((TM_SLOT, D_MODEL), F32),
                            pltpu.SemaphoreType.DMA((2,)), pltpu.SemaphoreType.DMA],
        ),
        out_shape=jax.ShapeDtypeStruct((N_SLOTS, D_MODEL), F32),
        compiler_params=pltpu.CompilerParams(
            dimension_semantics=("arbitrary",), vmem_limit_bytes=VMEM_LIMIT),
        name="dispatch",
    )(*tabs, *pad_tabs, h, meta)


def _experts_kernel(te_ref, tv_ref, ts_ref, xs_ref, wg_ref, wu_ref, wd_ref, o_ref, h2_ref, acc_ref):
    del te_ref, ts_ref
    j = pl.program_id(1)
    last = pl.num_programs(1) - 1
    valid = tv_ref[pl.program_id(0)] != 0

    @pl.when(valid)
    def _():
        @pl.when(j == 0)
        def _():
            h2_ref[...] = xs_ref[...].astype(BF16)
            acc_ref[...] = jnp.zeros_like(acc_ref)

        hb = h2_ref[...]
        g = jnp.dot(hb, wg_ref[...], preferred_element_type=F32)
        u = jnp.dot(hb, wu_ref[...], preferred_element_type=F32)
        act = g * _sigmoid(g) * u
        acc_ref[...] += jnp.dot(act.astype(BF16), wd_ref[...], preferred_element_type=F32)

        @pl.when(j == last)
        def _():
            o_ref[...] = acc_ref[...]

    @pl.when(jnp.logical_not(valid) & (j == last))
    def _():
        o_ref[...] = jnp.zeros_like(o_ref)


def _experts(tile_e, tile_v, tile_src, xs, wg, wu, wd):
    tm, tf = TM_SLOT, TF_FFN
    return pl.pallas_call(
        _experts_kernel,
        grid_spec=pltpu.PrefetchScalarGridSpec(
            num_scalar_prefetch=3,
            grid=(N_SLOTS // tm, D_FF // tf),
            in_specs=[
                pl.BlockSpec((tm, D_MODEL), lambda i, j, te, tv, ts: (ts[i], 0)),
                pl.BlockSpec((None, D_MODEL, tf), lambda i, j, te, tv, ts: (te[i], 0, j)),
                pl.BlockSpec((None, D_MODEL, tf), lambda i, j, te, tv, ts: (te[i], 0, j)),
                pl.BlockSpec((None, tf, D_MODEL), lambda i, j, te, tv, ts: (te[i], j, 0)),
            ],
            out_specs=pl.BlockSpec((tm, D_MODEL), lambda i, j, te, tv, ts: (i, 0)),
            scratch_shapes=[pltpu.VMEM((tm, D_MODEL), BF16), pltpu.VMEM((tm, D_MODEL), F32)],
        ),
        out_shape=jax.ShapeDtypeStruct((N_SLOTS, D_MODEL), F32),
        compiler_params=pltpu.CompilerParams(
            dimension_semantics=("arbitrary", "arbitrary"), vmem_limit_bytes=VMEM_LIMIT),
        name="experts",
    )(tile_e, tile_v, tile_src, xs, wg, wu, wd)


def _combine_kernel(n_ref, loc_ref, slot_ref, x_ref, mod_ref, meta_ref, gf_ref, ys_ref, o_ref,
                    block_ref, sem):
    i = pl.program_id(0)
    cur = i % 2
    tabs = (n_ref, loc_ref, slot_ref)

    @pl.when(i == 0)
    def _():
        block_ref[...] = jnp.zeros_like(block_ref)
        _chunk_copies(0, *tabs, block_ref.at[0], ys_ref, sem.at[0], False, _start)

    @pl.when(i + 1 < pl.num_programs(0))
    def _():
        _chunk_copies(i + 1, *tabs, block_ref.at[1 - cur], ys_ref, sem.at[1 - cur], False, _start)

    _chunk_copies(i, *tabs, block_ref.at[cur], ys_ref, sem.at[cur], False, _wait)
    meta = meta_ref[...]
    a1, a2 = _choice_onehots(meta, BLOCK_ROWS)
    blk = block_ref[cur].astype(BF16)
    y = (meta[:, 4:5] * jnp.dot(a1.astype(BF16), blk, preferred_element_type=F32)
         + meta[:, 5:6] * jnp.dot(a2.astype(BF16), blk, preferred_element_type=F32))
    y = x_ref[...] + mod_ref[5:6, :] * y
    o_ref[...] = y * lax.rsqrt(jnp.mean(y * y, axis=-1, keepdims=True) + EPS) * gf_ref[...]


def _combine(tabs, x, mod, meta, g_final, ys):
    tm = TM_ROUTE
    per_b = SEQ // tm
    return pl.pallas_call(
        _combine_kernel,
        grid_spec=pltpu.PrefetchScalarGridSpec(
            num_scalar_prefetch=3,
            grid=(TOKENS // tm,),
            in_specs=[
                pl.BlockSpec((tm, D_MODEL), lambda i, *_: (i, 0)),
                pl.BlockSpec((None, 6, D_MODEL), lambda i, *_: (i // per_b, 0, 0)),
                pl.BlockSpec((tm, LANES), lambda i, *_: (i, 0)),
                pl.BlockSpec((1, D_MODEL), lambda i, *_: (0, 0)),
                pl.BlockSpec(memory_space=pl.ANY),
            ],
            out_specs=pl.BlockSpec((tm, D_MODEL), lambda i, *_: (i, 0)),
            scratch_shapes=[pltpu.VMEM((2, BLOCK_ROWS, D_MODEL), F32), pltpu.SemaphoreType.DMA((2,))],
        ),
        out_shape=jax.ShapeDtypeStruct((TOKENS, D_MODEL), F32),
        compiler_params=pltpu.CompilerParams(
            dimension_semantics=("arbitrary",), vmem_limit_bytes=VMEM_LIMIT),
        name="combine",
    )(*tabs, x, mod, meta, g_final, ys)


def _moe(x, mod, g, w_router, b_router, wg, wu, wd, g_final):
    wr = _pad_lanes(w_router)
    wrh = wr.astype(BF16)
    wrl = (wr - wrh.astype(F32)).astype(BF16)
    stri = (lax.broadcasted_iota(jnp.int32, (TM_ROUTE, TM_ROUTE), 1)
            < lax.broadcasted_iota(jnp.int32, (TM_ROUTE, TM_ROUTE), 0)).astype(BF16)
    upper = (lax.broadcasted_iota(jnp.int32, (LANES, LANES), 0)
             < lax.broadcasted_iota(jnp.int32, (LANES, LANES), 1)).astype(BF16)
    h, meta, tab = _router(x, mod, g, wrh, wrl, _pad_lanes(b_router[None]), stri, upper)
    tab = tab[:, :3, :N_EXPERTS].astype(jnp.int32)
    n_ce, before_ce, loc_ce = tab[:, 0], tab[:, 1], tab[:, 2]
    counts = before_ce[-1] + n_ce[-1]
    padded = (counts + TM_SLOT - 1) // TM_SLOT * TM_SLOT
    ends = jnp.cumsum(padded)
    slot_ce = (ends - padded)[None, :] + before_ce
    tabs = (n_ce.reshape(-1), loc_ce.reshape(-1), slot_ce.reshape(-1))
    pad_tabs = (padded - counts, ends - padded + counts, ends[-1:])
    tile = jnp.arange(N_SLOTS // TM_SLOT, dtype=jnp.int32)
    tile_e = jnp.minimum(jnp.sum(tile[:, None] * TM_SLOT >= ends[None, :], axis=1), N_EXPERTS - 1)
    tile_v = (tile * TM_SLOT < ends[-1]).astype(jnp.int32)
    tile_src = jnp.minimum(tile, ends[-1] // TM_SLOT - 1)
    xs = _dispatch(tabs, pad_tabs, h, meta)
    ys = _experts(tile_e.astype(jnp.int32), tile_v, tile_src, xs, wg, wu, wd)
    return _combine(tabs, x, mod, meta, g_final, ys)


def _block_diag(w):
    eye = jnp.eye(N_LRU_BLOCKS, dtype=w.dtype)
    return jnp.einsum('nde,nm->ndme', w, eye).reshape(D_LRU, D_LRU)


def _pad_lanes(a):
    return jnp.zeros(a.shape[:-1] + (LANES,), a.dtype).at[..., :a.shape[-1]].set(a)


def kernel(x, c, w_ada, b_ada, g_norm_mix, g_norm_ffn, w_in, w_conv_qk, b_conv_qk, b_gates,
           w_conv_lru, b_conv_lru, w_lru_a, b_lru_a, w_lru_x, b_lru_x, lru_lambda, g_mix_out, w_out,
           w_ff_gate, w_ff_up, w_ff_down, w_router, b_router, w_exp_gate, w_exp_up, w_exp_down, g_final):
    assert DEPTH == 2
    mods = _ada(c, w_ada, b_ada)
    tri = (lax.broadcasted_iota(jnp.int32, (L_MIX, L_MIX), 1)
           <= lax.broadcasted_iota(jnp.int32, (L_MIX, L_MIX), 0)).astype(BF16)
    xt = x.reshape(TOKENS, D_MODEL)
    for l in range(DEPTH):
        mod = mods[l]
        wi = w_in[l]
        n_q = 4 * D_MLSTM
        wcat = jnp.concatenate([wi[:, :n_q], wi[:, n_q + 2 * N_HEADS:],
                                _pad_lanes(wi[:, n_q:n_q + 2 * N_HEADS])], axis=1).astype(BF16)
        wax = jnp.concatenate([_block_diag(w_lru_a[l]), _block_diag(w_lru_x[l])], axis=1).astype(BF16)
        bax = jnp.concatenate([b_lru_a[l], b_lru_x[l]])[None]
        zq, zg, hl = _inproj_lru(xt, mod, g_norm_mix[l][None], wcat,
                                 w_conv_lru[l], b_conv_lru[l][None], wax, bax, lru_lambda[l][None],
                                 g_mix_out[l][None, D_MLSTM:])
        wo = w_out[l].astype(BF16)
        xt = _mlstm_outproj(zq, zg, w_conv_qk[l], b_conv_qk[l][None], _pad_lanes(b_gates[l][None]),
                            g_mix_out[l][None, :D_MLSTM], tri, hl, xt, mod, wo[:D_MLSTM], wo[D_MLSTM:])
        jj = l // 2
        if l % 2 == 0:
            xt = _ffn(xt, mod, g_norm_ffn[l][None], w_ff_gate[jj].astype(BF16),
                      w_ff_up[jj].astype(BF16), w_ff_down[jj].astype(BF16))
        else:
            xt = _moe(xt, mod, g_norm_ffn[l][None], w_router[jj], b_router[jj],
                      w_exp_gate[jj].astype(BF16), w_exp_up[jj].astype(BF16),
                      w_exp_down[jj].astype(BF16), g_final[None])
    return xt.reshape(BATCH, SEQ, D_MODEL)
```

```python
import functools

import jax
import jax.numpy as jnp
from jax import lax
from jax.experimental import pallas as pl
from jax.experimental.pallas import tpu as pltpu

F32 = jnp.float32
BF16 = jnp.bfloat16

D_MODEL = 1024
BATCH = 4
SEQ = 8192
TOKENS = BATCH * SEQ
DEPTH = 2
D_MLSTM = 512
N_HEADS = 4
DH = 128
D_LRU = 512
N_LRU_BLOCKS = 8
DB_LRU = 64
CONV_W = 4
LRU_C = 8.0
D_FF = 2816
N_EXPERTS = 8
EPS = 1e-6

LANES = 128
SUBLANES = 8
VMEM_LIMIT = 56 * 1024 * 1024

L_MIX = 256
PROJ_CHUNK = 256
TM_FFN = 256
TM_ROUTE = 256
TM_SLOT = 256
N_SLOTS = 2 * TOKENS + (TOKENS // TM_ROUTE) * N_EXPERTS * SUBLANES + N_EXPERTS * TM_SLOT


def _sigmoid(x):
    return 1.0 / (1.0 + jnp.exp(-x))


def _mod_norm(x, g, scale, shift):
    ms = jnp.mean(x * x, axis=-1, keepdims=True)
    return (x * lax.rsqrt(ms + EPS)) * g * (1.0 + scale) + shift


def _causal_conv(x, ext_ref, w, b):
    L = x.shape[0]
    ext_ref[SUBLANES:, :] = x
    acc = b + w[CONV_W - 1:CONV_W] * x
    for s in range(1, CONV_W):
        acc = acc + w[CONV_W - 1 - s:CONV_W - s] * ext_ref[SUBLANES - s:SUBLANES - s + L, :]
    ext_ref[:SUBLANES, :] = x[L - SUBLANES:, :]
    return acc


def _zero_tile(z):
    bits = lax.bitcast_convert_type(z[-SUBLANES:, -LANES:], jnp.uint32)
    return ((bits >> 16) >> 16).astype(F32)


def _after(v, zeros, lag=0):
    if len(zeros) <= lag:
        return v
    tile = v[:SUBLANES, :LANES]
    while len(zeros) > lag:
        tile = tile + zeros.pop(0)
    head = tile if v.shape[1] == LANES else jnp.concatenate([tile, v[:SUBLANES, LANES:]], axis=1)
    return jnp.concatenate([head, v[SUBLANES:]], axis=0)


def _ada_kernel(c_ref, w_ref, b_ref, o_ref):
    c = c_ref[...]
    cs = c * _sigmoid(c)
    o_ref[...] = jnp.dot(cs, w_ref[...], preferred_element_type=F32) + b_ref[...]


def _ada(c, w_ada, b_ada):
    tn = 1536
    c8 = jnp.zeros((SUBLANES, D_MODEL), F32).at[:BATCH].set(c)
    out = pl.pallas_call(
        _ada_kernel,
        grid=(DEPTH, 6 * D_MODEL // tn),
        in_specs=[
            pl.BlockSpec((SUBLANES, D_MODEL), lambda l, n: (0, 0)),
            pl.BlockSpec((None, D_MODEL, tn), lambda l, n: (l, 0, n)),
            pl.BlockSpec((None, 1, tn), lambda l, n: (l, 0, n)),
        ],
        out_specs=pl.BlockSpec((None, SUBLANES, tn), lambda l, n: (l, 0, n)),
        out_shape=jax.ShapeDtypeStruct((DEPTH, SUBLANES, 6 * D_MODEL), F32),
        compiler_params=pltpu.CompilerParams(
            dimension_semantics=("arbitrary", "arbitrary"), vmem_limit_bytes=VMEM_LIMIT),
        name="ada",
    )(c8, w_ada, b_ada.reshape(DEPTH, 1, 6 * D_MODEL))
    return out[:, :BATCH].reshape(DEPTH, BATCH, 6, D_MODEL)


def _inproj_lru_kernel(x_ref, mod_ref, g_ref, w_ref, wc_ref, bc_ref, wax_ref, bax_ref, lam_ref, gm_ref,
                       zq_ref, zg_ref, hl_ref, zl_ref, ext_ref, hc_ref):
    L = L_MIX
    TIE_LAG = 4
    j = pl.program_id(1)
    slot = j % 2

    @pl.when((pl.program_id(0) == 0) & (j == 0))
    def _():
        zl_ref[...] = jnp.zeros_like(zl_ref)
        ext_ref[...] = jnp.zeros_like(ext_ref)
        hc_ref[...] = jnp.zeros_like(hc_ref)

    keep = j != 1
    ext_ref[:SUBLANES, :] = jnp.where(keep, ext_ref[:SUBLANES, :], 0.0)
    hc_ref[...] = jnp.where(keep, hc_ref[...], 0.0)
    xc = _causal_conv(zl_ref[1 - slot, :, :D_LRU].astype(F32), ext_ref, wc_ref[...], bc_ref[...])

    mod = mod_ref[...]
    h = _mod_norm(x_ref[...], g_ref[...], mod[1:2], mod[0:1]).astype(BF16)
    n_q, n_l = 4 * D_MLSTM, 2 * D_LRU
    todo = list(range(0, n_q + n_l + LANES, PROJ_CHUNK))
    zeros = []

    def project(n):
        for lo in todo[:n]:
            hi = min(lo + PROJ_CHUNK, n_q + n_l + LANES)
            z = jnp.dot(h, w_ref[:, lo:hi], preferred_element_type=F32)
            zeros.append(_zero_tile(z))
            if hi <= n_q:
                zq_ref[:, lo:hi] = z.astype(BF16)
            elif hi <= n_q + n_l:
                zl_ref[slot, :, lo - n_q:hi - n_q] = z.astype(BF16)
            else:
                zg_ref[...] = z
        del todo[:n]

    project(2)
    xc = _after(xc, zeros, TIE_LAG)
    gates = jnp.dot(xc.astype(BF16), wax_ref[...], preferred_element_type=F32) + bax_ref[...]
    r = _sigmoid(gates[:, :D_LRU])
    i = _sigmoid(gates[:, D_LRU:])
    project(2)
    r = _after(r, zeros, TIE_LAG)
    lam = lam_ref[...]
    sp = jnp.maximum(-lam, 0.0) + jnp.log1p(jnp.exp(-jnp.abs(lam)))
    log_a = -LRU_C * r * sp
    a = jnp.exp(log_a)
    y2 = 2.0 * log_a
    series = -y2 * (1.0 + y2 * (0.5 + y2 * (1.0 / 6.0 + y2 * (1.0 / 24.0))))
    u = jnp.sqrt(jnp.where(y2 > -0.01, series, 1.0 - a * a)) * (i * xc)
    project(2)
    u = _after(u, zeros, TIE_LAG)

    r8 = lax.broadcasted_iota(jnp.int32, (L, D_LRU), 0) & (SUBLANES - 1)
    for s in (1, 2, 4):
        a_sh = pltpu.roll(a, s, 0)
        u_sh = pltpu.roll(u, s, 0)
        valid = r8 >= s
        u = jnp.where(valid, a * u_sh + u, u)
        a = jnp.where(valid, a * a_sh, a)
        project(2)
        u = _after(u, zeros, TIE_LAG)
    gl = zl_ref[1 - slot, :, D_LRU:].astype(F32)
    gelu = 0.5 * gl * (1.0 + jnp.tanh(0.7978845608028654 * (gl + 0.044715 * gl * gl * gl)))
    project(len(todo))
    hc = hc_ref[...]
    enter = []
    for gi in range(L // SUBLANES):
        enter.append(hc)
        last = gi * SUBLANES + SUBLANES - 1
        hc = u[last:last + 1] + a[last:last + 1] * hc
    hc_ref[...] = hc
    hseq = jnp.concatenate(
        [u[gi * SUBLANES:(gi + 1) * SUBLANES] + a[gi * SUBLANES:(gi + 1) * SUBLANES] * enter[gi]
         for gi in range(L // SUBLANES)], axis=0)
    hl = hseq * gelu
    hl = hl * lax.rsqrt(jnp.mean(hl * hl, axis=-1, keepdims=True) + EPS) * gm_ref[...]
    hl_ref[...] = hl.astype(BF16)


def _inproj_lru(x, mod, g, w, w_conv, b_conv, wax, bax, lam, g_mix):
    L = L_MIX
    nc = SEQ // L
    proj = lambda b, j: (b * nc + jnp.minimum(j, nc - 1), 0)
    mixed = lambda b, j: (b * nc + jnp.maximum(j - 1, 0), 0)
    const = lambda shape: pl.BlockSpec(shape, lambda b, j: (0, 0))
    return pl.pallas_call(
        _inproj_lru_kernel,
        grid=(BATCH, nc + 1),
        in_specs=[
            pl.BlockSpec((L, D_MODEL), proj),
            pl.BlockSpec((None, 6, D_MODEL), lambda b, j: (b, 0, 0)),
            const((1, D_MODEL)), const(w.shape),
            const((CONV_W, D_LRU)), const((1, D_LRU)), const((D_LRU, 2 * D_LRU)),
            const((1, 2 * D_LRU)), const((1, D_LRU)), const((1, D_LRU)),
        ],
        out_specs=[
            pl.BlockSpec((L, 4 * D_MLSTM), proj),
            pl.BlockSpec((L, LANES), proj),
            pl.BlockSpec((L, D_LRU), mixed),
        ],
        out_shape=[
            jax.ShapeDtypeStruct((TOKENS, 4 * D_MLSTM), BF16),
            jax.ShapeDtypeStruct((TOKENS, LANES), F32),
            jax.ShapeDtypeStruct((TOKENS, D_LRU), BF16),
        ],
        scratch_shapes=[
            pltpu.VMEM((2, L, 2 * D_LRU), BF16),
            pltpu.VMEM((SUBLANES + L, D_LRU), F32),
            pltpu.VMEM((1, D_LRU), F32),
        ],
        compiler_params=pltpu.CompilerParams(
            dimension_semantics=("arbitrary", "arbitrary"), vmem_limit_bytes=VMEM_LIMIT),
        name="inproj_rglru",
    )(x, mod, g, w, w_conv, b_conv, wax, bax, lam, g_mix)


def _mlstm_outproj_kernel(zq_ref, zg_ref, wc_ref, bc_ref, bg_ref, gm_ref, tri_ref,
                          hl_ref, x_ref, mod_ref, wm_ref, wl_ref, o_ref,
                          ct_ref, m_ref, ext_ref, hm_ref):
    L = L_MIX
    j = pl.program_id(1)
    slot = j % 2

    @pl.when((pl.program_id(0) == 0) & (j == 0))
    def _():
        hm_ref[...] = jnp.zeros_like(hm_ref)

    @pl.when(j == 0)
    def _():
        ct_ref[...] = jnp.zeros_like(ct_ref)
        m_ref[...] = jnp.zeros_like(m_ref)
        ext_ref[:SUBLANES, :] = jnp.zeros((SUBLANES, ext_ref.shape[1]), F32)

    todo = list(range(0, D_MODEL, PROJ_CHUNK))
    zeros = []

    def project(n):
        for lo in todo[:n]:
            cs = slice(lo, lo + PROJ_CHUNK)
            y = (jnp.dot(hm_ref[1 - slot], wm_ref[:, cs], preferred_element_type=F32)
                 + jnp.dot(hl_ref[...], wl_ref[:, cs], preferred_element_type=F32))
            zeros.append(_zero_tile(y))
            o_ref[:, cs] = x_ref[:, cs] + mod_ref[2:3, cs] * y
        del todo[:n]

    project(1)
    qk = _causal_conv(zq_ref[:, :2 * D_MLSTM].astype(F32), ext_ref, wc_ref[...], bc_ref[...])
    qk = _after(qk, zeros)
    project(1)
    qk = qk * _sigmoid(qk)
    q = _after(qk[:, :D_MLSTM], zeros).astype(BF16)
    k = (qk[:, D_MLSTM:] * (DH ** -0.5)).astype(BF16)
    project(1)

    g = zg_ref[...] + bg_ref[...]
    lf = jnp.minimum(g, 0.0) - jnp.log1p(jnp.exp(-jnp.abs(g)))
    tri = tri_ref[...]
    hi = lf.astype(BF16)
    r1 = lf - hi.astype(F32)
    mid = r1.astype(BF16)
    lo = (r1 - mid.astype(F32)).astype(BF16)
    bcum = (jnp.dot(tri, hi, preferred_element_type=F32)
            + jnp.dot(tri, mid, preferred_element_type=F32)
            + jnp.dot(tri, lo, preferred_element_type=F32))
    lane = lax.broadcasted_iota(jnp.int32, (L, LANES), 1)
    cols = _after(jnp.where(lane < N_HEADS, g, bcum), zeros)
    project(len(todo))
    rows = jnp.transpose(cols)

    t_idx = lax.broadcasted_iota(jnp.int32, (L, L), 0)
    s_idx = lax.broadcasted_iota(jnp.int32, (L, L), 1)
    causal = s_idx <= t_idx
    ones_col = (lax.broadcasted_iota(jnp.int32, (L, DH), 1) == 0).astype(BF16)
    gm = gm_ref[...]

    heads = range(N_HEADS)
    sls = [slice(h * DH, (h + 1) * DH) for h in heads]
    ic = [cols[:, h:h + 1] for h in heads]
    bc = [cols[:, N_HEADS + h:N_HEADS + h + 1] for h in heads]
    ir = [rows[h:h + 1, :] for h in heads]
    br = [rows[N_HEADS + h:N_HEADS + h + 1, :] for h in heads]
    b_last = [br[h][:, L - 1:L] for h in heads]
    m_prev = [m_ref[h:h + 1, 0:1] for h in heads]
    qh = [q[:, sls[h]] for h in heads]
    kh = [k[:, sls[h]] for h in heads]
    vaug = [jnp.concatenate([zq_ref[:, 2 * D_MLSTM + h * DH:2 * D_MLSTM + (h + 1) * DH], ones_col], axis=1)
            for h in heads]
    ct = [ct_ref[h] for h in heads]

    dm = [jnp.where(causal, bc[h] - br[h] + ir[h], -jnp.inf) for h in heads]
    dm[0] = _after(dm[0], zeros)
    inter_log = [bc[h] + m_prev[h] for h in heads]
    sc = [lax.dot_general(qh[h], kh[h], (((1,), (1,)), ((), ())), preferred_element_type=F32) for h in heads]
    inter = [jnp.dot(qh[h], ct[h].astype(BF16), preferred_element_type=F32) for h in heads]
    m_t = [jnp.maximum(inter_log[h], jnp.max(dm[h], axis=1, keepdims=True)) for h in heads]

    w_end = [b_last[h] - bc[h] + ic[h] for h in heads]
    m_loc = [jnp.max(w_end[h], axis=0, keepdims=True) for h in heads]
    m_new = [jnp.maximum(b_last[h] + m_prev[h], m_loc[h]) for h in heads]
    ev = [(jnp.exp(w_end[h] - m_loc[h]) * vaug[h].astype(F32)).astype(BF16) for h in heads]
    c_loc = [lax.dot_general(kh[h], ev[h], (((0,), (0,)), ((), ())), preferred_element_type=F32)
             for h in heads]

    p = [(sc[h] * jnp.exp(dm[h] - m_t[h])).astype(BF16) for h in heads]
    out = [jnp.dot(p[h], vaug[h], preferred_element_type=F32) + jnp.exp(inter_log[h] - m_t[h]) * inter[h]
           for h in heads]
    for h in heads:
        ct_ref[h] = (jnp.exp(b_last[h] + m_prev[h] - m_new[h]) * ct[h]
                     + jnp.exp(m_loc[h] - m_new[h]) * c_loc[h])
        m_ref[h:h + 1, :] = jnp.broadcast_to(m_new[h], (1, LANES))
    hh = [out[h][:, :DH] / jnp.maximum(jnp.abs(out[h][:, DH:DH + 1]), jnp.exp(-m_t[h])) for h in heads]
    hn = [hh[h] * lax.rsqrt(jnp.mean(hh[h] * hh[h], axis=-1, keepdims=True) + EPS) * gm[:, sls[h]]
          for h in heads]
    for h in heads:
        og = zq_ref[:, 3 * D_MLSTM + h * DH:3 * D_MLSTM + (h + 1) * DH].astype(F32)
        hm_ref[slot, :, sls[h]] = (hn[h] * _sigmoid(og)).astype(BF16)


def _mlstm_outproj(zq, zg, w_conv, b_conv, b_gates, g_mix, tri, hl, x, mod, wm, wl):
    L = L_MIX
    nc = SEQ // L
    mixed = lambda b, j: (b * nc + jnp.minimum(j, nc - 1), 0)
    projected = lambda b, j: (b * nc + jnp.maximum(j - 1, 0), 0)
    const = lambda shape: pl.BlockSpec(shape, lambda b, j: (0, 0))
    return pl.pallas_call(
        _mlstm_outproj_kernel,
        grid=(BATCH, nc + 1),
        in_specs=[
            pl.BlockSpec((L, 4 * D_MLSTM), mixed),
            pl.BlockSpec((L, LANES), mixed),
            const((CONV_W, 2 * D_MLSTM)), const((1, 2 * D_MLSTM)), const((1, LANES)), const((1, D_MLSTM)),
            const((L, L)),
            pl.BlockSpec((L, D_LRU), projected),
            pl.BlockSpec((L, D_MODEL), projected),
            pl.BlockSpec((None, 6, D_MODEL), lambda b, j: (b, 0, 0)),
            const((D_MLSTM, D_MODEL)), const((D_LRU, D_MODEL)),
        ],
        out_specs=pl.BlockSpec((L, D_MODEL), projected),
        out_shape=jax.ShapeDtypeStruct((TOKENS, D_MODEL), F32),
        scratch_shapes=[
            pltpu.VMEM((N_HEADS, DH, 2 * DH), F32),
            pltpu.VMEM((SUBLANES, LANES), F32),
            pltpu.VMEM((SUBLANES + L, 2 * D_MLSTM), F32),
            pltpu.VMEM((2, L, D_MLSTM), BF16),
        ],
        compiler_params=pltpu.CompilerParams(
            dimension_semantics=("arbitrary", "arbitrary"), vmem_limit_bytes=VMEM_LIMIT),
        name="mlstm_outproj",
    )(zq, zg, w_conv, b_conv, b_gates, g_mix, tri, hl, x, mod, wm, wl)


def _swiglu(hb, wg_ref, wu_ref, wd_ref):
    g = jnp.dot(hb, wg_ref[...], preferred_element_type=F32)
    u = jnp.dot(hb, wu_ref[...], preferred_element_type=F32)
    act = g * _sigmoid(g) * u
    return jnp.dot(act.astype(BF16), wd_ref[...], preferred_element_type=F32)


def _ffn_kernel(x_ref, mod_ref, g_ref, wg_ref, wu_ref, wd_ref, o_ref):
    x = x_ref[...]
    hb = _mod_norm(x, g_ref[...], mod_ref[4:5, :], mod_ref[3:4, :]).astype(BF16)
    o_ref[...] = x + mod_ref[5:6, :] * _swiglu(hb, wg_ref, wu_ref, wd_ref)


def _ffn(x, mod, g, wg, wu, wd):
    tm = TM_FFN
    per_b = SEQ // tm
    return pl.pallas_call(
        _ffn_kernel,
        grid=(TOKENS // tm,),
        in_specs=[
            pl.BlockSpec((tm, D_MODEL), lambda i: (i, 0)),
            pl.BlockSpec((None, 6, D_MODEL), lambda i: (i // per_b, 0, 0)),
            pl.BlockSpec((1, D_MODEL), lambda i: (0, 0)),
            pl.BlockSpec((D_MODEL, D_FF), lambda i: (0, 0)),
            pl.BlockSpec((D_MODEL, D_FF), lambda i: (0, 0)),
            pl.BlockSpec((D_FF, D_MODEL), lambda i: (0, 0)),
        ],
        out_specs=pl.BlockSpec((tm, D_MODEL), lambda i: (i, 0)),
        out_shape=jax.ShapeDtypeStruct((TOKENS, D_MODEL), F32),
        compiler_params=pltpu.CompilerParams(
            dimension_semantics=("arbitrary",), vmem_limit_bytes=VMEM_LIMIT),
        name="ffn",
    )(x, mod, g, wg, wu, wd)


def _router_kernel(x_ref, mod_ref, g_ref, wrh_ref, wrl_ref, br_ref, stri_ref, upper_ref,
                   h_ref, meta_ref, tab_ref, carry_ref):
    @pl.when(pl.program_id(0) == 0)
    def _():
        carry_ref[...] = jnp.zeros_like(carry_ref)

    tm = x_ref.shape[0]
    lane = lax.broadcasted_iota(jnp.int32, (tm, LANES), 1)
    h = _mod_norm(x_ref[...], g_ref[...], mod_ref[4:5, :], mod_ref[3:4, :])
    hb = h.astype(BF16)
    h_ref[...] = hb
    hlo = (h - hb.astype(F32)).astype(BF16)
    wrh = wrh_ref[...]
    logits = (jnp.dot(hb, wrh, preferred_element_type=F32)
              + jnp.dot(hlo, wrh, preferred_element_type=F32)
              + jnp.dot(hb, wrl_ref[...], preferred_element_type=F32)) + br_ref[...]
    logits = jnp.where(lane < N_EXPERTS, logits, -jnp.inf)
    m1 = jnp.max(logits, axis=-1, keepdims=True)
    i1 = jnp.min(jnp.where(logits == m1, lane, LANES), axis=-1, keepdims=True)
    rest = jnp.where(lane == i1, -jnp.inf, logits)
    m2 = jnp.max(rest, axis=-1, keepdims=True)
    i2 = jnp.min(jnp.where(rest == m2, lane, LANES), axis=-1, keepdims=True)
    e2 = jnp.exp(m2 - m1)
    p1 = 1.0 / (1.0 + e2)
    p2 = e2 * p1
    ind = jnp.where((lane == i1) | (lane == i2), 1.0, 0.0)
    n_e = jnp.floor((jnp.sum(ind, axis=0, keepdims=True) + (SUBLANES - 1)) * (1.0 / SUBLANES)) * SUBLANES
    lrank = jnp.dot(stri_ref[...], ind.astype(BF16), preferred_element_type=F32)
    loff = jnp.dot(jnp.broadcast_to(n_e, (SUBLANES, LANES)).astype(BF16), upper_ref[...],
                   preferred_element_type=F32)[0:1, :]
    row = lrank + loff
    d1 = jnp.sum(jnp.where(lane == i1, row, 0.0), axis=-1, keepdims=True)
    d2 = jnp.sum(jnp.where(lane == i2, row, 0.0), axis=-1, keepdims=True)
    vals = (i1.astype(F32), i2.astype(F32), d1, d2, p1, p2)
    meta = jnp.zeros((tm, LANES), F32)
    for n, v in enumerate(vals):
        meta = jnp.where(lane == n, v, meta)
    meta_ref[...] = meta
    carry = carry_ref[0:1, :]
    srow = lax.broadcasted_iota(jnp.int32, (SUBLANES, LANES), 0)
    tab_ref[...] = jnp.where(srow == 0, n_e, jnp.where(srow == 1, carry, jnp.where(srow == 2, loff, 0.0)))
    carry_ref[0:1, :] = carry + n_e


def _router(x, mod, g, wrh, wrl, br, stri, upper):
    tm = TM_ROUTE
    per_b = SEQ // tm
    row = lambda shape: pl.BlockSpec(shape, lambda i: (0, 0))
    return pl.pallas_call(
        _router_kernel,
        grid=(TOKENS // tm,),
        in_specs=[
            pl.BlockSpec((tm, D_MODEL), lambda i: (i, 0)),
            pl.BlockSpec((None, 6, D_MODEL), lambda i: (i // per_b, 0, 0)),
            row((1, D_MODEL)), row((D_MODEL, LANES)), row((D_MODEL, LANES)), row((1, LANES)),
            row((tm, tm)), row((LANES, LANES)),
        ],
        out_specs=[
            pl.BlockSpec((tm, D_MODEL), lambda i: (i, 0)),
            pl.BlockSpec((tm, LANES), lambda i: (i, 0)),
            pl.BlockSpec((None, SUBLANES, LANES), lambda i: (i, 0, 0)),
        ],
        out_shape=[
            jax.ShapeDtypeStruct((TOKENS, D_MODEL), BF16),
            jax.ShapeDtypeStruct((TOKENS, LANES), F32),
            jax.ShapeDtypeStruct((TOKENS // tm, SUBLANES, LANES), F32),
        ],
        scratch_shapes=[pltpu.VMEM((SUBLANES, LANES), F32)],
        compiler_params=pltpu.CompilerParams(
            dimension_semantics=("arbitrary",), vmem_limit_bytes=VMEM_LIMIT),
        name="router",
    )(x, mod, g, wrh, wrl, br, stri, upper)


BLOCK_ROWS = 2 * TM_ROUTE + LANES
RANGE_PIECES = tuple(1 << b for b in range(TM_ROUTE.bit_length() - 1, 2, -1))
assert TM_SLOT <= 2 * RANGE_PIECES[0]


def _range_pieces(n, loc, slot, block_ref, slots_ref, sem, to_slots, visit):
    loc = pl.multiple_of(loc, SUBLANES)
    slot = pl.multiple_of(slot, SUBLANES)
    for piece in RANGE_PIECES:
        has = (n & piece) != 0
        src = block_ref.at[pl.ds(loc, piece)]
        dst = slots_ref.at[pl.ds(slot, piece)]
        if not to_slots:
            src, dst = dst, src

        @pl.when(has)
        def _():
            visit(pltpu.make_async_copy(src, dst, sem))

        step = jnp.where(has, piece, 0)
        loc = pl.multiple_of(loc + step, SUBLANES)
        slot = pl.multiple_of(slot + step, SUBLANES)


def _chunk_copies(chunk, n_ref, loc_ref, slot_ref, block_ref, slots_ref, sem, to_slots, visit):
    for e in range(N_EXPERTS):
        i = chunk * N_EXPERTS + e
        _range_pieces(n_ref[i], loc_ref[i], slot_ref[i], block_ref, slots_ref, sem, to_slots, visit)


def _start(cp):
    cp.start()


def _wait(cp):
    cp.wait()


def _choice_onehots(meta, width):
    col = lax.broadcasted_iota(jnp.int32, (meta.shape[0], width), 1)
    d1 = meta[:, 2:3].astype(jnp.int32)
    d2 = meta[:, 3:4].astype(jnp.int32)
    return col == d1, col == d2


def _zero_copies(padn_ref, pads_ref, used_ref, zero_ref, xs_ref, sem, visit):
    for e in range(N_EXPERTS):
        _range_pieces(padn_ref[e], 0, pads_ref[e], zero_ref, xs_ref, sem, True, visit)
    for t in range(2 * TOKENS // TM_SLOT, N_SLOTS // TM_SLOT):
        @pl.when(t * TM_SLOT >= used_ref[0])
        def _():
            visit(pltpu.make_async_copy(zero_ref, xs_ref.at[pl.ds(t * TM_SLOT, TM_SLOT)], sem))


def _dispatch_kernel(n_ref, loc_ref, slot_ref, padn_ref, pads_ref, used_ref, h_ref, meta_ref, xs_ref,
                     block_ref, zero_ref, sem, zsem):
    i = pl.program_id(0)
    last = pl.num_programs(0) - 1
    cur = i % 2
    zeros = (padn_ref, pads_ref, used_ref, zero_ref, xs_ref, zsem)

    @pl.when(i == 0)
    def _():
        zero_ref[...] = jnp.zeros_like(zero_ref)
        _zero_copies(*zeros, _start)

    a1, a2 = _choice_onehots(meta_ref[...], BLOCK_ROWS)
    sel = (a1 | a2).astype(BF16)
    block_ref[cur] = lax.dot_general(sel, h_ref[...], (((0,), (0,)), ((), ())),
                                     preferred_element_type=F32)
    tabs = (n_ref, loc_ref, slot_ref)
    _chunk_copies(i, *tabs, block_ref.at[cur], xs_ref, sem.at[cur], True, _start)

    @pl.when(i > 0)
    def _():
        _chunk_copies(i - 1, *tabs, block_ref.at[1 - cur], xs_ref, sem.at[1 - cur], True, _wait)

    @pl.when(i == last)
    def _():
        _chunk_copies(i, *tabs, block_ref.at[cur], xs_ref, sem.at[cur], True, _wait)
        _zero_copies(*zeros, _wait)


def _dispatch(tabs, pad_tabs, h, meta):
    tm = TM_ROUTE
    return pl.pallas_call(
        _dispatch_kernel,
        grid_spec=pltpu.PrefetchScalarGridSpec(
            num_scalar_prefetch=6,
            grid=(TOKENS // tm,),
            in_specs=[
                pl.BlockSpec((tm, D_MODEL), lambda i, *_: (i, 0)),
                pl.BlockSpec((tm, LANES), lambda i, *_: (i, 0)),
            ],
            out_specs=pl.BlockSpec(memory_space=pl.ANY),
            scratch_shapes=[pltpu.VMEM((2, BLOCK_ROWS, D_MODEL), F32),
                            pltpu.VMEM((TM_SLOT, D_MODEL), F32),
                            pltpu.SemaphoreType.DMA((2,)), pltpu.SemaphoreType.DMA],
        ),
        out_shape=jax.ShapeDtypeStruct((N_SLOTS, D_MODEL), F32),
        compiler_params=pltpu.CompilerParams(
            dimension_semantics=("arbitrary",), vmem_limit_bytes=VMEM_LIMIT),
        name="dispatch",
    )(*tabs, *pad_tabs, h, meta)


def _experts_kernel(te_ref, tv_ref, ts_ref, xs_ref, wg_ref, wu_ref, wd_ref, o_ref):
    del te_ref, ts_ref
    valid = tv_ref[pl.program_id(0)] != 0

    @pl.when(valid)
    def _():
        o_ref[...] = _swiglu(xs_ref[...].astype(BF16), wg_ref, wu_ref, wd_ref)

    @pl.when(jnp.logical_not(valid))
    def _():
        o_ref[...] = jnp.zeros_like(o_ref)


def _experts(tile_e, tile_v, tile_src, xs, wg, wu, wd):
    tm = TM_SLOT
    return pl.pallas_call(
        _experts_kernel,
        grid_spec=pltpu.PrefetchScalarGridSpec(
            num_scalar_prefetch=3,
            grid=(N_SLOTS // tm,),
            in_specs=[
                pl.BlockSpec((tm, D_MODEL), lambda i, te, tv, ts: (ts[i], 0)),
                pl.BlockSpec((None, D_MODEL, D_FF), lambda i, te, tv, ts: (te[i], 0, 0)),
                pl.BlockSpec((None, D_MODEL, D_FF), lambda i, te, tv, ts: (te[i], 0, 0)),
                pl.BlockSpec((None, D_FF, D_MODEL), lambda i, te, tv, ts: (te[i], 0, 0)),
            ],
            out_specs=pl.BlockSpec((tm, D_MODEL), lambda i, te, tv, ts: (i, 0)),
        ),
        out_shape=jax.ShapeDtypeStruct((N_SLOTS, D_MODEL), F32),
        compiler_params=pltpu.CompilerParams(
            dimension_semantics=("arbitrary",), vmem_limit_bytes=VMEM_LIMIT),
        name="experts",
    )(tile_e, tile_v, tile_src, xs, wg, wu, wd)


def _combine_kernel(n_ref, loc_ref, slot_ref, x_ref, mod_ref, meta_ref, gf_ref, ys_ref, o_ref,
                    block_ref, sem):
    i = pl.program_id(0)
    cur = i % 2
    tabs = (n_ref, loc_ref, slot_ref)

    @pl.when(i == 0)
    def _():
        block_ref[...] = jnp.zeros_like(block_ref)
        _chunk_copies(0, *tabs, block_ref.at[0], ys_ref, sem.at[0], False, _start)

    @pl.when(i + 1 < pl.num_programs(0))
    def _():
        _chunk_copies(i + 1, *tabs, block_ref.at[1 - cur], ys_ref, sem.at[1 - cur], False, _start)

    _chunk_copies(i, *tabs, block_ref.at[cur], ys_ref, sem.at[cur], False, _wait)
    meta = meta_ref[...]
    a1, a2 = _choice_onehots(meta, BLOCK_ROWS)
    blk = block_ref[cur].astype(BF16)
    y = (meta[:, 4:5] * jnp.dot(a1.astype(BF16), blk, preferred_element_type=F32)
         + meta[:, 5:6] * jnp.dot(a2.astype(BF16), blk, preferred_element_type=F32))
    y = x_ref[...] + mod_ref[5:6, :] * y
    o_ref[...] = y * lax.rsqrt(jnp.mean(y * y, axis=-1, keepdims=True) + EPS) * gf_ref[...]


def _combine(tabs, x, mod, meta, g_final, ys):
    tm = TM_ROUTE
    per_b = SEQ // tm
    return pl.pallas_call(
        _combine_kernel,
        grid_spec=pltpu.PrefetchScalarGridSpec(
            num_scalar_prefetch=3,
            grid=(TOKENS // tm,),
            in_specs=[
                pl.BlockSpec((tm, D_MODEL), lambda i, *_: (i, 0)),
                pl.BlockSpec((None, 6, D_MODEL), lambda i, *_: (i // per_b, 0, 0)),
                pl.BlockSpec((tm, LANES), lambda i, *_: (i, 0)),
                pl.BlockSpec((1, D_MODEL), lambda i, *_: (0, 0)),
                pl.BlockSpec(memory_space=pl.ANY),
            ],
            out_specs=pl.BlockSpec((tm, D_MODEL), lambda i, *_: (i, 0)),
            scratch_shapes=[pltpu.VMEM((2, BLOCK_ROWS, D_MODEL), F32), pltpu.SemaphoreType.DMA((2,))],
        ),
        out_shape=jax.ShapeDtypeStruct((TOKENS, D_MODEL), F32),
        compiler_params=pltpu.CompilerParams(
            dimension_semantics=("arbitrary",), vmem_limit_bytes=VMEM_LIMIT),
        name="combine",
    )(*tabs, x, mod, meta, g_final, ys)


def _moe(x, mod, g, w_router, b_router, wg, wu, wd, g_final):
    wr = _pad_lanes(w_router)
    wrh = wr.astype(BF16)
    wrl = (wr - wrh.astype(F32)).astype(BF16)
    stri = (lax.broadcasted_iota(jnp.int32, (TM_ROUTE, TM_ROUTE), 1)
            < lax.broadcasted_iota(jnp.int32, (TM_ROUTE, TM_ROUTE), 0)).astype(BF16)
    upper = (lax.broadcasted_iota(jnp.int32, (LANES, LANES), 0)
             < lax.broadcasted_iota(jnp.int32, (LANES, LANES), 1)).astype(BF16)
    h, meta, tab = _router(x, mod, g, wrh, wrl, _pad_lanes(b_router[None]), stri, upper)
    tab = tab[:, :3, :N_EXPERTS].astype(jnp.int32)
    n_ce, before_ce, loc_ce = tab[:, 0], tab[:, 1], tab[:, 2]
    counts = before_ce[-1] + n_ce[-1]
    padded = (counts + TM_SLOT - 1) // TM_SLOT * TM_SLOT
    ends = jnp.cumsum(padded)
    slot_ce = (ends - padded)[None, :] + before_ce
    tabs = (n_ce.reshape(-1), loc_ce.reshape(-1), slot_ce.reshape(-1))
    pad_tabs = (padded - counts, ends - padded + counts, ends[-1:])
    tile = jnp.arange(N_SLOTS // TM_SLOT, dtype=jnp.int32)
    tile_e = jnp.minimum(jnp.sum(tile[:, None] * TM_SLOT >= ends[None, :], axis=1), N_EXPERTS - 1)
    tile_v = (tile * TM_SLOT < ends[-1]).astype(jnp.int32)
    tile_src = jnp.minimum(tile, ends[-1] // TM_SLOT - 1)
    xs = _dispatch(tabs, pad_tabs, h, meta)
    ys = _experts(tile_e.astype(jnp.int32), tile_v, tile_src, xs, wg, wu, wd)
    return _combine(tabs, x, mod, meta, g_final, ys)


def _block_diag(w):
    eye = jnp.eye(N_LRU_BLOCKS, dtype=w.dtype)
    return jnp.einsum('nde,nm->ndme', w, eye).reshape(D_LRU, D_LRU)


def _pad_lanes(a):
    return jnp.zeros(a.shape[:-1] + (LANES,), a.dtype).at[..., :a.shape[-1]].set(a)


def kernel(x, c, w_ada, b_ada, g_norm_mix, g_norm_ffn, w_in, w_conv_qk, b_conv_qk, b_gates,
           w_conv_lru, b_conv_lru, w_lru_a, b_lru_a, w_lru_x, b_lru_x, lru_lambda, g_mix_out, w_out,
           w_ff_gate, w_ff_up, w_ff_down, w_router, b_router, w_exp_gate, w_exp_up, w_exp_down, g_final):
    assert DEPTH == 2
    mods = _ada(c, w_ada, b_ada)
    tri = (lax.broadcasted_iota(jnp.int32, (L_MIX, L_MIX), 1)
           <= lax.broadcasted_iota(jnp.int32, (L_MIX, L_MIX), 0)).astype(BF16)
    xt = x.reshape(TOKENS, D_MODEL)
    for l in range(DEPTH):
        mod = mods[l]
        wi = w_in[l]
        n_q = 4 * D_MLSTM
        wcat = jnp.concatenate([wi[:, :n_q], wi[:, n_q + 2 * N_HEADS:],
                                _pad_lanes(wi[:, n_q:n_q + 2 * N_HEADS])], axis=1).astype(BF16)
        wax = jnp.concatenate([_block_diag(w_lru_a[l]), _block_diag(w_lru_x[l])], axis=1).astype(BF16)
        bax = jnp.concatenate([b_lru_a[l], b_lru_x[l]])[None]
        zq, zg, hl = _inproj_lru(xt, mod, g_norm_mix[l][None], wcat,
                                 w_conv_lru[l], b_conv_lru[l][None], wax, bax, lru_lambda[l][None],
                                 g_mix_out[l][None, D_MLSTM:])
        wo = w_out[l].astype(BF16)
        xt = _mlstm_outproj(zq, zg, w_conv_qk[l], b_conv_qk[l][None], _pad_lanes(b_gates[l][None]),
                            g_mix_out[l][None, :D_MLSTM], tri, hl, xt, mod, wo[:D_MLSTM], wo[D_MLSTM:])
        jj = l // 2
        if l % 2 == 0:
            xt = _ffn(xt, mod, g_norm_ffn[l][None], w_ff_gate[jj].astype(BF16),
                      w_ff_up[jj].astype(BF16), w_ff_down[jj].astype(BF16))
        else:
            xt = _moe(xt, mod, g_norm_ffn[l][None], w_router[jj], b_router[jj],
                      w_exp_gate[jj].astype(BF16), w_exp_up[jj].astype(BF16),
                      w_exp_down[jj].astype(BF16), g_final[None])
    return xt.reshape(BATCH, SEQ, D_MODEL)
```

```python
import functools

import jax
import jax.numpy as jnp
from jax import lax
from jax.experimental import pallas as pl
from jax.experimental.pallas import tpu as pltpu

F32 = jnp.float32
BF16 = jnp.bfloat16

D_MODEL = 1024
BATCH = 4
SEQ = 8192
TOKENS = BATCH * SEQ
DEPTH = 2
D_MLSTM = 512
N_HEADS = 4
DH = 128
D_LRU = 512
N_LRU_BLOCKS = 8
DB_LRU = 64
CONV_W = 4
LRU_C = 8.0
D_FF = 2816
N_EXPERTS = 8
EPS = 1e-6

LANES = 128
SUBLANES = 8
VMEM_LIMIT = 56 * 1024 * 1024

L_MIX = 256
PROJ_CHUNK = 256
TM_FFN = 256
TM_ROUTE = 256
TM_SLOT = 256
N_SLOTS = 2 * TOKENS + (TOKENS // TM_ROUTE) * N_EXPERTS * SUBLANES + N_EXPERTS * TM_SLOT


def _sigmoid(x):
    return 1.0 / (1.0 + jnp.exp(-x))


def _mod_norm(x, g, scale, shift):
    ms = jnp.mean(x * x, axis=-1, keepdims=True)
    return (x * lax.rsqrt(ms + EPS)) * g * (1.0 + scale) + shift


def _causal_conv(x, ext_ref, w, b):
    L = x.shape[0]
    ext_ref[SUBLANES:, :] = x
    acc = b + w[CONV_W - 1:CONV_W] * x
    for s in range(1, CONV_W):
        acc = acc + w[CONV_W - 1 - s:CONV_W - s] * ext_ref[SUBLANES - s:SUBLANES - s + L, :]
    ext_ref[:SUBLANES, :] = x[L - SUBLANES:, :]
    return acc


def _zero_tile(z):
    bits = lax.bitcast_convert_type(z[-SUBLANES:, -LANES:], jnp.uint32)
    return ((bits >> 16) >> 16).astype(F32)


def _after(v, zeros, lag=0):
    if len(zeros) <= lag:
        return v
    tile = v[:SUBLANES, :LANES]
    while len(zeros) > lag:
        tile = tile + zeros.pop(0)
    head = tile if v.shape[1] == LANES else jnp.concatenate([tile, v[:SUBLANES, LANES:]], axis=1)
    return jnp.concatenate([head, v[SUBLANES:]], axis=0)


def _ada_kernel(c_ref, w_ref, b_ref, o_ref):
    c = c_ref[...]
    cs = c * _sigmoid(c)
    o_ref[...] = jnp.dot(cs, w_ref[...], preferred_element_type=F32) + b_ref[...]


def _ada(c, w_ada, b_ada):
    tn = 1536
    c8 = jnp.zeros((SUBLANES, D_MODEL), F32).at[:BATCH].set(c)
    out = pl.pallas_call(
        _ada_kernel,
        grid=(DEPTH, 6 * D_MODEL // tn),
        in_specs=[
            pl.BlockSpec((SUBLANES, D_MODEL), lambda l, n: (0, 0)),
            pl.BlockSpec((None, D_MODEL, tn), lambda l, n: (l, 0, n)),
            pl.BlockSpec((None, 1, tn), lambda l, n: (l, 0, n)),
        ],
        out_specs=pl.BlockSpec((None, SUBLANES, tn), lambda l, n: (l, 0, n)),
        out_shape=jax.ShapeDtypeStruct((DEPTH, SUBLANES, 6 * D_MODEL), F32),
        compiler_params=pltpu.CompilerParams(
            dimension_semantics=("arbitrary", "arbitrary"), vmem_limit_bytes=VMEM_LIMIT),
        name="ada",
    )(c8, w_ada, b_ada.reshape(DEPTH, 1, 6 * D_MODEL))
    return out[:, :BATCH].reshape(DEPTH, BATCH, 6, D_MODEL)


def _inproj_lru_kernel(x_ref, mod_ref, g_ref, w_ref, wc_ref, bc_ref, wax_ref, bax_ref, lam_ref, gm_ref,
                       zq_ref, zg_ref, hl_ref, zl_ref, ext_ref, hc_ref):
    L = L_MIX
    TIE_LAG = 4
    j = pl.program_id(1)
    slot = j % 2

    @pl.when((pl.program_id(0) == 0) & (j == 0))
    def _():
        zl_ref[...] = jnp.zeros_like(zl_ref)
        ext_ref[...] = jnp.zeros_like(ext_ref)
        hc_ref[...] = jnp.zeros_like(hc_ref)

    keep = j != 1
    ext_ref[:SUBLANES, :] = jnp.where(keep, ext_ref[:SUBLANES, :], 0.0)
    hc_ref[...] = jnp.where(keep, hc_ref[...], 0.0)
    xc = _causal_conv(zl_ref[1 - slot, :, :D_LRU].astype(F32), ext_ref, wc_ref[...], bc_ref[...])

    mod = mod_ref[...]
    h = _mod_norm(x_ref[...], g_ref[...], mod[1:2], mod[0:1]).astype(BF16)
    n_q, n_l = 4 * D_MLSTM, 2 * D_LRU
    todo = list(range(0, n_q + n_l + LANES, PROJ_CHUNK))
    zeros = []

    def project(n):
        for lo in todo[:n]:
            hi = min(lo + PROJ_CHUNK, n_q + n_l + LANES)
            z = jnp.dot(h, w_ref[:, lo:hi], preferred_element_type=F32)
            zeros.append(_zero_tile(z))
            if hi <= n_q:
                zq_ref[:, lo:hi] = z.astype(BF16)
            elif hi <= n_q + n_l:
                zl_ref[slot, :, lo - n_q:hi - n_q] = z.astype(BF16)
            else:
                zg_ref[...] = z
        del todo[:n]

    project(2)
    xc = _after(xc, zeros, TIE_LAG)
    gates = jnp.dot(xc.astype(BF16), wax_ref[...], preferred_element_type=F32) + bax_ref[...]
    r = _sigmoid(gates[:, :D_LRU])
    i = _sigmoid(gates[:, D_LRU:])
    project(2)
    r = _after(r, zeros, TIE_LAG)
    lam = lam_ref[...]
    sp = jnp.maximum(-lam, 0.0) + jnp.log1p(jnp.exp(-jnp.abs(lam)))
    log_a = -LRU_C * r * sp
    a = jnp.exp(log_a)
    y2 = 2.0 * log_a
    series = -y2 * (1.0 + y2 * (0.5 + y2 * (1.0 / 6.0 + y2 * (1.0 / 24.0))))
    u = jnp.sqrt(jnp.where(y2 > -0.01, series, 1.0 - a * a)) * (i * xc)
    project(2)
    u = _after(u, zeros, TIE_LAG)

    r8 = lax.broadcasted_iota(jnp.int32, (L, D_LRU), 0) & (SUBLANES - 1)
    for s in (1, 2, 4):
        a_sh = pltpu.roll(a, s, 0)
        u_sh = pltpu.roll(u, s, 0)
        valid = r8 >= s
        u = jnp.where(valid, a * u_sh + u, u)
        a = jnp.where(valid, a * a_sh, a)
        project(2)
        u = _after(u, zeros, TIE_LAG)
    gl = zl_ref[1 - slot, :, D_LRU:].astype(F32)
    gelu = 0.5 * gl * (1.0 + jnp.tanh(0.7978845608028654 * (gl + 0.044715 * gl * gl * gl)))
    project(len(todo))
    hc = hc_ref[...]
    enter = []
    for gi in range(L // SUBLANES):
        enter.append(hc)
        last = gi * SUBLANES + SUBLANES - 1
        hc = u[last:last + 1] + a[last:last + 1] * hc
    hc_ref[...] = hc
    hseq = jnp.concatenate(
        [u[gi * SUBLANES:(gi + 1) * SUBLANES] + a[gi * SUBLANES:(gi + 1) * SUBLANES] * enter[gi]
         for gi in range(L // SUBLANES)], axis=0)
    hl = hseq * gelu
    hl = hl * lax.rsqrt(jnp.mean(hl * hl, axis=-1, keepdims=True) + EPS) * gm_ref[...]
    hl_ref[...] = hl.astype(BF16)


def _inproj_lru(x, mod, g, w, w_conv, b_conv, wax, bax, lam, g_mix):
    L = L_MIX
    nc = SEQ // L
    proj = lambda b, j: (b * nc + jnp.minimum(j, nc - 1), 0)
    mixed = lambda b, j: (b * nc + jnp.maximum(j - 1, 0), 0)
    const = lambda shape: pl.BlockSpec(shape, lambda b, j: (0, 0))
    return pl.pallas_call(
        _inproj_lru_kernel,
        grid=(BATCH, nc + 1),
        in_specs=[
            pl.BlockSpec((L, D_MODEL), proj),
            pl.BlockSpec((None, 6, D_MODEL), lambda b, j: (b, 0, 0)),
            const((1, D_MODEL)), const(w.shape),
            const((CONV_W, D_LRU)), const((1, D_LRU)), const((D_LRU, 2 * D_LRU)),
            const((1, 2 * D_LRU)), const((1, D_LRU)), const((1, D_LRU)),
        ],
        out_specs=[
            pl.BlockSpec((L, 4 * D_MLSTM), proj),
            pl.BlockSpec((L, LANES), proj),
            pl.BlockSpec((L, D_LRU), mixed),
        ],
        out_shape=[
            jax.ShapeDtypeStruct((TOKENS, 4 * D_MLSTM), BF16),
            jax.ShapeDtypeStruct((TOKENS, LANES), F32),
            jax.ShapeDtypeStruct((TOKENS, D_LRU), BF16),
        ],
        scratch_shapes=[
            pltpu.VMEM((2, L, 2 * D_LRU), BF16),
            pltpu.VMEM((SUBLANES + L, D_LRU), F32),
            pltpu.VMEM((1, D_LRU), F32),
        ],
        compiler_params=pltpu.CompilerParams(
            dimension_semantics=("arbitrary", "arbitrary"), vmem_limit_bytes=VMEM_LIMIT),
        name="inproj_rglru",
    )(x, mod, g, w, w_conv, b_conv, wax, bax, lam, g_mix)


def _mlstm_outproj_kernel(zq_ref, zg_ref, wc_ref, bc_ref, bg_ref, gm_ref, tri_ref,
                          hl_ref, x_ref, mod_ref, wm_ref, wl_ref, o_ref,
                          ct_ref, m_ref, ext_ref, hm_ref):
    L = L_MIX
    j = pl.program_id(1)
    slot = j % 2

    @pl.when((pl.program_id(0) == 0) & (j == 0))
    def _():
        hm_ref[...] = jnp.zeros_like(hm_ref)

    @pl.when(j == 0)
    def _():
        ct_ref[...] = jnp.zeros_like(ct_ref)
        m_ref[...] = jnp.zeros_like(m_ref)
        ext_ref[:SUBLANES, :] = jnp.zeros((SUBLANES, ext_ref.shape[1]), F32)

    todo = list(range(0, D_MODEL, PROJ_CHUNK))
    zeros = []

    def project(n):
        for lo in todo[:n]:
            cs = slice(lo, lo + PROJ_CHUNK)
            y = (jnp.dot(hm_ref[1 - slot], wm_ref[:, cs], preferred_element_type=F32)
                 + jnp.dot(hl_ref[...], wl_ref[:, cs], preferred_element_type=F32))
            zeros.append(_zero_tile(y))
            o_ref[:, cs] = x_ref[:, cs] + mod_ref[2:3, cs] * y
        del todo[:n]

    project(1)
    qk = _causal_conv(zq_ref[:, :2 * D_MLSTM].astype(F32), ext_ref, wc_ref[...], bc_ref[...])
    qk = _after(qk, zeros)
    project(1)
    qk = qk * _sigmoid(qk)
    q = _after(qk[:, :D_MLSTM], zeros).astype(BF16)
    k = (qk[:, D_MLSTM:] * (DH ** -0.5)).astype(BF16)
    project(1)

    g = zg_ref[...] + bg_ref[...]
    lf = jnp.minimum(g, 0.0) - jnp.log1p(jnp.exp(-jnp.abs(g)))
    tri = tri_ref[...]
    hi = lf.astype(BF16)
    r1 = lf - hi.astype(F32)
    mid = r1.astype(BF16)
    lo = (r1 - mid.astype(F32)).astype(BF16)
    bcum = (jnp.dot(tri, hi, preferred_element_type=F32)
            + jnp.dot(tri, mid, preferred_element_type=F32)
            + jnp.dot(tri, lo, preferred_element_type=F32))
    lane = lax.broadcasted_iota(jnp.int32, (L, LANES), 1)
    cols = _after(jnp.where(lane < N_HEADS, g, bcum), zeros)
    project(len(todo))
    rows = jnp.transpose(cols)

    t_idx = lax.broadcasted_iota(jnp.int32, (L, L), 0)
    s_idx = lax.broadcasted_iota(jnp.int32, (L, L), 1)
    causal = s_idx <= t_idx
    ones_col = (lax.broadcasted_iota(jnp.int32, (L, DH), 1) == 0).astype(BF16)
    gm = gm_ref[...]

    heads = range(N_HEADS)
    sls = [slice(h * DH, (h + 1) * DH) for h in heads]
    ic = [cols[:, h:h + 1] for h in heads]
    bc = [cols[:, N_HEADS + h:N_HEADS + h + 1] for h in heads]
    ir = [rows[h:h + 1, :] for h in heads]
    br = [rows[N_HEADS + h:N_HEADS + h + 1, :] for h in heads]
    b_last = [br[h][:, L - 1:L] for h in heads]
    m_prev = [m_ref[h:h + 1, 0:1] for h in heads]
    qh = [q[:, sls[h]] for h in heads]
    kh = [k[:, sls[h]] for h in heads]
    vaug = [jnp.concatenate([zq_ref[:, 2 * D_MLSTM + h * DH:2 * D_MLSTM + (h + 1) * DH], ones_col], axis=1)
            for h in heads]
    ct = [ct_ref[h] for h in heads]

    dm = [jnp.where(causal, bc[h] - br[h] + ir[h], -jnp.inf) for h in heads]
    dm[0] = _after(dm[0], zeros)
    inter_log = [bc[h] + m_prev[h] for h in heads]
    sc = [lax.dot_general(qh[h], kh[h], (((1,), (1,)), ((), ())), preferred_element_type=F32) for h in heads]
    inter = [jnp.dot(qh[h], ct[h].astype(BF16), preferred_element_type=F32) for h in heads]
    m_t = [jnp.maximum(inter_log[h], jnp.max(dm[h], axis=1, keepdims=True)) for h in heads]

    w_end = [b_last[h] - bc[h] + ic[h] for h in heads]
    m_loc = [jnp.max(w_end[h], axis=0, keepdims=True) for h in heads]
    m_new = [jnp.maximum(b_last[h] + m_prev[h], m_loc[h]) for h in heads]
    ev = [(jnp.exp(w_end[h] - m_loc[h]) * vaug[h].astype(F32)).astype(BF16) for h in heads]
    c_loc = [lax.dot_general(kh[h], ev[h], (((0,), (0,)), ((), ())), preferred_element_type=F32)
             for h in heads]

    p = [(sc[h] * jnp.exp(dm[h] - m_t[h])).astype(BF16) for h in heads]
    out = [jnp.dot(p[h], vaug[h], preferred_element_type=F32) + jnp.exp(inter_log[h] - m_t[h]) * inter[h]
           for h in heads]
    for h in heads:
        ct_ref[h] = (jnp.exp(b_last[h] + m_prev[h] - m_new[h]) * ct[h]
                     + jnp.exp(m_loc[h] - m_new[h]) * c_loc[h])
        m_ref[h:h + 1, :] = jnp.broadcast_to(m_new[h], (1, LANES))
    hh = [out[h][:, :DH] / jnp.maximum(jnp.abs(out[h][:, DH:DH + 1]), jnp.exp(-m_t[h])) for h in heads]
    hn = [hh[h] * lax.rsqrt(jnp.mean(hh[h] * hh[h], axis=-1, keepdims=True) + EPS) * gm[:, sls[h]]
          for h in heads]
    for h in heads:
        og = zq_ref[:, 3 * D_MLSTM + h * DH:3 * D_MLSTM + (h + 1) * DH].astype(F32)
        hm_ref[slot, :, sls[h]] = (hn[h] * _sigmoid(og)).astype(BF16)


def _mlstm_outproj(zq, zg, w_conv, b_conv, b_gates, g_mix, tri, hl, x, mod, wm, wl):
    L = L_MIX
    nc = SEQ // L
    mixed = lambda b, j: (b * nc + jnp.minimum(j, nc - 1), 0)
    projected = lambda b, j: (b * nc + jnp.maximum(j - 1, 0), 0)
    const = lambda shape: pl.BlockSpec(shape, lambda b, j: (0, 0))
    return pl.pallas_call(
        _mlstm_outproj_kernel,
        grid=(BATCH, nc + 1),
        in_specs=[
            pl.BlockSpec((L, 4 * D_MLSTM), mixed),
            pl.BlockSpec((L, LANES), mixed),
            const((CONV_W, 2 * D_MLSTM)), const((1, 2 * D_MLSTM)), const((1, LANES)), const((1, D_MLSTM)),
            const((L, L)),
            pl.BlockSpec((L, D_LRU), projected),
            pl.BlockSpec((L, D_MODEL), projected),
            pl.BlockSpec((None, 6, D_MODEL), lambda b, j: (b, 0, 0)),
            const((D_MLSTM, D_MODEL)), const((D_LRU, D_MODEL)),
        ],
        out_specs=pl.BlockSpec((L, D_MODEL), projected),
        out_shape=jax.ShapeDtypeStruct((TOKENS, D_MODEL), F32),
        scratch_shapes=[
            pltpu.VMEM((N_HEADS, DH, 2 * DH), F32),
            pltpu.VMEM((SUBLANES, LANES), F32),
            pltpu.VMEM((SUBLANES + L, 2 * D_MLSTM), F32),
            pltpu.VMEM((2, L, D_MLSTM), BF16),
        ],
        compiler_params=pltpu.CompilerParams(
            dimension_semantics=("arbitrary", "arbitrary"), vmem_limit_bytes=VMEM_LIMIT),
        name="mlstm_outproj",
    )(zq, zg, w_conv, b_conv, b_gates, g_mix, tri, hl, x, mod, wm, wl)


def _swiglu(hb, wg_ref, wu_ref, wd_ref):
    g = jnp.dot(hb, wg_ref[...], preferred_element_type=F32)
    u = jnp.dot(hb, wu_ref[...], preferred_element_type=F32)
    act = g * _sigmoid(g) * u
    return jnp.dot(act.astype(BF16), wd_ref[...], preferred_element_type=F32)


def _ffn_kernel(x_ref, mod_ref, g_ref, wg_ref, wu_ref, wd_ref, o_ref):
    x = x_ref[...]
    hb = _mod_norm(x, g_ref[...], mod_ref[4:5, :], mod_ref[3:4, :]).astype(BF16)
    o_ref[...] = x + mod_ref[5:6, :] * _swiglu(hb, wg_ref, wu_ref, wd_ref)


def _ffn(x, mod, g, wg, wu, wd):
    tm = TM_FFN
    per_b = SEQ // tm
    return pl.pallas_call(
        _ffn_kernel,
        grid=(TOKENS // tm,),
        in_specs=[
            pl.BlockSpec((tm, D_MODEL), lambda i: (i, 0)),
            pl.BlockSpec((None, 6, D_MODEL), lambda i: (i // per_b, 0, 0)),
            pl.BlockSpec((1, D_MODEL), lambda i: (0, 0)),
            pl.BlockSpec((D_MODEL, D_FF), lambda i: (0, 0)),
            pl.BlockSpec((D_MODEL, D_FF), lambda i: (0, 0)),
            pl.BlockSpec((D_FF, D_MODEL), lambda i: (0, 0)),
        ],
        out_specs=pl.BlockSpec((tm, D_MODEL), lambda i: (i, 0)),
        out_shape=jax.ShapeDtypeStruct((TOKENS, D_MODEL), F32),
        compiler_params=pltpu.CompilerParams(
            dimension_semantics=("arbitrary",), vmem_limit_bytes=VMEM_LIMIT),
        name="ffn",
    )(x, mod, g, wg, wu, wd)


def _router_kernel(x_ref, mod_ref, g_ref, wrh_ref, wrl_ref, br_ref, stri_ref, upper_ref,
                   h_ref, meta_ref, tab_ref, carry_ref, lg_ref):
    i = pl.program_id(0)
    slot = i % 2

    @pl.when(i == 0)
    def _():
        carry_ref[...] = jnp.zeros_like(carry_ref)
        lg_ref[...] = jnp.zeros_like(lg_ref)

    tm = x_ref.shape[0]
    lane = lax.broadcasted_iota(jnp.int32, (tm, LANES), 1)
    logits = jnp.where(lane < N_EXPERTS, lg_ref[1 - slot], -jnp.inf)
    m1 = jnp.max(logits, axis=-1, keepdims=True)

    h = _mod_norm(x_ref[...], g_ref[...], mod_ref[4:5, :], mod_ref[3:4, :])
    hb = h.astype(BF16)
    h_ref[...] = hb
    i1 = jnp.min(jnp.where(logits == m1, lane, LANES), axis=-1, keepdims=True)

    hlo = (h - hb.astype(F32)).astype(BF16)
    wrh = wrh_ref[...]
    new_logits = jnp.dot(hb, wrh, preferred_element_type=F32)
    rest = jnp.where(lane == i1, -jnp.inf, logits)
    m2 = jnp.max(rest, axis=-1, keepdims=True)
    new_logits = new_logits + jnp.dot(hlo, wrh, preferred_element_type=F32)
    i2 = jnp.min(jnp.where(rest == m2, lane, LANES), axis=-1, keepdims=True)
    new_logits = new_logits + jnp.dot(hb, wrl_ref[...], preferred_element_type=F32)
    lg_ref[slot] = new_logits + br_ref[...]

    e2 = jnp.exp(m2 - m1)
    p1 = 1.0 / (1.0 + e2)
    p2 = e2 * p1
    ind = jnp.where((lane == i1) | (lane == i2), 1.0, 0.0)
    n_e = jnp.floor((jnp.sum(ind, axis=0, keepdims=True) + (SUBLANES - 1)) * (1.0 / SUBLANES)) * SUBLANES
    lrank = jnp.dot(stri_ref[...], ind.astype(BF16), preferred_element_type=F32)
    loff = jnp.dot(jnp.broadcast_to(n_e, (SUBLANES, LANES)).astype(BF16), upper_ref[...],
                   preferred_element_type=F32)[0:1, :]
    row = lrank + loff
    d1 = jnp.sum(jnp.where(lane == i1, row, 0.0), axis=-1, keepdims=True)
    d2 = jnp.sum(jnp.where(lane == i2, row, 0.0), axis=-1, keepdims=True)
    vals = (i1.astype(F32), i2.astype(F32), d1, d2, p1, p2)
    meta = jnp.zeros((tm, LANES), F32)
    for n, v in enumerate(vals):
        meta = jnp.where(lane == n, v, meta)
    meta_ref[...] = meta
    carry = carry_ref[0:1, :]
    srow = lax.broadcasted_iota(jnp.int32, (SUBLANES, LANES), 0)
    tab_ref[...] = jnp.where(srow == 0, n_e, jnp.where(srow == 1, carry, jnp.where(srow == 2, loff, 0.0)))
    carry_ref[0:1, :] = carry + jnp.where(i > 0, n_e, 0.0)


def _router(x, mod, g, wrh, wrl, br, stri, upper):
    tm = TM_ROUTE
    n = TOKENS // tm
    per_b = SEQ // tm
    row = lambda shape: pl.BlockSpec(shape, lambda i: (0, 0))
    normed = lambda i: jnp.minimum(i, n - 1)
    ranked = lambda i: jnp.maximum(i - 1, 0)
    return pl.pallas_call(
        _router_kernel,
        grid=(n + 1,),
        in_specs=[
            pl.BlockSpec((tm, D_MODEL), lambda i: (normed(i), 0)),
            pl.BlockSpec((None, 6, D_MODEL), lambda i: (normed(i) // per_b, 0, 0)),
            row((1, D_MODEL)), row((D_MODEL, LANES)), row((D_MODEL, LANES)), row((1, LANES)),
            row((tm, tm)), row((LANES, LANES)),
        ],
        out_specs=[
            pl.BlockSpec((tm, D_MODEL), lambda i: (normed(i), 0)),
            pl.BlockSpec((tm, LANES), lambda i: (ranked(i), 0)),
            pl.BlockSpec((None, SUBLANES, LANES), lambda i: (ranked(i), 0, 0)),
        ],
        out_shape=[
            jax.ShapeDtypeStruct((TOKENS, D_MODEL), BF16),
            jax.ShapeDtypeStruct((TOKENS, LANES), F32),
            jax.ShapeDtypeStruct((n, SUBLANES, LANES), F32),
        ],
        scratch_shapes=[pltpu.VMEM((SUBLANES, LANES), F32), pltpu.VMEM((2, tm, LANES), F32)],
        compiler_params=pltpu.CompilerParams(
            dimension_semantics=("arbitrary",), vmem_limit_bytes=VMEM_LIMIT),
        name="router",
    )(x, mod, g, wrh, wrl, br, stri, upper)


BLOCK_ROWS = 2 * TM_ROUTE + LANES
RANGE_PIECES = tuple(1 << b for b in range(TM_ROUTE.bit_length() - 1, 2, -1))
assert TM_SLOT <= 2 * RANGE_PIECES[0]


def _range_pieces(n, loc, slot, block_ref, slots_ref, sem, to_slots, visit):
    loc = pl.multiple_of(loc, SUBLANES)
    slot = pl.multiple_of(slot, SUBLANES)
    for piece in RANGE_PIECES:
        has = (n & piece) != 0
        src = block_ref.at[pl.ds(loc, piece)]
        dst = slots_ref.at[pl.ds(slot, piece)]
        if not to_slots:
            src, dst = dst, src

        @pl.when(has)
        def _():
            visit(pltpu.make_async_copy(src, dst, sem))

        step = jnp.where(has, piece, 0)
        loc = pl.multiple_of(loc + step, SUBLANES)
        slot = pl.multiple_of(slot + step, SUBLANES)


def _chunk_copies(chunk, n_ref, loc_ref, slot_ref, block_ref, slots_ref, sem, to_slots, visit):
    for e in range(N_EXPERTS):
        i = chunk * N_EXPERTS + e
        _range_pieces(n_ref[i], loc_ref[i], slot_ref[i], block_ref, slots_ref, sem, to_slots, visit)


def _start(cp):
    cp.start()


def _wait(cp):
    cp.wait()


def _choice_onehots(meta, width):
    col = lax.broadcasted_iota(jnp.int32, (meta.shape[0], width), 1)
    d1 = meta[:, 2:3].astype(jnp.int32)
    d2 = meta[:, 3:4].astype(jnp.int32)
    return col == d1, col == d2


def _zero_copies(padn_ref, pads_ref, used_ref, zero_ref, xs_ref, sem, visit):
    for e in range(N_EXPERTS):
        _range_pieces(padn_ref[e], 0, pads_ref[e], zero_ref, xs_ref, sem, True, visit)
    for t in range(2 * TOKENS // TM_SLOT, N_SLOTS // TM_SLOT):
        @pl.when(t * TM_SLOT >= used_ref[0])
        def _():
            visit(pltpu.make_async_copy(zero_ref, xs_ref.at[pl.ds(t * TM_SLOT, TM_SLOT)], sem))


def _dispatch_kernel(n_ref, loc_ref, slot_ref, padn_ref, pads_ref, used_ref, h_ref, meta_ref, xs_ref,
                     block_ref, zero_ref, sem, zsem):
    i = pl.program_id(0)
    last = pl.num_programs(0) - 1
    cur = i % 2
    zeros = (padn_ref, pads_ref, used_ref, zero_ref, xs_ref, zsem)

    @pl.when(i == 0)
    def _():
        zero_ref[...] = jnp.zeros_like(zero_ref)
        _zero_copies(*zeros, _start)

    a1, a2 = _choice_onehots(meta_ref[...], BLOCK_ROWS)
    sel = (a1 | a2).astype(BF16)
    block_ref[cur] = lax.dot_general(sel, h_ref[...], (((0,), (0,)), ((), ())),
                                     preferred_element_type=F32)
    tabs = (n_ref, loc_ref, slot_ref)
    _chunk_copies(i, *tabs, block_ref.at[cur], xs_ref, sem.at[cur], True, _start)

    @pl.when(i > 0)
    def _():
        _chunk_copies(i - 1, *tabs, block_ref.at[1 - cur], xs_ref, sem.at[1 - cur], True, _wait)

    @pl.when(i == last)
    def _():
        _chunk_copies(i, *tabs, block_ref.at[cur], xs_ref, sem.at[cur], True, _wait)
        _zero_copies(*zeros, _wait)


def _dispatch(tabs, pad_tabs, h, meta):
    tm = TM_ROUTE
    return pl.pallas_call(
        _dispatch_kernel,
        grid_spec=pltpu.PrefetchScalarGridSpec(
            num_scalar_prefetch=6,
            grid=(TOKENS // tm,),
            in_specs=[
                pl.BlockSpec((tm, D_MODEL), lambda i, *_: (i, 0)),
                pl.BlockSpec((tm, LANES), lambda i, *_: (i, 0)),
            ],
            out_specs=pl.BlockSpec(memory_space=pl.ANY),
            scratch_shapes=[pltpu.VMEM((2, BLOCK_ROWS, D_MODEL), F32),
                            pltpu.VMEM((TM_SLOT, D_MODEL), F32),
                            pltpu.SemaphoreType.DMA((2,)), pltpu.SemaphoreType.DMA],
        ),
        out_shape=jax.ShapeDtypeStruct((N_SLOTS, D_MODEL), F32),
        compiler_params=pltpu.CompilerParams(
            dimension_semantics=("arbitrary",), vmem_limit_bytes=VMEM_LIMIT),
        name="dispatch",
    )(*tabs, *pad_tabs, h, meta)


def _experts_kernel(te_ref, tv_ref, ts_ref, xs_ref, wg_ref, wu_ref, wd_ref, o_ref):
    del te_ref, ts_ref
    valid = tv_ref[pl.program_id(0)] != 0

    @pl.when(valid)
    def _():
        o_ref[...] = _swiglu(xs_ref[...].astype(BF16), wg_ref, wu_ref, wd_ref)

    @pl.when(jnp.logical_not(valid))
    def _():
        o_ref[...] = jnp.zeros_like(o_ref)


def _experts(tile_e, tile_v, tile_src, xs, wg, wu, wd):
    tm = TM_SLOT
    return pl.pallas_call(
        _experts_kernel,
        grid_spec=pltpu.PrefetchScalarGridSpec(
            num_scalar_prefetch=3,
            grid=(N_SLOTS // tm,),
            in_specs=[
                pl.BlockSpec((tm, D_MODEL), lambda i, te, tv, ts: (ts[i], 0)),
                pl.BlockSpec((None, D_MODEL, D_FF), lambda i, te, tv, ts: (te[i], 0, 0)),
                pl.BlockSpec((None, D_MODEL, D_FF), lambda i, te, tv, ts: (te[i], 0, 0)),
                pl.BlockSpec((None, D_FF, D_MODEL), lambda i, te, tv, ts: (te[i], 0, 0)),
            ],
            out_specs=pl.BlockSpec((tm, D_MODEL), lambda i, te, tv, ts: (i, 0)),
        ),
        out_shape=jax.ShapeDtypeStruct((N_SLOTS, D_MODEL), F32),
        compiler_params=pltpu.CompilerParams(
            dimension_semantics=("arbitrary",), vmem_limit_bytes=VMEM_LIMIT),
        name="experts",
    )(tile_e, tile_v, tile_src, xs, wg, wu, wd)


def _combine_kernel(n_ref, loc_ref, slot_ref, x_ref, mod_ref, meta_ref, gf_ref, ys_ref, o_ref,
                    block_ref, sem):
    i = pl.program_id(0)
    cur = i % 2
    tabs = (n_ref, loc_ref, slot_ref)

    @pl.when(i == 0)
    def _():
        block_ref[...] = jnp.zeros_like(block_ref)
        _chunk_copies(0, *tabs, block_ref.at[0], ys_ref, sem.at[0], False, _start)

    @pl.when(i + 1 < pl.num_programs(0))
    def _():
        _chunk_copies(i + 1, *tabs, block_ref.at[1 - cur], ys_ref, sem.at[1 - cur], False, _start)

    _chunk_copies(i, *tabs, block_ref.at[cur], ys_ref, sem.at[cur], False, _wait)
    meta = meta_ref[...]
    a1, a2 = _choice_onehots(meta, BLOCK_ROWS)
    blk = block_ref[cur].astype(BF16)
    y = (meta[:, 4:5] * jnp.dot(a1.astype(BF16), blk, preferred_element_type=F32)
         + meta[:, 5:6] * jnp.dot(a2.astype(BF16), blk, preferred_element_type=F32))
    y = x_ref[...] + mod_ref[5:6, :] * y
    o_ref[...] = y * lax.rsqrt(jnp.mean(y * y, axis=-1, keepdims=True) + EPS) * gf_ref[...]


def _combine(tabs, x, mod, meta, g_final, ys):
    tm = TM_ROUTE
    per_b = SEQ // tm
    return pl.pallas_call(
        _combine_kernel,
        grid_spec=pltpu.PrefetchScalarGridSpec(
            num_scalar_prefetch=3,
            grid=(TOKENS // tm,),
            in_specs=[
                pl.BlockSpec((tm, D_MODEL), lambda i, *_: (i, 0)),
                pl.BlockSpec((None, 6, D_MODEL), lambda i, *_: (i // per_b, 0, 0)),
                pl.BlockSpec((tm, LANES), lambda i, *_: (i, 0)),
                pl.BlockSpec((1, D_MODEL), lambda i, *_: (0, 0)),
                pl.BlockSpec(memory_space=pl.ANY),
            ],
            out_specs=pl.BlockSpec((tm, D_MODEL), lambda i, *_: (i, 0)),
            scratch_shapes=[pltpu.VMEM((2, BLOCK_ROWS, D_MODEL), F32), pltpu.SemaphoreType.DMA((2,))],
        ),
        out_shape=jax.ShapeDtypeStruct((TOKENS, D_MODEL), F32),
        compiler_params=pltpu.CompilerParams(
            dimension_semantics=("arbitrary",), vmem_limit_bytes=VMEM_LIMIT),
        name="combine",
    )(*tabs, x, mod, meta, g_final, ys)


def _moe(x, mod, g, w_router, b_router, wg, wu, wd, g_final):
    wr = _pad_lanes(w_router)
    wrh = wr.astype(BF16)
    wrl = (wr - wrh.astype(F32)).astype(BF16)
    stri = (lax.broadcasted_iota(jnp.int32, (TM_ROUTE, TM_ROUTE), 1)
            < lax.broadcasted_iota(jnp.int32, (TM_ROUTE, TM_ROUTE), 0)).astype(BF16)
    upper = (lax.broadcasted_iota(jnp.int32, (LANES, LANES), 0)
             < lax.broadcasted_iota(jnp.int32, (LANES, LANES), 1)).astype(BF16)
    h, meta, tab = _router(x, mod, g, wrh, wrl, _pad_lanes(b_router[None]), stri, upper)
    tab = tab[:, :3, :N_EXPERTS].astype(jnp.int32)
    n_ce, before_ce, loc_ce = tab[:, 0], tab[:, 1], tab[:, 2]
    counts = before_ce[-1] + n_ce[-1]
    padded = (counts + TM_SLOT - 1) // TM_SLOT * TM_SLOT
    ends = jnp.cumsum(padded)
    slot_ce = (ends - padded)[None, :] + before_ce
    tabs = (n_ce.reshape(-1), loc_ce.reshape(-1), slot_ce.reshape(-1))
    pad_tabs = (padded - counts, ends - padded + counts, ends[-1:])
    tile = jnp.arange(N_SLOTS // TM_SLOT, dtype=jnp.int32)
    tile_e = jnp.minimum(jnp.sum(tile[:, None] * TM_SLOT >= ends[None, :], axis=1), N_EXPERTS - 1)
    tile_v = (tile * TM_SLOT < ends[-1]).astype(jnp.int32)
    tile_src = jnp.minimum(tile, ends[-1] // TM_SLOT - 1)
    xs = _dispatch(tabs, pad_tabs, h, meta)
    ys = _experts(tile_e.astype(jnp.int32), tile_v, tile_src, xs, wg, wu, wd)
    return _combine(tabs, x, mod, meta, g_final, ys)


def _block_diag(w):
    eye = jnp.eye(N_LRU_BLOCKS, dtype=w.dtype)
    return jnp.einsum('nde,nm->ndme', w, eye).reshape(D_LRU, D_LRU)


def _pad_lanes(a):
    return jnp.zeros(a.shape[:-1] + (LANES,), a.dtype).at[..., :a.shape[-1]].set(a)


def kernel(x, c, w_ada, b_ada, g_norm_mix, g_norm_ffn, w_in, w_conv_qk, b_conv_qk, b_gates,
           w_conv_lru, b_conv_lru, w_lru_a, b_lru_a, w_lru_x, b_lru_x, lru_lambda, g_mix_out, w_out,
           w_ff_gate, w_ff_up, w_ff_down, w_router, b_router, w_exp_gate, w_exp_up, w_exp_down, g_final):
    assert DEPTH == 2
    mods = _ada(c, w_ada, b_ada)
    tri = (lax.broadcasted_iota(jnp.int32, (L_MIX, L_MIX), 1)
           <= lax.broadcasted_iota(jnp.int32, (L_MIX, L_MIX), 0)).astype(BF16)
    xt = x.reshape(TOKENS, D_MODEL)
    for l in range(DEPTH):
        mod = mods[l]
        wi = w_in[l]
        n_q = 4 * D_MLSTM
        wcat = jnp.concatenate([wi[:, :n_q], wi[:, n_q + 2 * N_HEADS:],
                                _pad_lanes(wi[:, n_q:n_q + 2 * N_HEADS])], axis=1).astype(BF16)
        wax = jnp.concatenate([_block_diag(w_lru_a[l]), _block_diag(w_lru_x[l])], axis=1).astype(BF16)
        bax = jnp.concatenate([b_lru_a[l], b_lru_x[l]])[None]
        zq, zg, hl = _inproj_lru(xt, mod, g_norm_mix[l][None], wcat,
                                 w_conv_lru[l], b_conv_lru[l][None], wax, bax, lru_lambda[l][None],
                                 g_mix_out[l][None, D_MLSTM:])
        wo = w_out[l].astype(BF16)
        xt = _mlstm_outproj(zq, zg, w_conv_qk[l], b_conv_qk[l][None], _pad_lanes(b_gates[l][None]),
                            g_mix_out[l][None, :D_MLSTM], tri, hl, xt, mod, wo[:D_MLSTM], wo[D_MLSTM:])
        jj = l // 2
        if l % 2 == 0:
            xt = _ffn(xt, mod, g_norm_ffn[l][None], w_ff_gate[jj].astype(BF16),
                      w_ff_up[jj].astype(BF16), w_ff_down[jj].astype(BF16))
        else:
            xt = _moe(xt, mod, g_norm_ffn[l][None], w_router[jj], b_router[jj],
                      w_exp_gate[jj].astype(BF16), w_exp_up[jj].astype(BF16),
                      w_exp_down[jj].astype(BF16), g_final[None])
    return xt.reshape(BATCH, SEQ, D_MODEL)
```

```python
import functools

import jax
import jax.numpy as jnp
from jax import lax
from jax.experimental import pallas as pl
from jax.experimental.pallas import tpu as pltpu

F32 = jnp.float32
BF16 = jnp.bfloat16

D_MODEL = 1024
BATCH = 4
SEQ = 8192
TOKENS = BATCH * SEQ
DEPTH = 2
D_MLSTM = 512
N_HEADS = 4
DH = 128
D_LRU = 512
N_LRU_BLOCKS = 8
DB_LRU = 64
CONV_W = 4
LRU_C = 8.0
D_FF = 2816
N_EXPERTS = 8
EPS = 1e-6

LANES = 128
SUBLANES = 8
VMEM_LIMIT = 56 * 1024 * 1024

L_MIX = 256
PROJ_CHUNK = 256
TM_FFN = 256
TM_ROUTE = 256
TM_SLOT = 256
N_SLOTS = 2 * TOKENS + N_EXPERTS * TM_SLOT


def _sigmoid(x):
    return 1.0 / (1.0 + jnp.exp(-x))


def _mod_norm(x, g, scale, shift):
    ms = jnp.mean(x * x, axis=-1, keepdims=True)
    return (x * lax.rsqrt(ms + EPS)) * g * (1.0 + scale) + shift


def _causal_conv(x, ext_ref, w, b):
    L = x.shape[0]
    ext_ref[SUBLANES:, :] = x
    acc = b + w[CONV_W - 1:CONV_W] * x
    for s in range(1, CONV_W):
        acc = acc + w[CONV_W - 1 - s:CONV_W - s] * ext_ref[SUBLANES - s:SUBLANES - s + L, :]
    ext_ref[:SUBLANES, :] = x[L - SUBLANES:, :]
    return acc


def _zero_tile(z):
    bits = lax.bitcast_convert_type(z[-SUBLANES:, -LANES:], jnp.uint32)
    return ((bits >> 16) >> 16).astype(F32)


def _after(v, zeros, lag=0):
    if len(zeros) <= lag:
        return v
    tile = v[:SUBLANES, :LANES]
    while len(zeros) > lag:
        tile = tile + zeros.pop(0)
    head = tile if v.shape[1] == LANES else jnp.concatenate([tile, v[:SUBLANES, LANES:]], axis=1)
    return jnp.concatenate([head, v[SUBLANES:]], axis=0)


def _ada_kernel(c_ref, w_ref, b_ref, o_ref):
    c = c_ref[...]
    cs = c * _sigmoid(c)
    o_ref[...] = jnp.dot(cs, w_ref[...], preferred_element_type=F32) + b_ref[...]


def _ada(c, w_ada, b_ada):
    tn = 1536
    c8 = jnp.zeros((SUBLANES, D_MODEL), F32).at[:BATCH].set(c)
    out = pl.pallas_call(
        _ada_kernel,
        grid=(DEPTH, 6 * D_MODEL // tn),
        in_specs=[
            pl.BlockSpec((SUBLANES, D_MODEL), lambda l, n: (0, 0)),
            pl.BlockSpec((None, D_MODEL, tn), lambda l, n: (l, 0, n)),
            pl.BlockSpec((None, 1, tn), lambda l, n: (l, 0, n)),
        ],
        out_specs=pl.BlockSpec((None, SUBLANES, tn), lambda l, n: (l, 0, n)),
        out_shape=jax.ShapeDtypeStruct((DEPTH, SUBLANES, 6 * D_MODEL), F32),
        compiler_params=pltpu.CompilerParams(
            dimension_semantics=("arbitrary", "arbitrary"), vmem_limit_bytes=VMEM_LIMIT),
        name="ada",
    )(c8, w_ada, b_ada.reshape(DEPTH, 1, 6 * D_MODEL))
    return out[:, :BATCH].reshape(DEPTH, BATCH, 6, D_MODEL)


def _inproj_lru_kernel(x_ref, mod_ref, g_ref, w_ref, wc_ref, bc_ref, wax_ref, bax_ref, lam_ref, gm_ref,
                       zq_ref, zg_ref, hl_ref, zl_ref, ext_ref, hc_ref):
    L = L_MIX
    TIE_LAG = 4
    j = pl.program_id(1)
    slot = j % 2

    @pl.when((pl.program_id(0) == 0) & (j == 0))
    def _():
        zl_ref[...] = jnp.zeros_like(zl_ref)
        ext_ref[...] = jnp.zeros_like(ext_ref)
        hc_ref[...] = jnp.zeros_like(hc_ref)

    keep = j != 1
    ext_ref[:SUBLANES, :] = jnp.where(keep, ext_ref[:SUBLANES, :], 0.0)
    hc_ref[...] = jnp.where(keep, hc_ref[...], 0.0)
    xc = _causal_conv(zl_ref[1 - slot, :, :D_LRU].astype(F32), ext_ref, wc_ref[...], bc_ref[...])

    mod = mod_ref[...]
    h = _mod_norm(x_ref[...], g_ref[...], mod[1:2], mod[0:1]).astype(BF16)
    n_q, n_l = 4 * D_MLSTM, 2 * D_LRU
    todo = list(range(0, n_q + n_l + LANES, PROJ_CHUNK))
    zeros = []

    def project(n):
        for lo in todo[:n]:
            hi = min(lo + PROJ_CHUNK, n_q + n_l + LANES)
            z = jnp.dot(h, w_ref[:, lo:hi], preferred_element_type=F32)
            zeros.append(_zero_tile(z))
            if hi <= n_q:
                zq_ref[:, lo:hi] = z.astype(BF16)
            elif hi <= n_q + n_l:
                zl_ref[slot, :, lo - n_q:hi - n_q] = z.astype(BF16)
            else:
                zg_ref[...] = z
        del todo[:n]

    project(2)
    xc = _after(xc, zeros, TIE_LAG)
    gates = jnp.dot(xc.astype(BF16), wax_ref[...], preferred_element_type=F32) + bax_ref[...]
    r = _sigmoid(gates[:, :D_LRU])
    i = _sigmoid(gates[:, D_LRU:])
    project(2)
    r = _after(r, zeros, TIE_LAG)
    lam = lam_ref[...]
    sp = jnp.maximum(-lam, 0.0) + jnp.log1p(jnp.exp(-jnp.abs(lam)))
    log_a = -LRU_C * r * sp
    a = jnp.exp(log_a)
    y2 = 2.0 * log_a
    series = -y2 * (1.0 + y2 * (0.5 + y2 * (1.0 / 6.0 + y2 * (1.0 / 24.0))))
    u = jnp.sqrt(jnp.where(y2 > -0.01, series, 1.0 - a * a)) * (i * xc)
    project(2)
    u = _after(u, zeros, TIE_LAG)

    r8 = lax.broadcasted_iota(jnp.int32, (L, D_LRU), 0) & (SUBLANES - 1)
    for s in (1, 2, 4):
        a_sh = pltpu.roll(a, s, 0)
        u_sh = pltpu.roll(u, s, 0)
        valid = r8 >= s
        u = jnp.where(valid, a * u_sh + u, u)
        a = jnp.where(valid, a * a_sh, a)
        project(2)
        u = _after(u, zeros, TIE_LAG)
    gl = zl_ref[1 - slot, :, D_LRU:].astype(F32)
    gelu = 0.5 * gl * (1.0 + jnp.tanh(0.7978845608028654 * (gl + 0.044715 * gl * gl * gl)))
    project(len(todo))
    hc = hc_ref[...]
    enter = []
    for gi in range(L // SUBLANES):
        enter.append(hc)
        last = gi * SUBLANES + SUBLANES - 1
        hc = u[last:last + 1] + a[last:last + 1] * hc
    hc_ref[...] = hc
    hseq = jnp.concatenate(
        [u[gi * SUBLANES:(gi + 1) * SUBLANES] + a[gi * SUBLANES:(gi + 1) * SUBLANES] * enter[gi]
         for gi in range(L // SUBLANES)], axis=0)
    hl = hseq * gelu
    hl = hl * lax.rsqrt(jnp.mean(hl * hl, axis=-1, keepdims=True) + EPS) * gm_ref[...]
    hl_ref[...] = hl.astype(BF16)


def _inproj_lru(x, mod, g, w, w_conv, b_conv, wax, bax, lam, g_mix):
    L = L_MIX
    nc = SEQ // L
    proj = lambda b, j: (b * nc + jnp.minimum(j, nc - 1), 0)
    mixed = lambda b, j: (b * nc + jnp.maximum(j - 1, 0), 0)
    const = lambda shape: pl.BlockSpec(shape, lambda b, j: (0, 0))
    return pl.pallas_call(
        _inproj_lru_kernel,
        grid=(BATCH, nc + 1),
        in_specs=[
            pl.BlockSpec((L, D_MODEL), proj),
            pl.BlockSpec((None, 6, D_MODEL), lambda b, j: (b, 0, 0)),
            const((1, D_MODEL)), const(w.shape),
            const((CONV_W, D_LRU)), const((1, D_LRU)), const((D_LRU, 2 * D_LRU)),
            const((1, 2 * D_LRU)), const((1, D_LRU)), const((1, D_LRU)),
        ],
        out_specs=[
            pl.BlockSpec((L, 4 * D_MLSTM), proj),
            pl.BlockSpec((L, LANES), proj),
            pl.BlockSpec((L, D_LRU), mixed),
        ],
        out_shape=[
            jax.ShapeDtypeStruct((TOKENS, 4 * D_MLSTM), BF16),
            jax.ShapeDtypeStruct((TOKENS, LANES), F32),
            jax.ShapeDtypeStruct((TOKENS, D_LRU), BF16),
        ],
        scratch_shapes=[
            pltpu.VMEM((2, L, 2 * D_LRU), BF16),
            pltpu.VMEM((SUBLANES + L, D_LRU), F32),
            pltpu.VMEM((1, D_LRU), F32),
        ],
        compiler_params=pltpu.CompilerParams(
            dimension_semantics=("arbitrary", "arbitrary"), vmem_limit_bytes=VMEM_LIMIT),
        name="inproj_rglru",
    )(x, mod, g, w, w_conv, b_conv, wax, bax, lam, g_mix)


def _mlstm_outproj_kernel(zq_ref, zg_ref, wc_ref, bc_ref, bg_ref, gm_ref, tri_ref,
                          hl_ref, x_ref, mod_ref, wm_ref, wl_ref, o_ref,
                          ct_ref, m_ref, ext_ref, hm_ref):
    L = L_MIX
    j = pl.program_id(1)
    slot = j % 2

    @pl.when((pl.program_id(0) == 0) & (j == 0))
    def _():
        hm_ref[...] = jnp.zeros_like(hm_ref)

    @pl.when(j == 0)
    def _():
        ct_ref[...] = jnp.zeros_like(ct_ref)
        m_ref[...] = jnp.zeros_like(m_ref)
        ext_ref[:SUBLANES, :] = jnp.zeros((SUBLANES, ext_ref.shape[1]), F32)

    todo = list(range(0, D_MODEL, PROJ_CHUNK))
    zeros = []

    def project(n):
        for lo in todo[:n]:
            cs = slice(lo, lo + PROJ_CHUNK)
            y = (jnp.dot(hm_ref[1 - slot], wm_ref[:, cs], preferred_element_type=F32)
                 + jnp.dot(hl_ref[...], wl_ref[:, cs], preferred_element_type=F32))
            zeros.append(_zero_tile(y))
            o_ref[:, cs] = x_ref[:, cs] + mod_ref[2:3, cs] * y
        del todo[:n]

    project(1)
    qk = _causal_conv(zq_ref[:, :2 * D_MLSTM].astype(F32), ext_ref, wc_ref[...], bc_ref[...])
    qk = _after(qk, zeros)
    project(1)
    qk = qk * _sigmoid(qk)
    q = _after(qk[:, :D_MLSTM], zeros).astype(BF16)
    k = (qk[:, D_MLSTM:] * (DH ** -0.5)).astype(BF16)
    project(1)

    g = zg_ref[...] + bg_ref[...]
    lf = jnp.minimum(g, 0.0) - jnp.log1p(jnp.exp(-jnp.abs(g)))
    tri = tri_ref[...]
    hi = lf.astype(BF16)
    r1 = lf - hi.astype(F32)
    mid = r1.astype(BF16)
    lo = (r1 - mid.astype(F32)).astype(BF16)
    bcum = (jnp.dot(tri, hi, preferred_element_type=F32)
            + jnp.dot(tri, mid, preferred_element_type=F32)
            + jnp.dot(tri, lo, preferred_element_type=F32))
    lane = lax.broadcasted_iota(jnp.int32, (L, LANES), 1)
    cols = _after(jnp.where(lane < N_HEADS, g, bcum), zeros)
    project(len(todo))
    rows = jnp.transpose(cols)

    t_idx = lax.broadcasted_iota(jnp.int32, (L, L), 0)
    s_idx = lax.broadcasted_iota(jnp.int32, (L, L), 1)
    causal = s_idx <= t_idx
    ones_col = (lax.broadcasted_iota(jnp.int32, (L, DH), 1) == 0).astype(BF16)
    gm = gm_ref[...]

    heads = range(N_HEADS)
    sls = [slice(h * DH, (h + 1) * DH) for h in heads]
    ic = [cols[:, h:h + 1] for h in heads]
    bc = [cols[:, N_HEADS + h:N_HEADS + h + 1] for h in heads]
    ir = [rows[h:h + 1, :] for h in heads]
    br = [rows[N_HEADS + h:N_HEADS + h + 1, :] for h in heads]
    b_last = [br[h][:, L - 1:L] for h in heads]
    m_prev = [m_ref[h:h + 1, 0:1] for h in heads]
    qh = [q[:, sls[h]] for h in heads]
    kh = [k[:, sls[h]] for h in heads]
    vaug = [jnp.concatenate([zq_ref[:, 2 * D_MLSTM + h * DH:2 * D_MLSTM + (h + 1) * DH], ones_col], axis=1)
            for h in heads]
    ct = [ct_ref[h] for h in heads]

    dm = [jnp.where(causal, bc[h] - br[h] + ir[h], -jnp.inf) for h in heads]
    dm[0] = _after(dm[0], zeros)
    inter_log = [bc[h] + m_prev[h] for h in heads]
    sc = [lax.dot_general(qh[h], kh[h], (((1,), (1,)), ((), ())), preferred_element_type=F32) for h in heads]
    inter = [jnp.dot(qh[h], ct[h].astype(BF16), preferred_element_type=F32) for h in heads]
    m_t = [jnp.maximum(inter_log[h], jnp.max(dm[h], axis=1, keepdims=True)) for h in heads]

    w_end = [b_last[h] - bc[h] + ic[h] for h in heads]
    m_loc = [jnp.max(w_end[h], axis=0, keepdims=True) for h in heads]
    m_new = [jnp.maximum(b_last[h] + m_prev[h], m_loc[h]) for h in heads]
    ev = [(jnp.exp(w_end[h] - m_loc[h]) * vaug[h].astype(F32)).astype(BF16) for h in heads]
    c_loc = [lax.dot_general(kh[h], ev[h], (((0,), (0,)), ((), ())), preferred_element_type=F32)
             for h in heads]

    p = [(sc[h] * jnp.exp(dm[h] - m_t[h])).astype(BF16) for h in heads]
    out = [jnp.dot(p[h], vaug[h], preferred_element_type=F32) + jnp.exp(inter_log[h] - m_t[h]) * inter[h]
           for h in heads]
    for h in heads:
        ct_ref[h] = (jnp.exp(b_last[h] + m_prev[h] - m_new[h]) * ct[h]
                     + jnp.exp(m_loc[h] - m_new[h]) * c_loc[h])
        m_ref[h:h + 1, :] = jnp.broadcast_to(m_new[h], (1, LANES))
    hh = [out[h][:, :DH] / jnp.maximum(jnp.abs(out[h][:, DH:DH + 1]), jnp.exp(-m_t[h])) for h in heads]
    hn = [hh[h] * lax.rsqrt(jnp.mean(hh[h] * hh[h], axis=-1, keepdims=True) + EPS) * gm[:, sls[h]]
          for h in heads]
    for h in heads:
        og = zq_ref[:, 3 * D_MLSTM + h * DH:3 * D_MLSTM + (h + 1) * DH].astype(F32)
        hm_ref[slot, :, sls[h]] = (hn[h] * _sigmoid(og)).astype(BF16)


def _mlstm_outproj(zq, zg, w_conv, b_conv, b_gates, g_mix, tri, hl, x, mod, wm, wl):
    L = L_MIX
    nc = SEQ // L
    mixed = lambda b, j: (b * nc + jnp.minimum(j, nc - 1), 0)
    projected = lambda b, j: (b * nc + jnp.maximum(j - 1, 0), 0)
    const = lambda shape: pl.BlockSpec(shape, lambda b, j: (0, 0))
    return pl.pallas_call(
        _mlstm_outproj_kernel,
        grid=(BATCH, nc + 1),
        in_specs=[
            pl.BlockSpec((L, 4 * D_MLSTM), mixed),
            pl.BlockSpec((L, LANES), mixed),
            const((CONV_W, 2 * D_MLSTM)), const((1, 2 * D_MLSTM)), const((1, LANES)), const((1, D_MLSTM)),
            const((L, L)),
            pl.BlockSpec((L, D_LRU), projected),
            pl.BlockSpec((L, D_MODEL), projected),
            pl.BlockSpec((None, 6, D_MODEL), lambda b, j: (b, 0, 0)),
            const((D_MLSTM, D_MODEL)), const((D_LRU, D_MODEL)),
        ],
        out_specs=pl.BlockSpec((L, D_MODEL), projected),
        out_shape=jax.ShapeDtypeStruct((TOKENS, D_MODEL), F32),
        scratch_shapes=[
            pltpu.VMEM((N_HEADS, DH, 2 * DH), F32),
            pltpu.VMEM((SUBLANES, LANES), F32),
            pltpu.VMEM((SUBLANES + L, 2 * D_MLSTM), F32),
            pltpu.VMEM((2, L, D_MLSTM), BF16),
        ],
        compiler_params=pltpu.CompilerParams(
            dimension_semantics=("arbitrary", "arbitrary"), vmem_limit_bytes=VMEM_LIMIT),
        name="mlstm_outproj",
    )(zq, zg, w_conv, b_conv, b_gates, g_mix, tri, hl, x, mod, wm, wl)


def _swiglu(hb, wg_ref, wu_ref, wd_ref):
    g = jnp.dot(hb, wg_ref[...], preferred_element_type=F32)
    u = jnp.dot(hb, wu_ref[...], preferred_element_type=F32)
    act = g * _sigmoid(g) * u
    return jnp.dot(act.astype(BF16), wd_ref[...], preferred_element_type=F32)


def _ffn_kernel(x_ref, mod_ref, g_ref, wg_ref, wu_ref, wd_ref, o_ref):
    x = x_ref[...]
    hb = _mod_norm(x, g_ref[...], mod_ref[4:5, :], mod_ref[3:4, :]).astype(BF16)
    o_ref[...] = x + mod_ref[5:6, :] * _swiglu(hb, wg_ref, wu_ref, wd_ref)


def _ffn(x, mod, g, wg, wu, wd):
    tm = TM_FFN
    per_b = SEQ // tm
    return pl.pallas_call(
        _ffn_kernel,
        grid=(TOKENS // tm,),
        in_specs=[
            pl.BlockSpec((tm, D_MODEL), lambda i: (i, 0)),
            pl.BlockSpec((None, 6, D_MODEL), lambda i: (i // per_b, 0, 0)),
            pl.BlockSpec((1, D_MODEL), lambda i: (0, 0)),
            pl.BlockSpec((D_MODEL, D_FF), lambda i: (0, 0)),
            pl.BlockSpec((D_MODEL, D_FF), lambda i: (0, 0)),
            pl.BlockSpec((D_FF, D_MODEL), lambda i: (0, 0)),
        ],
        out_specs=pl.BlockSpec((tm, D_MODEL), lambda i: (i, 0)),
        out_shape=jax.ShapeDtypeStruct((TOKENS, D_MODEL), F32),
        compiler_params=pltpu.CompilerParams(
            dimension_semantics=("arbitrary",), vmem_limit_bytes=VMEM_LIMIT),
        name="ffn",
    )(x, mod, g, wg, wu, wd)


def _router_kernel(x_ref, mod_ref, g_ref, wrh_ref, wrl_ref, br_ref, stri_ref, upper_ref,
                   h_ref, meta_ref, tab_ref, carry_ref, lg_ref):
    i = pl.program_id(0)
    slot = i % 2

    @pl.when(i == 0)
    def _():
        carry_ref[...] = jnp.zeros_like(carry_ref)
        lg_ref[...] = jnp.zeros_like(lg_ref)

    tm = x_ref.shape[0]
    lane = lax.broadcasted_iota(jnp.int32, (tm, LANES), 1)
    logits = jnp.where(lane < N_EXPERTS, lg_ref[1 - slot], -jnp.inf)
    m1 = jnp.max(logits, axis=-1, keepdims=True)

    h = _mod_norm(x_ref[...], g_ref[...], mod_ref[4:5, :], mod_ref[3:4, :])
    hb = h.astype(BF16)
    h_ref[...] = hb
    i1 = jnp.min(jnp.where(logits == m1, lane, LANES), axis=-1, keepdims=True)

    hlo = (h - hb.astype(F32)).astype(BF16)
    wrh = wrh_ref[...]
    new_logits = jnp.dot(hb, wrh, preferred_element_type=F32)
    rest = jnp.where(lane == i1, -jnp.inf, logits)
    m2 = jnp.max(rest, axis=-1, keepdims=True)
    new_logits = new_logits + jnp.dot(hlo, wrh, preferred_element_type=F32)
    i2 = jnp.min(jnp.where(rest == m2, lane, LANES), axis=-1, keepdims=True)
    new_logits = new_logits + jnp.dot(hb, wrl_ref[...], preferred_element_type=F32)
    lg_ref[slot] = new_logits + br_ref[...]

    e2 = jnp.exp(m2 - m1)
    p1 = 1.0 / (1.0 + e2)
    p2 = e2 * p1
    ind = jnp.where((lane == i1) | (lane == i2), 1.0, 0.0)
    carry = carry_ref[0:1, :]
    aligned = jnp.floor(carry * (1.0 / SUBLANES)) * SUBLANES
    lead = carry - aligned
    filled = lead + jnp.sum(ind, axis=0, keepdims=True)
    full = jnp.floor(filled * (1.0 / SUBLANES)) * SUBLANES
    n_e = jnp.floor((filled + (SUBLANES - 1)) * (1.0 / SUBLANES)) * SUBLANES
    lrank = jnp.dot(stri_ref[...], ind.astype(BF16), preferred_element_type=F32)
    loff = jnp.dot(jnp.broadcast_to(n_e, (SUBLANES, LANES)).astype(BF16), upper_ref[...],
                   preferred_element_type=F32)[0:1, :]
    row = lrank + loff + lead
    d1 = jnp.sum(jnp.where(lane == i1, row, 0.0), axis=-1, keepdims=True)
    d2 = jnp.sum(jnp.where(lane == i2, row, 0.0), axis=-1, keepdims=True)
    vals = (i1.astype(F32), i2.astype(F32), d1, d2, p1, p2)
    meta = jnp.zeros((tm, LANES), F32)
    for n, v in enumerate(vals):
        meta = jnp.where(lane == n, v, meta)
    meta_ref[...] = meta
    srow = lax.broadcasted_iota(jnp.int32, (SUBLANES, LANES), 0)
    carry = carry + jnp.where(i > 0, filled - lead, 0.0)
    tab = jnp.zeros((SUBLANES, LANES), F32)
    for n, v in enumerate((n_e, aligned, loff, full, carry)):
        tab = jnp.where(srow == n, v, tab)
    tab_ref[...] = tab
    carry_ref[0:1, :] = carry


def _router(x, mod, g, wrh, wrl, br, stri, upper):
    tm = TM_ROUTE
    n = TOKENS // tm
    per_b = SEQ // tm
    row = lambda shape: pl.BlockSpec(shape, lambda i: (0, 0))
    normed = lambda i: jnp.minimum(i, n - 1)
    ranked = lambda i: jnp.maximum(i - 1, 0)
    return pl.pallas_call(
        _router_kernel,
        grid=(n + 1,),
        in_specs=[
            pl.BlockSpec((tm, D_MODEL), lambda i: (normed(i), 0)),
            pl.BlockSpec((None, 6, D_MODEL), lambda i: (normed(i) // per_b, 0, 0)),
            row((1, D_MODEL)), row((D_MODEL, LANES)), row((D_MODEL, LANES)), row((1, LANES)),
            row((tm, tm)), row((LANES, LANES)),
        ],
        out_specs=[
            pl.BlockSpec((tm, D_MODEL), lambda i: (normed(i), 0)),
            pl.BlockSpec((tm, LANES), lambda i: (ranked(i), 0)),
            pl.BlockSpec((None, SUBLANES, LANES), lambda i: (ranked(i), 0, 0)),
        ],
        out_shape=[
            jax.ShapeDtypeStruct((TOKENS, D_MODEL), BF16),
            jax.ShapeDtypeStruct((TOKENS, LANES), F32),
            jax.ShapeDtypeStruct((n, SUBLANES, LANES), F32),
        ],
        scratch_shapes=[pltpu.VMEM((SUBLANES, LANES), F32), pltpu.VMEM((2, tm, LANES), F32)],
        compiler_params=pltpu.CompilerParams(
            dimension_semantics=("arbitrary",), vmem_limit_bytes=VMEM_LIMIT),
        name="router",
    )(x, mod, g, wrh, wrl, br, stri, upper)


BLOCK_ROWS = 2 * TM_ROUTE + LANES
RANGE_PIECES = tuple(1 << b for b in range(TM_ROUTE.bit_length() - 1, 2, -1))
assert TM_SLOT <= 2 * RANGE_PIECES[0]


def _range_pieces(n, loc, slot, block_ref, slots_ref, sem, to_slots, visit):
    loc = pl.multiple_of(loc, SUBLANES)
    slot = pl.multiple_of(slot, SUBLANES)
    for piece in RANGE_PIECES:
        has = (n & piece) != 0
        src = block_ref.at[pl.ds(loc, piece)]
        dst = slots_ref.at[pl.ds(slot, piece)]
        if not to_slots:
            src, dst = dst, src

        @pl.when(has)
        def _():
            visit(pltpu.make_async_copy(src, dst, sem))

        step = jnp.where(has, piece, 0)
        loc = pl.multiple_of(loc + step, SUBLANES)
        slot = pl.multiple_of(slot + step, SUBLANES)


def _chunk_copies(chunk, n_ref, loc_ref, slot_ref, block_ref, slots_ref, sem, to_slots, visit):
    for e in range(N_EXPERTS):
        i = chunk * N_EXPERTS + e
        _range_pieces(n_ref[i], loc_ref[i], slot_ref[i], block_ref, slots_ref, sem, to_slots, visit)


def _start(cp):
    cp.start()


def _wait(cp):
    cp.wait()


def _choice_onehots(meta, width):
    col = lax.broadcasted_iota(jnp.int32, (meta.shape[0], width), 1)
    d1 = meta[:, 2:3].astype(jnp.int32)
    d2 = meta[:, 3:4].astype(jnp.int32)
    return col == d1, col == d2


def _zero_copies(padn_ref, pads_ref, used_ref, zero_ref, xs_ref, sem, visit):
    for e in range(N_EXPERTS):
        _range_pieces(padn_ref[e], 0, pads_ref[e], zero_ref, xs_ref, sem, True, visit)
    for t in range(2 * TOKENS // TM_SLOT, N_SLOTS // TM_SLOT):
        @pl.when(t * TM_SLOT >= used_ref[0])
        def _():
            visit(pltpu.make_async_copy(zero_ref, xs_ref.at[pl.ds(t * TM_SLOT, TM_SLOT)], sem))


def _dispatch_kernel(n_ref, loc_ref, slot_ref, full_ref, padn_ref, pads_ref, used_ref, h_ref, meta_ref,
                     xs_ref, block_ref, zero_ref, tail_ref, sem, zsem):
    i = pl.program_id(0)
    last = pl.num_programs(0) - 1
    cur = i % 2
    zeros = (padn_ref, pads_ref, used_ref, zero_ref, xs_ref, zsem)

    @pl.when(i == 0)
    def _():
        zero_ref[...] = jnp.zeros_like(zero_ref)
        tail_ref[...] = jnp.zeros_like(tail_ref)
        _zero_copies(*zeros, _start)

    a1, a2 = _choice_onehots(meta_ref[...], BLOCK_ROWS)
    sel = (a1 | a2).astype(BF16)
    block_ref[cur] = lax.dot_general(sel, h_ref[...], (((0,), (0,)), ((), ())),
                                     preferred_element_type=F32)
    blk = block_ref.at[cur]
    for e in range(N_EXPERTS):
        t = i * N_EXPERTS + e
        first = pl.ds(pl.multiple_of(loc_ref[t], SUBLANES), SUBLANES)
        blk[first, :] = blk[first, :] + tail_ref[e]
        partial = full_ref[t] < n_ref[t]
        lastt = pl.ds(pl.multiple_of(loc_ref[t] + full_ref[t], SUBLANES), SUBLANES)
        tail_ref[e] = jnp.where(partial, blk[lastt, :], 0.0)
    tabs = (n_ref, loc_ref, slot_ref)

    @pl.when(i > 0)
    def _():
        _chunk_copies(i - 1, *tabs, block_ref.at[1 - cur], xs_ref, sem.at[1 - cur], True, _wait)

    _chunk_copies(i, *tabs, block_ref.at[cur], xs_ref, sem.at[cur], True, _start)

    @pl.when(i == last)
    def _():
        _chunk_copies(i, *tabs, block_ref.at[cur], xs_ref, sem.at[cur], True, _wait)
        _zero_copies(*zeros, _wait)


def _dispatch(tabs, full, pad_tabs, h, meta):
    tm = TM_ROUTE
    return pl.pallas_call(
        _dispatch_kernel,
        grid_spec=pltpu.PrefetchScalarGridSpec(
            num_scalar_prefetch=7,
            grid=(TOKENS // tm,),
            in_specs=[
                pl.BlockSpec((tm, D_MODEL), lambda i, *_: (i, 0)),
                pl.BlockSpec((tm, LANES), lambda i, *_: (i, 0)),
            ],
            out_specs=pl.BlockSpec(memory_space=pl.ANY),
            scratch_shapes=[pltpu.VMEM((2, BLOCK_ROWS, D_MODEL), F32),
                            pltpu.VMEM((TM_SLOT, D_MODEL), F32),
                            pltpu.VMEM((N_EXPERTS, SUBLANES, D_MODEL), F32),
                            pltpu.SemaphoreType.DMA((2,)), pltpu.SemaphoreType.DMA],
        ),
        out_shape=jax.ShapeDtypeStruct((N_SLOTS, D_MODEL), F32),
        compiler_params=pltpu.CompilerParams(
            dimension_semantics=("arbitrary",), vmem_limit_bytes=VMEM_LIMIT),
        name="dispatch",
    )(*tabs, full, *pad_tabs, h, meta)


def _experts_kernel(te_ref, tv_ref, ts_ref, xs_ref, wg_ref, wu_ref, wd_ref, o_ref):
    del te_ref, ts_ref
    valid = tv_ref[pl.program_id(0)] != 0

    @pl.when(valid)
    def _():
        o_ref[...] = _swiglu(xs_ref[...].astype(BF16), wg_ref, wu_ref, wd_ref)

    @pl.when(jnp.logical_not(valid))
    def _():
        o_ref[...] = jnp.zeros_like(o_ref)


def _experts(tile_e, tile_v, tile_src, xs, wg, wu, wd):
    tm = TM_SLOT
    return pl.pallas_call(
        _experts_kernel,
        grid_spec=pltpu.PrefetchScalarGridSpec(
            num_scalar_prefetch=3,
            grid=(N_SLOTS // tm,),
            in_specs=[
                pl.BlockSpec((tm, D_MODEL), lambda i, te, tv, ts: (ts[i], 0)),
                pl.BlockSpec((None, D_MODEL, D_FF), lambda i, te, tv, ts: (te[i], 0, 0)),
                pl.BlockSpec((None, D_MODEL, D_FF), lambda i, te, tv, ts: (te[i], 0, 0)),
                pl.BlockSpec((None, D_FF, D_MODEL), lambda i, te, tv, ts: (te[i], 0, 0)),
            ],
            out_specs=pl.BlockSpec((tm, D_MODEL), lambda i, te, tv, ts: (i, 0)),
        ),
        out_shape=jax.ShapeDtypeStruct((N_SLOTS, D_MODEL), F32),
        compiler_params=pltpu.CompilerParams(
            dimension_semantics=("arbitrary",), vmem_limit_bytes=VMEM_LIMIT),
        name="experts",
    )(tile_e, tile_v, tile_src, xs, wg, wu, wd)


def _combine_kernel(n_ref, loc_ref, slot_ref, x_ref, mod_ref, meta_ref, gf_ref, ys_ref, o_ref,
                    block_ref, sem):
    i = pl.program_id(0)
    cur = i % 2
    tabs = (n_ref, loc_ref, slot_ref)

    @pl.when(i == 0)
    def _():
        block_ref[...] = jnp.zeros_like(block_ref)
        _chunk_copies(0, *tabs, block_ref.at[0], ys_ref, sem.at[0], False, _start)

    @pl.when(i + 1 < pl.num_programs(0))
    def _():
        _chunk_copies(i + 1, *tabs, block_ref.at[1 - cur], ys_ref, sem.at[1 - cur], False, _start)

    _chunk_copies(i, *tabs, block_ref.at[cur], ys_ref, sem.at[cur], False, _wait)
    meta = meta_ref[...]
    a1, a2 = _choice_onehots(meta, BLOCK_ROWS)
    blk = block_ref[cur].astype(BF16)
    y = (meta[:, 4:5] * jnp.dot(a1.astype(BF16), blk, preferred_element_type=F32)
         + meta[:, 5:6] * jnp.dot(a2.astype(BF16), blk, preferred_element_type=F32))
    y = x_ref[...] + mod_ref[5:6, :] * y
    o_ref[...] = y * lax.rsqrt(jnp.mean(y * y, axis=-1, keepdims=True) + EPS) * gf_ref[...]


def _combine(tabs, x, mod, meta, g_final, ys):
    tm = TM_ROUTE
    per_b = SEQ // tm
    return pl.pallas_call(
        _combine_kernel,
        grid_spec=pltpu.PrefetchScalarGridSpec(
            num_scalar_prefetch=3,
            grid=(TOKENS // tm,),
            in_specs=[
                pl.BlockSpec((tm, D_MODEL), lambda i, *_: (i, 0)),
                pl.BlockSpec((None, 6, D_MODEL), lambda i, *_: (i // per_b, 0, 0)),
                pl.BlockSpec((tm, LANES), lambda i, *_: (i, 0)),
                pl.BlockSpec((1, D_MODEL), lambda i, *_: (0, 0)),
                pl.BlockSpec(memory_space=pl.ANY),
            ],
            out_specs=pl.BlockSpec((tm, D_MODEL), lambda i, *_: (i, 0)),
            scratch_shapes=[pltpu.VMEM((2, BLOCK_ROWS, D_MODEL), F32), pltpu.SemaphoreType.DMA((2,))],
        ),
        out_shape=jax.ShapeDtypeStruct((TOKENS, D_MODEL), F32),
        compiler_params=pltpu.CompilerParams(
            dimension_semantics=("arbitrary",), vmem_limit_bytes=VMEM_LIMIT),
        name="combine",
    )(*tabs, x, mod, meta, g_final, ys)


def _moe(x, mod, g, w_router, b_router, wg, wu, wd, g_final):
    wr = _pad_lanes(w_router)
    wrh = wr.astype(BF16)
    wrl = (wr - wrh.astype(F32)).astype(BF16)
    stri = (lax.broadcasted_iota(jnp.int32, (TM_ROUTE, TM_ROUTE), 1)
            < lax.broadcasted_iota(jnp.int32, (TM_ROUTE, TM_ROUTE), 0)).astype(BF16)
    upper = (lax.broadcasted_iota(jnp.int32, (LANES, LANES), 0)
             < lax.broadcasted_iota(jnp.int32, (LANES, LANES), 1)).astype(BF16)
    h, meta, tab = _router(x, mod, g, wrh, wrl, _pad_lanes(b_router[None]), stri, upper)
    tab = tab[:, :5, :N_EXPERTS].astype(jnp.int32)
    n_ce, tile_ce, loc_ce, full_ce = tab[:, 0], tab[:, 1], tab[:, 2], tab[:, 3]
    counts = tab[-1, 4]
    padded = (counts + TM_SLOT - 1) // TM_SLOT * TM_SLOT
    ends = jnp.cumsum(padded)
    slot_ce = (ends - padded)[None, :] + tile_ce
    tabs = (n_ce.reshape(-1), loc_ce.reshape(-1), slot_ce.reshape(-1))
    tiled = (counts + SUBLANES - 1) // SUBLANES * SUBLANES
    pad_tabs = (padded - tiled, ends - padded + tiled, ends[-1:])
    tile = jnp.arange(N_SLOTS // TM_SLOT, dtype=jnp.int32)
    tile_e = jnp.minimum(jnp.sum(tile[:, None] * TM_SLOT >= ends[None, :], axis=1), N_EXPERTS - 1)
    tile_v = (tile * TM_SLOT < ends[-1]).astype(jnp.int32)
    tile_src = jnp.minimum(tile, ends[-1] // TM_SLOT - 1)
    xs = _dispatch(tabs, full_ce.reshape(-1), pad_tabs, h, meta)
    ys = _experts(tile_e.astype(jnp.int32), tile_v, tile_src, xs, wg, wu, wd)
    return _combine(tabs, x, mod, meta, g_final, ys)


def _block_diag(w):
    eye = jnp.eye(N_LRU_BLOCKS, dtype=w.dtype)
    return jnp.einsum('nde,nm->ndme', w, eye).reshape(D_LRU, D_LRU)


def _pad_lanes(a):
    return jnp.zeros(a.shape[:-1] + (LANES,), a.dtype).at[..., :a.shape[-1]].set(a)


def kernel(x, c, w_ada, b_ada, g_norm_mix, g_norm_ffn, w_in, w_conv_qk, b_conv_qk, b_gates,
           w_conv_lru, b_conv_lru, w_lru_a, b_lru_a, w_lru_x, b_lru_x, lru_lambda, g_mix_out, w_out,
           w_ff_gate, w_ff_up, w_ff_down, w_router, b_router, w_exp_gate, w_exp_up, w_exp_down, g_final):
    assert DEPTH == 2
    mods = _ada(c, w_ada, b_ada)
    tri = (lax.broadcasted_iota(jnp.int32, (L_MIX, L_MIX), 1)
           <= lax.broadcasted_iota(jnp.int32, (L_MIX, L_MIX), 0)).astype(BF16)
    xt = x.reshape(TOKENS, D_MODEL)
    for l in range(DEPTH):
        mod = mods[l]
        wi = w_in[l]
        n_q = 4 * D_MLSTM
        wcat = jnp.concatenate([wi[:, :n_q], wi[:, n_q + 2 * N_HEADS:],
                                _pad_lanes(wi[:, n_q:n_q + 2 * N_HEADS])], axis=1).astype(BF16)
        wax = jnp.concatenate([_block_diag(w_lru_a[l]), _block_diag(w_lru_x[l])], axis=1).astype(BF16)
        bax = jnp.concatenate([b_lru_a[l], b_lru_x[l]])[None]
        zq, zg, hl = _inproj_lru(xt, mod, g_norm_mix[l][None], wcat,
                                 w_conv_lru[l], b_conv_lru[l][None], wax, bax, lru_lambda[l][None],
                                 g_mix_out[l][None, D_MLSTM:])
        wo = w_out[l].astype(BF16)
        xt = _mlstm_outproj(zq, zg, w_conv_qk[l], b_conv_qk[l][None], _pad_lanes(b_gates[l][None]),
                            g_mix_out[l][None, :D_MLSTM], tri, hl, xt, mod, wo[:D_MLSTM], wo[D_MLSTM:])
        jj = l // 2
        if l % 2 == 0:
            xt = _ffn(xt, mod, g_norm_ffn[l][None], w_ff_gate[jj].astype(BF16),
                      w_ff_up[jj].astype(BF16), w_ff_down[jj].astype(BF16))
        else:
            xt = _moe(xt, mod, g_norm_ffn[l][None], w_router[jj], b_router[jj],
                      w_exp_gate[jj].astype(BF16), w_exp_up[jj].astype(BF16),
                      w_exp_down[jj].astype(BF16), g_final[None])
    return xt.reshape(BATCH, SEQ, D_MODEL)
```

```python
import jax
import jax.numpy as jnp
from jax import lax
from jax.experimental import pallas as pl
from jax.experimental.pallas import tpu as pltpu

F32 = jnp.float32
BF16 = jnp.bfloat16

D_MODEL = 1024
BATCH = 4
SEQ = 8192
TOKENS = BATCH * SEQ
DEPTH = 2
D_MLSTM = 512
N_HEADS = 4
DH = 128
D_LRU = 512
N_LRU_BLOCKS = 8
DB_LRU = 64
CONV_W = 4
LRU_C = 8.0
D_FF = 2816
N_EXPERTS = 8
EPS = 1e-6

LANES = 128
SUBLANES = 8
VMEM_LIMIT = 56 * 1024 * 1024

L_MIX = 256
PROJ_CHUNK = 256
TM_FFN = 512
TM_ROUTE = 256
TM_SLOT = 256
N_SLOTS = 2 * TOKENS + N_EXPERTS * TM_SLOT


def _sigmoid(x):
    return 1.0 / (1.0 + jnp.exp(-x))


def _mod_norm(x, g, scale, shift):
    ms = jnp.mean(x * x, axis=-1, keepdims=True)
    return (x * lax.rsqrt(ms + EPS)) * g * (1.0 + scale) + shift


def _causal_conv(x, ext_ref, w, b):
    L = x.shape[0]
    ext_ref[SUBLANES:, :] = x
    acc = b + w[CONV_W - 1:CONV_W] * x
    for s in range(1, CONV_W):
        acc = acc + w[CONV_W - 1 - s:CONV_W - s] * ext_ref[SUBLANES - s:SUBLANES - s + L, :]
    ext_ref[:SUBLANES, :] = x[L - SUBLANES:, :]
    return acc


def _zero_tile(z):
    bits = lax.bitcast_convert_type(z[-SUBLANES:, -LANES:], jnp.uint32)
    return ((bits >> 16) >> 16).astype(F32)


def _after(v, zeros, lag=0):
    if len(zeros) <= lag:
        return v
    tile = v[:SUBLANES, :LANES]
    while len(zeros) > lag:
        tile = tile + zeros.pop(0)
    head = tile if v.shape[1] == LANES else jnp.concatenate([tile, v[:SUBLANES, LANES:]], axis=1)
    return jnp.concatenate([head, v[SUBLANES:]], axis=0)


def _ada_kernel(c_ref, w_ref, b_ref, o_ref):
    c = c_ref[...]
    cs = c * _sigmoid(c)
    o_ref[...] = jnp.dot(cs, w_ref[...], preferred_element_type=F32) + b_ref[...]


def _ada(c, w_ada, b_ada):
    tn = 1536
    c8 = jnp.zeros((SUBLANES, D_MODEL), F32).at[:BATCH].set(c)
    out = pl.pallas_call(
        _ada_kernel,
        grid=(DEPTH, 6 * D_MODEL // tn),
        in_specs=[
            pl.BlockSpec((SUBLANES, D_MODEL), lambda l, n: (0, 0)),
            pl.BlockSpec((None, D_MODEL, tn), lambda l, n: (l, 0, n)),
            pl.BlockSpec((None, 1, tn), lambda l, n: (l, 0, n)),
        ],
        out_specs=pl.BlockSpec((None, SUBLANES, tn), lambda l, n: (l, 0, n)),
        out_shape=jax.ShapeDtypeStruct((DEPTH, SUBLANES, 6 * D_MODEL), F32),
        compiler_params=pltpu.CompilerParams(
            dimension_semantics=("arbitrary", "arbitrary"), vmem_limit_bytes=VMEM_LIMIT),
        name="ada",
    )(c8, w_ada, b_ada.reshape(DEPTH, 1, 6 * D_MODEL))
    return out[:, :BATCH].reshape(DEPTH, BATCH, 6, D_MODEL)


def _inproj_lru_kernel(x_ref, mod_ref, g_ref, w_ref, wc_ref, bc_ref, wax_ref, bax_ref, lam_ref, gm_ref,
                       zq_ref, zg_ref, hl_ref, zl_ref, ext_ref, hc_ref):
    L = L_MIX
    TIE_LAG = 4
    j = pl.program_id(1)
    slot = j % 2

    @pl.when((pl.program_id(0) == 0) & (j == 0))
    def _():
        zl_ref[...] = jnp.zeros_like(zl_ref)
        ext_ref[...] = jnp.zeros_like(ext_ref)
        hc_ref[...] = jnp.zeros_like(hc_ref)

    keep = j != 1
    ext_ref[:SUBLANES, :] = jnp.where(keep, ext_ref[:SUBLANES, :], 0.0)
    hc_ref[...] = jnp.where(keep, hc_ref[...], 0.0)
    xc = _causal_conv(zl_ref[1 - slot, :, :D_LRU].astype(F32), ext_ref, wc_ref[...], bc_ref[...])

    mod = mod_ref[...]
    h = _mod_norm(x_ref[...], g_ref[...], mod[1:2], mod[0:1]).astype(BF16)
    n_q, n_l = 4 * D_MLSTM, 2 * D_LRU
    todo = list(range(0, n_q + n_l + LANES, PROJ_CHUNK))
    zeros = []

    def project(n):
        for lo in todo[:n]:
            hi = min(lo + PROJ_CHUNK, n_q + n_l + LANES)
            z = jnp.dot(h, w_ref[:, lo:hi], preferred_element_type=F32)
            zeros.append(_zero_tile(z))
            if hi <= n_q:
                zq_ref[:, lo:hi] = z.astype(BF16)
            elif hi <= n_q + n_l:
                zl_ref[slot, :, lo - n_q:hi - n_q] = z.astype(BF16)
            else:
                zg_ref[...] = z
        del todo[:n]

    project(2)
    xc = _after(xc, zeros, TIE_LAG)
    gates = jnp.dot(xc.astype(BF16), wax_ref[...], preferred_element_type=F32) + bax_ref[...]
    r = _sigmoid(gates[:, :D_LRU])
    i = _sigmoid(gates[:, D_LRU:])
    project(2)
    r = _after(r, zeros, TIE_LAG)
    lam = lam_ref[...]
    sp = jnp.maximum(-lam, 0.0) + jnp.log1p(jnp.exp(-jnp.abs(lam)))
    log_a = -LRU_C * r * sp
    a = jnp.exp(log_a)
    y2 = 2.0 * log_a
    series = -y2 * (1.0 + y2 * (0.5 + y2 * (1.0 / 6.0 + y2 * (1.0 / 24.0))))
    u = jnp.sqrt(jnp.where(y2 > -0.01, series, 1.0 - a * a)) * (i * xc)
    project(2)
    u = _after(u, zeros, TIE_LAG)

    r8 = lax.broadcasted_iota(jnp.int32, (L, D_LRU), 0) & (SUBLANES - 1)
    for s in (1, 2, 4):
        a_sh = pltpu.roll(a, s, 0)
        u_sh = pltpu.roll(u, s, 0)
        valid = r8 >= s
        u = jnp.where(valid, a * u_sh + u, u)
        a = jnp.where(valid, a * a_sh, a)
        project(2)
        u = _after(u, zeros, TIE_LAG)
    gl = zl_ref[1 - slot, :, D_LRU:].astype(F32)
    gelu = 0.5 * gl * (1.0 + jnp.tanh(0.7978845608028654 * (gl + 0.044715 * gl * gl * gl)))
    project(len(todo))
    hc = hc_ref[...]
    enter = []
    for gi in range(L // SUBLANES):
        enter.append(hc)
        last = gi * SUBLANES + SUBLANES - 1
        hc = u[last:last + 1] + a[last:last + 1] * hc
    hc_ref[...] = hc
    hseq = jnp.concatenate(
        [u[gi * SUBLANES:(gi + 1) * SUBLANES] + a[gi * SUBLANES:(gi + 1) * SUBLANES] * enter[gi]
         for gi in range(L // SUBLANES)], axis=0)
    hl = hseq * gelu
    hl = hl * lax.rsqrt(jnp.mean(hl * hl, axis=-1, keepdims=True) + EPS) * gm_ref[...]
    hl_ref[...] = hl.astype(BF16)


def _inproj_lru(x, mod, g, w, w_conv, b_conv, wax, bax, lam, g_mix):
    L = L_MIX
    nc = SEQ // L
    proj = lambda b, j: (b * nc + jnp.minimum(j, nc - 1), 0)
    mixed = lambda b, j: (b * nc + jnp.maximum(j - 1, 0), 0)
    const = lambda shape: pl.BlockSpec(shape, lambda b, j: (0, 0))
    return pl.pallas_call(
        _inproj_lru_kernel,
        grid=(BATCH, nc + 1),
        in_specs=[
            pl.BlockSpec((L, D_MODEL), proj),
            pl.BlockSpec((None, 6, D_MODEL), lambda b, j: (b, 0, 0)),
            const((1, D_MODEL)), const(w.shape),
            const((CONV_W, D_LRU)), const((1, D_LRU)), const((D_LRU, 2 * D_LRU)),
            const((1, 2 * D_LRU)), const((1, D_LRU)), const((1, D_LRU)),
        ],
        out_specs=[
            pl.BlockSpec((L, 4 * D_MLSTM), proj),
            pl.BlockSpec((L, LANES), proj),
            pl.BlockSpec((L, D_LRU), mixed),
        ],
        out_shape=[
            jax.ShapeDtypeStruct((TOKENS, 4 * D_MLSTM), BF16),
            jax.ShapeDtypeStruct((TOKENS, LANES), F32),
            jax.ShapeDtypeStruct((TOKENS, D_LRU), BF16),
        ],
        scratch_shapes=[
            pltpu.VMEM((2, L, 2 * D_LRU), BF16),
            pltpu.VMEM((SUBLANES + L, D_LRU), F32),
            pltpu.VMEM((1, D_LRU), F32),
        ],
        compiler_params=pltpu.CompilerParams(
            dimension_semantics=("arbitrary", "arbitrary"), vmem_limit_bytes=VMEM_LIMIT),
        name="inproj_rglru",
    )(x, mod, g, w, w_conv, b_conv, wax, bax, lam, g_mix)


def _mlstm_outproj_kernel(zq_ref, zg_ref, wc_ref, bc_ref, bg_ref, gm_ref, tri_ref,
                          hl_ref, x_ref, mod_ref, wm_ref, wl_ref, o_ref,
                          ct_ref, m_ref, ext_ref, hm_ref):
    L = L_MIX
    j = pl.program_id(1)
    slot = j % 2

    @pl.when((pl.program_id(0) == 0) & (j == 0))
    def _():
        hm_ref[...] = jnp.zeros_like(hm_ref)

    @pl.when(j == 0)
    def _():
        ct_ref[...] = jnp.zeros_like(ct_ref)
        m_ref[...] = jnp.zeros_like(m_ref)
        ext_ref[:SUBLANES, :] = jnp.zeros((SUBLANES, ext_ref.shape[1]), F32)

    todo = list(range(0, D_MODEL, PROJ_CHUNK))
    zeros = []

    def project(n):
        for lo in todo[:n]:
            cs = slice(lo, lo + PROJ_CHUNK)
            y = (jnp.dot(hm_ref[1 - slot], wm_ref[:, cs], preferred_element_type=F32)
                 + jnp.dot(hl_ref[...], wl_ref[:, cs], preferred_element_type=F32))
            zeros.append(_zero_tile(y))
            o_ref[:, cs] = x_ref[:, cs] + mod_ref[2:3, cs] * y
        del todo[:n]

    project(1)
    qk = _causal_conv(zq_ref[:, :2 * D_MLSTM].astype(F32), ext_ref, wc_ref[...], bc_ref[...])
    qk = _after(qk, zeros)
    project(1)
    qk = qk * _sigmoid(qk)
    q = _after(qk[:, :D_MLSTM], zeros).astype(BF16)
    k = (qk[:, D_MLSTM:] * (DH ** -0.5)).astype(BF16)
    project(1)

    g = zg_ref[...] + bg_ref[...]
    lf = jnp.minimum(g, 0.0) - jnp.log1p(jnp.exp(-jnp.abs(g)))
    tri = tri_ref[...]
    hi = lf.astype(BF16)
    r1 = lf - hi.astype(F32)
    mid = r1.astype(BF16)
    lo = (r1 - mid.astype(F32)).astype(BF16)
    bcum = (jnp.dot(tri, hi, preferred_element_type=F32)
            + jnp.dot(tri, mid, preferred_element_type=F32)
            + jnp.dot(tri, lo, preferred_element_type=F32))
    lane = lax.broadcasted_iota(jnp.int32, (L, LANES), 1)
    cols = _after(jnp.where(lane < N_HEADS, g, bcum), zeros)
    project(len(todo))
    rows = jnp.transpose(cols)

    t_idx = lax.broadcasted_iota(jnp.int32, (L, L), 0)
    s_idx = lax.broadcasted_iota(jnp.int32, (L, L), 1)
    causal = s_idx <= t_idx
    ones_col = (lax.broadcasted_iota(jnp.int32, (L, DH), 1) == 0).astype(BF16)
    gm = gm_ref[...]

    heads = range(N_HEADS)
    sls = [slice(h * DH, (h + 1) * DH) for h in heads]
    ic = [cols[:, h:h + 1] for h in heads]
    bc = [cols[:, N_HEADS + h:N_HEADS + h + 1] for h in heads]
    ir = [rows[h:h + 1, :] for h in heads]
    br = [rows[N_HEADS + h:N_HEADS + h + 1, :] for h in heads]
    b_last = [br[h][:, L - 1:L] for h in heads]
    m_prev = [m_ref[h:h + 1, 0:1] for h in heads]
    qh = [q[:, sls[h]] for h in heads]
    kh = [k[:, sls[h]] for h in heads]
    vaug = [jnp.concatenate([zq_ref[:, 2 * D_MLSTM + h * DH:2 * D_MLSTM + (h + 1) * DH], ones_col], axis=1)
            for h in heads]
    ct = [ct_ref[h] for h in heads]

    dm = [jnp.where(causal, bc[h] - br[h] + ir[h], -jnp.inf) for h in heads]
    dm[0] = _after(dm[0], zeros)
    inter_log = [bc[h] + m_prev[h] for h in heads]
    sc = [lax.dot_general(qh[h], kh[h], (((1,), (1,)), ((), ())), preferred_element_type=F32) for h in heads]
    inter = [jnp.dot(qh[h], ct[h].astype(BF16), preferred_element_type=F32) for h in heads]
    m_t = [jnp.maximum(inter_log[h], jnp.max(dm[h], axis=1, keepdims=True)) for h in heads]

    w_end = [b_last[h] - bc[h] + ic[h] for h in heads]
    m_loc = [jnp.max(w_end[h], axis=0, keepdims=True) for h in heads]
    m_new = [jnp.maximum(b_last[h] + m_prev[h], m_loc[h]) for h in heads]
    ev = [(jnp.exp(w_end[h] - m_loc[h]) * vaug[h].astype(F32)).astype(BF16) for h in heads]
    c_loc = [lax.dot_general(kh[h], ev[h], (((0,), (0,)), ((), ())), preferred_element_type=F32)
             for h in heads]

    p = [(sc[h] * jnp.exp(dm[h] - m_t[h])).astype(BF16) for h in heads]
    out = [jnp.dot(p[h], vaug[h], preferred_element_type=F32) + jnp.exp(inter_log[h] - m_t[h]) * inter[h]
           for h in heads]
    for h in heads:
        ct_ref[h] = (jnp.exp(b_last[h] + m_prev[h] - m_new[h]) * ct[h]
                     + jnp.exp(m_loc[h] - m_new[h]) * c_loc[h])
        m_ref[h:h + 1, :] = jnp.broadcast_to(m_new[h], (1, LANES))
    hh = [out[h][:, :DH] / jnp.maximum(jnp.abs(out[h][:, DH:DH + 1]), jnp.exp(-m_t[h])) for h in heads]
    hn = [hh[h] * lax.rsqrt(jnp.mean(hh[h] * hh[h], axis=-1, keepdims=True) + EPS) * gm[:, sls[h]]
          for h in heads]
    for h in heads:
        og = zq_ref[:, 3 * D_MLSTM + h * DH:3 * D_MLSTM + (h + 1) * DH].astype(F32)
        hm_ref[slot, :, sls[h]] = (hn[h] * _sigmoid(og)).astype(BF16)


def _mlstm_outproj(zq, zg, w_conv, b_conv, b_gates, g_mix, tri, hl, x, mod, wm, wl):
    L = L_MIX
    nc = SEQ // L
    mixed = lambda b, j: (b * nc + jnp.minimum(j, nc - 1), 0)
    projected = lambda b, j: (b * nc + jnp.maximum(j - 1, 0), 0)
    const = lambda shape: pl.BlockSpec(shape, lambda b, j: (0, 0))
    return pl.pallas_call(
        _mlstm_outproj_kernel,
        grid=(BATCH, nc + 1),
        in_specs=[
            pl.BlockSpec((L, 4 * D_MLSTM), mixed),
            pl.BlockSpec((L, LANES), mixed),
            const((CONV_W, 2 * D_MLSTM)), const((1, 2 * D_MLSTM)), const((1, LANES)), const((1, D_MLSTM)),
            const((L, L)),
            pl.BlockSpec((L, D_LRU), projected),
            pl.BlockSpec((L, D_MODEL), projected),
            pl.BlockSpec((None, 6, D_MODEL), lambda b, j: (b, 0, 0)),
            const((D_MLSTM, D_MODEL)), const((D_LRU, D_MODEL)),
        ],
        out_specs=pl.BlockSpec((L, D_MODEL), projected),
        out_shape=jax.ShapeDtypeStruct((TOKENS, D_MODEL), F32),
        scratch_shapes=[
            pltpu.VMEM((N_HEADS, DH, 2 * DH), F32),
            pltpu.VMEM((SUBLANES, LANES), F32),
            pltpu.VMEM((SUBLANES + L, 2 * D_MLSTM), F32),
            pltpu.VMEM((2, L, D_MLSTM), BF16),
        ],
        compiler_params=pltpu.CompilerParams(
            dimension_semantics=("arbitrary", "arbitrary"), vmem_limit_bytes=VMEM_LIMIT),
        name="mlstm_outproj",
    )(zq, zg, w_conv, b_conv, b_gates, g_mix, tri, hl, x, mod, wm, wl)


def _swiglu(hb, wg_ref, wu_ref, wd_ref):
    g = jnp.dot(hb, wg_ref[...], preferred_element_type=F32)
    u = jnp.dot(hb, wu_ref[...], preferred_element_type=F32)
    act = g * _sigmoid(g) * u
    return jnp.dot(act.astype(BF16), wd_ref[...], preferred_element_type=F32)


def _ffn_kernel(x_ref, mod_ref, g_ref, wg_ref, wu_ref, wd_ref, o_ref):
    x = x_ref[...]
    hb = _mod_norm(x, g_ref[...], mod_ref[4:5, :], mod_ref[3:4, :]).astype(BF16)
    o_ref[...] = x + mod_ref[5:6, :] * _swiglu(hb, wg_ref, wu_ref, wd_ref)


def _ffn(x, mod, g, wg, wu, wd):
    tm = TM_FFN
    per_b = SEQ // tm
    return pl.pallas_call(
        _ffn_kernel,
        grid=(TOKENS // tm,),
        in_specs=[
            pl.BlockSpec((tm, D_MODEL), lambda i: (i, 0)),
            pl.BlockSpec((None, 6, D_MODEL), lambda i: (i // per_b, 0, 0)),
            pl.BlockSpec((1, D_MODEL), lambda i: (0, 0)),
            pl.BlockSpec((D_MODEL, D_FF), lambda i: (0, 0)),
            pl.BlockSpec((D_MODEL, D_FF), lambda i: (0, 0)),
            pl.BlockSpec((D_FF, D_MODEL), lambda i: (0, 0)),
        ],
        out_specs=pl.BlockSpec((tm, D_MODEL), lambda i: (i, 0)),
        out_shape=jax.ShapeDtypeStruct((TOKENS, D_MODEL), F32),
        compiler_params=pltpu.CompilerParams(
            dimension_semantics=("arbitrary",), vmem_limit_bytes=VMEM_LIMIT),
        name="ffn",
    )(x, mod, g, wg, wu, wd)


def _router_kernel(x_ref, mod_ref, g_ref, wrh_ref, wrl_ref, br_ref, stri_ref, upper_ref,
                   h_ref, meta_ref, tab_ref, carry_ref, lg_ref):
    i = pl.program_id(0)
    slot = i % 2

    @pl.when(i == 0)
    def _():
        carry_ref[...] = jnp.zeros_like(carry_ref)
        lg_ref[...] = jnp.zeros_like(lg_ref)

    tm = x_ref.shape[0]
    lane = lax.broadcasted_iota(jnp.int32, (tm, LANES), 1)
    logits = jnp.where(lane < N_EXPERTS, lg_ref[1 - slot], -jnp.inf)
    m1 = jnp.max(logits, axis=-1, keepdims=True)

    h = _mod_norm(x_ref[...], g_ref[...], mod_ref[4:5, :], mod_ref[3:4, :])
    hb = h.astype(BF16)
    h_ref[...] = hb
    i1 = jnp.min(jnp.where(logits == m1, lane, LANES), axis=-1, keepdims=True)

    hlo = (h - hb.astype(F32)).astype(BF16)
    wrh = wrh_ref[...]
    new_logits = jnp.dot(hb, wrh, preferred_element_type=F32)
    rest = jnp.where(lane == i1, -jnp.inf, logits)
    m2 = jnp.max(rest, axis=-1, keepdims=True)
    new_logits = new_logits + jnp.dot(hlo, wrh, preferred_element_type=F32)
    i2 = jnp.min(jnp.where(rest == m2, lane, LANES), axis=-1, keepdims=True)
    new_logits = new_logits + jnp.dot(hb, wrl_ref[...], preferred_element_type=F32)
    lg_ref[slot] = new_logits + br_ref[...]

    e2 = jnp.exp(m2 - m1)
    p1 = 1.0 / (1.0 + e2)
    p2 = e2 * p1
    ind = jnp.where((lane == i1) | (lane == i2), 1.0, 0.0)
    carry = carry_ref[0:1, :]
    aligned = jnp.floor(carry * (1.0 / SUBLANES)) * SUBLANES
    lead = carry - aligned
    filled = lead + jnp.sum(ind, axis=0, keepdims=True)
    full = jnp.floor(filled * (1.0 / SUBLANES)) * SUBLANES
    n_e = jnp.floor((filled + (SUBLANES - 1)) * (1.0 / SUBLANES)) * SUBLANES
    lrank = jnp.dot(stri_ref[...], ind.astype(BF16), preferred_element_type=F32)
    loff = jnp.dot(jnp.broadcast_to(n_e, (SUBLANES, LANES)).astype(BF16), upper_ref[...],
                   preferred_element_type=F32)[0:1, :]
    row = lrank + loff + lead
    d1 = jnp.sum(jnp.where(lane == i1, row, 0.0), axis=-1, keepdims=True)
    d2 = jnp.sum(jnp.where(lane == i2, row, 0.0), axis=-1, keepdims=True)
    vals = (i1.astype(F32), i2.astype(F32), d1, d2, p1, p2)
    meta = jnp.zeros((tm, LANES), F32)
    for n, v in enumerate(vals):
        meta = jnp.where(lane == n, v, meta)
    meta_ref[...] = meta
    srow = lax.broadcasted_iota(jnp.int32, (SUBLANES, LANES), 0)
    carry = carry + jnp.where(i > 0, filled - lead, 0.0)
    tab = jnp.zeros((SUBLANES, LANES), F32)
    for n, v in enumerate((n_e, aligned, loff, full, carry)):
        tab = jnp.where(srow == n, v, tab)
    tab_ref[...] = tab
    carry_ref[0:1, :] = carry


def _router(x, mod, g, wrh, wrl, br, stri, upper):
    tm = TM_ROUTE
    n = TOKENS // tm
    per_b = SEQ // tm
    row = lambda shape: pl.BlockSpec(shape, lambda i: (0, 0))
    normed = lambda i: jnp.minimum(i, n - 1)
    ranked = lambda i: jnp.maximum(i - 1, 0)
    return pl.pallas_call(
        _router_kernel,
        grid=(n + 1,),
        in_specs=[
            pl.BlockSpec((tm, D_MODEL), lambda i: (normed(i), 0)),
            pl.BlockSpec((None, 6, D_MODEL), lambda i: (normed(i) // per_b, 0, 0)),
            row((1, D_MODEL)), row((D_MODEL, LANES)), row((D_MODEL, LANES)), row((1, LANES)),
            row((tm, tm)), row((LANES, LANES)),
        ],
        out_specs=[
            pl.BlockSpec((tm, D_MODEL), lambda i: (normed(i), 0)),
            pl.BlockSpec((tm, LANES), lambda i: (ranked(i), 0)),
            pl.BlockSpec((None, SUBLANES, LANES), lambda i: (ranked(i), 0, 0)),
        ],
        out_shape=[
            jax.ShapeDtypeStruct((TOKENS, D_MODEL), BF16),
            jax.ShapeDtypeStruct((TOKENS, LANES), F32),
            jax.ShapeDtypeStruct((n, SUBLANES, LANES), F32),
        ],
        scratch_shapes=[pltpu.VMEM((SUBLANES, LANES), F32), pltpu.VMEM((2, tm, LANES), F32)],
        compiler_params=pltpu.CompilerParams(
            dimension_semantics=("arbitrary",), vmem_limit_bytes=VMEM_LIMIT),
        name="router",
    )(x, mod, g, wrh, wrl, br, stri, upper)


BLOCK_ROWS = 2 * TM_ROUTE + LANES
RANGE_PIECES = tuple(1 << b for b in range(TM_ROUTE.bit_length() - 1, 2, -1))
assert TM_SLOT <= 2 * RANGE_PIECES[0]


def _range_pieces(n, loc, slot, block_ref, slots_ref, sem, to_slots, visit):
    loc = pl.multiple_of(loc, SUBLANES)
    slot = pl.multiple_of(slot, SUBLANES)
    for piece in RANGE_PIECES:
        has = (n & piece) != 0
        src = block_ref.at[pl.ds(loc, piece)]
        dst = slots_ref.at[pl.ds(slot, piece)]
        if not to_slots:
            src, dst = dst, src

        @pl.when(has)
        def _():
            visit(pltpu.make_async_copy(src, dst, sem))

        step = jnp.where(has, piece, 0)
        loc = pl.multiple_of(loc + step, SUBLANES)
        slot = pl.multiple_of(slot + step, SUBLANES)


def _chunk_copies(chunk, n_ref, loc_ref, slot_ref, block_ref, slots_ref, sem, to_slots, visit):
    for e in range(N_EXPERTS):
        i = chunk * N_EXPERTS + e
        _range_pieces(n_ref[i], loc_ref[i], slot_ref[i], block_ref, slots_ref, sem, to_slots, visit)


def _start(cp):
    cp.start()


def _wait(cp):
    cp.wait()


def _choice_onehots(meta, width):
    col = lax.broadcasted_iota(jnp.int32, (meta.shape[0], width), 1)
    d1 = meta[:, 2:3].astype(jnp.int32)
    d2 = meta[:, 3:4].astype(jnp.int32)
    return col == d1, col == d2


def _zero_copies(padn_ref, pads_ref, used_ref, zero_ref, xs_ref, sem, visit):
    for e in range(N_EXPERTS):
        _range_pieces(padn_ref[e], 0, pads_ref[e], zero_ref, xs_ref, sem, True, visit)
    for t in range(2 * TOKENS // TM_SLOT, N_SLOTS // TM_SLOT):
        @pl.when(t * TM_SLOT >= used_ref[0])
        def _():
            visit(pltpu.make_async_copy(zero_ref, xs_ref.at[pl.ds(t * TM_SLOT, TM_SLOT)], sem))


def _dispatch_kernel(n_ref, loc_ref, slot_ref, full_ref, padn_ref, pads_ref, used_ref, h_ref, meta_ref,
                     xs_ref, block_ref, zero_ref, tail_ref, sem, zsem):
    i = pl.program_id(0)
    last = pl.num_programs(0) - 1
    cur = i % 2
    zeros = (padn_ref, pads_ref, used_ref, zero_ref, xs_ref, zsem)

    @pl.when(i == 0)
    def _():
        zero_ref[...] = jnp.zeros_like(zero_ref)
        tail_ref[...] = jnp.zeros_like(tail_ref)
        _zero_copies(*zeros, _start)

    rows_t = jnp.transpose(meta_ref[...])
    brow = lax.broadcasted_iota(jnp.int32, (BLOCK_ROWS, rows_t.shape[1]), 0)
    sel = ((brow == rows_t[2:3, :].astype(jnp.int32)) | (brow == rows_t[3:4, :].astype(jnp.int32))).astype(BF16)
    block_ref[cur] = jnp.dot(sel, h_ref[...], preferred_element_type=F32)
    blk = block_ref.at[cur]
    for e in range(N_EXPERTS):
        t = i * N_EXPERTS + e
        first = pl.ds(pl.multiple_of(loc_ref[t], SUBLANES), SUBLANES)
        blk[first, :] = blk[first, :] + tail_ref[e]
        partial = full_ref[t] < n_ref[t]
        lastt = pl.ds(pl.multiple_of(loc_ref[t] + full_ref[t], SUBLANES), SUBLANES)
        tail_ref[e] = jnp.where(partial, blk[lastt, :], 0.0)
    tabs = (n_ref, loc_ref, slot_ref)

    @pl.when(i > 0)
    def _():
        _chunk_copies(i - 1, *tabs, block_ref.at[1 - cur], xs_ref, sem.at[1 - cur], True, _wait)

    _chunk_copies(i, *tabs, block_ref.at[cur], xs_ref, sem.at[cur], True, _start)

    @pl.when(i == last)
    def _():
        _chunk_copies(i, *tabs, block_ref.at[cur], xs_ref, sem.at[cur], True, _wait)
        _zero_copies(*zeros, _wait)


def _dispatch(tabs, full, pad_tabs, h, meta):
    tm = TM_ROUTE
    return pl.pallas_call(
        _dispatch_kernel,
        grid_spec=pltpu.PrefetchScalarGridSpec(
            num_scalar_prefetch=7,
            grid=(TOKENS // tm,),
            in_specs=[
                pl.BlockSpec((tm, D_MODEL), lambda i, *_: (i, 0)),
                pl.BlockSpec((tm, LANES), lambda i, *_: (i, 0)),
            ],
            out_specs=pl.BlockSpec(memory_space=pl.ANY),
            scratch_shapes=[pltpu.VMEM((2, BLOCK_ROWS, D_MODEL), F32),
                            pltpu.VMEM((TM_SLOT, D_MODEL), F32),
                            pltpu.VMEM((N_EXPERTS, SUBLANES, D_MODEL), F32),
                            pltpu.SemaphoreType.DMA((2,)), pltpu.SemaphoreType.DMA],
        ),
        out_shape=jax.ShapeDtypeStruct((N_SLOTS, D_MODEL), F32),
        compiler_params=pltpu.CompilerParams(
            dimension_semantics=("arbitrary",), vmem_limit_bytes=VMEM_LIMIT),
        name="dispatch",
    )(*tabs, full, *pad_tabs, h, meta)


def _experts_kernel(te_ref, tv_ref, ts_ref, xs_ref, wg_ref, wu_ref, wd_ref, o_ref):
    del te_ref, ts_ref
    valid = tv_ref[pl.program_id(0)] != 0

    @pl.when(valid)
    def _():
        o_ref[...] = _swiglu(xs_ref[...].astype(BF16), wg_ref, wu_ref, wd_ref)

    @pl.when(jnp.logical_not(valid))
    def _():
        o_ref[...] = jnp.zeros_like(o_ref)


def _experts(tile_e, tile_v, tile_src, xs, wg, wu, wd):
    tm = TM_SLOT
    return pl.pallas_call(
        _experts_kernel,
        grid_spec=pltpu.PrefetchScalarGridSpec(
            num_scalar_prefetch=3,
            grid=(N_SLOTS // tm,),
            in_specs=[
                pl.BlockSpec((tm, D_MODEL), lambda i, te, tv, ts: (ts[i], 0)),
                pl.BlockSpec((None, D_MODEL, D_FF), lambda i, te, tv, ts: (te[i], 0, 0)),
                pl.BlockSpec((None, D_MODEL, D_FF), lambda i, te, tv, ts: (te[i], 0, 0)),
                pl.BlockSpec((None, D_FF, D_MODEL), lambda i, te, tv, ts: (te[i], 0, 0)),
            ],
            out_specs=pl.BlockSpec((tm, D_MODEL), lambda i, te, tv, ts: (i, 0)),
        ),
        out_shape=jax.ShapeDtypeStruct((N_SLOTS, D_MODEL), F32),
        compiler_params=pltpu.CompilerParams(
            dimension_semantics=("arbitrary",), vmem_limit_bytes=VMEM_LIMIT),
        name="experts",
    )(tile_e, tile_v, tile_src, xs, wg, wu, wd)


def _combine_kernel(n_ref, loc_ref, slot_ref, x_ref, mod_ref, meta_ref, gf_ref, ys_ref, o_ref,
                    block_ref, sem):
    i = pl.program_id(0)
    cur = i % 2
    tabs = (n_ref, loc_ref, slot_ref)

    @pl.when(i == 0)
    def _():
        block_ref[...] = jnp.zeros_like(block_ref)
        _chunk_copies(0, *tabs, block_ref.at[0], ys_ref, sem.at[0], False, _start)

    @pl.when(i + 1 < pl.num_programs(0))
    def _():
        _chunk_copies(i + 1, *tabs, block_ref.at[1 - cur], ys_ref, sem.at[1 - cur], False, _start)

    _chunk_copies(i, *tabs, block_ref.at[cur], ys_ref, sem.at[cur], False, _wait)
    meta = meta_ref[...]
    a1, a2 = _choice_onehots(meta, BLOCK_ROWS)
    blk = block_ref[cur].astype(BF16)
    y = (meta[:, 4:5] * jnp.dot(a1.astype(BF16), blk, preferred_element_type=F32)
         + meta[:, 5:6] * jnp.dot(a2.astype(BF16), blk, preferred_element_type=F32))
    y = x_ref[...] + mod_ref[5:6, :] * y
    o_ref[...] = y * lax.rsqrt(jnp.mean(y * y, axis=-1, keepdims=True) + EPS) * gf_ref[...]


def _combine(tabs, x, mod, meta, g_final, ys):
    tm = TM_ROUTE
    per_b = SEQ // tm
    return pl.pallas_call(
        _combine_kernel,
        grid_spec=pltpu.PrefetchScalarGridSpec(
            num_scalar_prefetch=3,
            grid=(TOKENS // tm,),
            in_specs=[
                pl.BlockSpec((tm, D_MODEL), lambda i, *_: (i, 0)),
                pl.BlockSpec((None, 6, D_MODEL), lambda i, *_: (i // per_b, 0, 0)),
                pl.BlockSpec((tm, LANES), lambda i, *_: (i, 0)),
                pl.BlockSpec((1, D_MODEL), lambda i, *_: (0, 0)),
                pl.BlockSpec(memory_space=pl.ANY),
            ],
            out_specs=pl.BlockSpec((tm, D_MODEL), lambda i, *_: (i, 0)),
            scratch_shapes=[pltpu.VMEM((2, BLOCK_ROWS, D_MODEL), F32), pltpu.SemaphoreType.DMA((2,))],
        ),
        out_shape=jax.ShapeDtypeStruct((TOKENS, D_MODEL), F32),
        compiler_params=pltpu.CompilerParams(
            dimension_semantics=("arbitrary",), vmem_limit_bytes=VMEM_LIMIT),
        name="combine",
    )(*tabs, x, mod, meta, g_final, ys)


def _moe(x, mod, g, w_router, b_router, wg, wu, wd, g_final):
    wr = _pad_lanes(w_router)
    wrh = wr.astype(BF16)
    wrl = (wr - wrh.astype(F32)).astype(BF16)
    stri = (lax.broadcasted_iota(jnp.int32, (TM_ROUTE, TM_ROUTE), 1)
            < lax.broadcasted_iota(jnp.int32, (TM_ROUTE, TM_ROUTE), 0)).astype(BF16)
    upper = (lax.broadcasted_iota(jnp.int32, (LANES, LANES), 0)
             < lax.broadcasted_iota(jnp.int32, (LANES, LANES), 1)).astype(BF16)
    h, meta, tab = _router(x, mod, g, wrh, wrl, _pad_lanes(b_router[None]), stri, upper)
    tab = tab[:, :5, :N_EXPERTS].astype(jnp.int32)
    n_ce, tile_ce, loc_ce, full_ce = tab[:, 0], tab[:, 1], tab[:, 2], tab[:, 3]
    counts = tab[-1, 4]
    padded = (counts + TM_SLOT - 1) // TM_SLOT * TM_SLOT
    ends = jnp.cumsum(padded)
    slot_ce = (ends - padded)[None, :] + tile_ce
    tabs = (n_ce.reshape(-1), loc_ce.reshape(-1), slot_ce.reshape(-1))
    tiled = (counts + SUBLANES - 1) // SUBLANES * SUBLANES
    pad_tabs = (padded - tiled, ends - padded + tiled, ends[-1:])
    tile = jnp.arange(N_SLOTS // TM_SLOT, dtype=jnp.int32)
    tile_e = jnp.minimum(jnp.sum(tile[:, None] * TM_SLOT >= ends[None, :], axis=1), N_EXPERTS - 1)
    tile_v = (tile * TM_SLOT < ends[-1]).astype(jnp.int32)
    tile_src = jnp.minimum(tile, ends[-1] // TM_SLOT - 1)
    xs = _dispatch(tabs, full_ce.reshape(-1), pad_tabs, h, meta)
    ys = _experts(tile_e.astype(jnp.int32), tile_v, tile_src, xs, wg, wu, wd)
    return _combine(tabs, x, mod, meta, g_final, ys)


def _block_diag(w):
    eye = jnp.eye(N_LRU_BLOCKS, dtype=w.dtype)
    return jnp.einsum('nde,nm->ndme', w, eye).reshape(D_LRU, D_LRU)


def _pad_lanes(a):
    return jnp.zeros(a.shape[:-1] + (LANES,), a.dtype).at[..., :a.shape[-1]].set(a)


def kernel(x, c, w_ada, b_ada, g_norm_mix, g_norm_ffn, w_in, w_conv_qk, b_conv_qk, b_gates,
           w_conv_lru, b_conv_lru, w_lru_a, b_lru_a, w_lru_x, b_lru_x, lru_lambda, g_mix_out, w_out,
           w_ff_gate, w_ff_up, w_ff_down, w_router, b_router, w_exp_gate, w_exp_up, w_exp_down, g_final):
    assert DEPTH == 2
    mods = _ada(c, w_ada, b_ada)
    tri = (lax.broadcasted_iota(jnp.int32, (L_MIX, L_MIX), 1)
           <= lax.broadcasted_iota(jnp.int32, (L_MIX, L_MIX), 0)).astype(BF16)
    xt = x.reshape(TOKENS, D_MODEL)
    for l in range(DEPTH):
        mod = mods[l]
        wi = w_in[l]
        n_q = 4 * D_MLSTM
        wcat = jnp.concatenate([wi[:, :n_q], wi[:, n_q + 2 * N_HEADS:],
                                _pad_lanes(wi[:, n_q:n_q + 2 * N_HEADS])], axis=1).astype(BF16)
        wax = jnp.concatenate([_block_diag(w_lru_a[l]), _block_diag(w_lru_x[l])], axis=1).astype(BF16)
        bax = jnp.concatenate([b_lru_a[l], b_lru_x[l]])[None]
        zq, zg, hl = _inproj_lru(xt, mod, g_norm_mix[l][None], wcat,
                                 w_conv_lru[l], b_conv_lru[l][None], wax, bax, lru_lambda[l][None],
                                 g_mix_out[l][None, D_MLSTM:])
        wo = w_out[l].astype(BF16)
        xt = _mlstm_outproj(zq, zg, w_conv_qk[l], b_conv_qk[l][None], _pad_lanes(b_gates[l][None]),
                            g_mix_out[l][None, :D_MLSTM], tri, hl, xt, mod, wo[:D_MLSTM], wo[D_MLSTM:])
        jj = l // 2
        if l % 2 == 0:
            xt = _ffn(xt, mod, g_norm_ffn[l][None], w_ff_gate[jj].astype(BF16),
                      w_ff_up[jj].astype(BF16), w_ff_down[jj].astype(BF16))
        else:
            xt = _moe(xt, mod, g_norm_ffn[l][None], w_router[jj], b_router[jj],
                      w_exp_gate[jj].astype(BF16), w_exp_up[jj].astype(BF16),
                      w_exp_down[jj].astype(BF16), g_final[None])
    return xt.reshape(BATCH, SEQ, D_MODEL)
```

```python
import jax
import jax.numpy as jnp
from jax import lax
from jax.experimental import pallas as pl
from jax.experimental.pallas import tpu as pltpu

F32 = jnp.float32
BF16 = jnp.bfloat16

D_MODEL = 1024
BATCH = 4
SEQ = 8192
TOKENS = BATCH * SEQ
DEPTH = 2
D_MLSTM = 512
N_HEADS = 4
DH = 128
D_LRU = 512
N_LRU_BLOCKS = 8
DB_LRU = 64
CONV_W = 4
LRU_C = 8.0
D_FF = 2816
N_EXPERTS = 8
EPS = 1e-6

LANES = 128
SUBLANES = 8
VMEM_LIMIT = 56 * 1024 * 1024

L_MIX = 256
PROJ_CHUNK = 256
TM_FFN = 512
TM_ROUTE = 256
TM_SLOT = 256
N_SLOTS = 2 * TOKENS + N_EXPERTS * TM_SLOT


def _sigmoid(x):
    return 1.0 / (1.0 + jnp.exp(-x))


def _mod_norm(x, g, scale, shift):
    ms = jnp.mean(x * x, axis=-1, keepdims=True)
    return (x * lax.rsqrt(ms + EPS)) * g * (1.0 + scale) + shift


def _causal_conv(x, ext_ref, w, b):
    L = x.shape[0]
    ext_ref[SUBLANES:, :] = x
    acc = b + w[CONV_W - 1:CONV_W] * x
    for s in range(1, CONV_W):
        acc = acc + w[CONV_W - 1 - s:CONV_W - s] * ext_ref[SUBLANES - s:SUBLANES - s + L, :]
    ext_ref[:SUBLANES, :] = x[L - SUBLANES:, :]
    return acc


def _zero_tile(z):
    bits = lax.bitcast_convert_type(z[-SUBLANES:, -LANES:], jnp.uint32)
    return ((bits >> 16) >> 16).astype(F32)


def _after(v, zeros, lag=0):
    if len(zeros) <= lag:
        return v
    tile = v[:SUBLANES, :LANES]
    while len(zeros) > lag:
        tile = tile + zeros.pop(0)
    head = tile if v.shape[1] == LANES else jnp.concatenate([tile, v[:SUBLANES, LANES:]], axis=1)
    return jnp.concatenate([head, v[SUBLANES:]], axis=0)


def _ada_kernel(c_ref, w_ref, b_ref, o_ref):
    c = c_ref[...]
    cs = c * _sigmoid(c)
    o_ref[...] = jnp.dot(cs, w_ref[...], preferred_element_type=F32) + b_ref[...]


def _ada(c, w_ada, b_ada):
    tn = 1536
    c8 = jnp.zeros((SUBLANES, D_MODEL), F32).at[:BATCH].set(c)
    out = pl.pallas_call(
        _ada_kernel,
        grid=(DEPTH, 6 * D_MODEL // tn),
        in_specs=[
            pl.BlockSpec((SUBLANES, D_MODEL), lambda l, n: (0, 0)),
            pl.BlockSpec((None, D_MODEL, tn), lambda l, n: (l, 0, n)),
            pl.BlockSpec((None, 1, tn), lambda l, n: (l, 0, n)),
        ],
        out_specs=pl.BlockSpec((None, SUBLANES, tn), lambda l, n: (l, 0, n)),
        out_shape=jax.ShapeDtypeStruct((DEPTH, SUBLANES, 6 * D_MODEL), F32),
        compiler_params=pltpu.CompilerParams(
            dimension_semantics=("arbitrary", "arbitrary"), vmem_limit_bytes=VMEM_LIMIT),
        name="ada",
    )(c8, w_ada, b_ada.reshape(DEPTH, 1, 6 * D_MODEL))
    return out[:, :BATCH].reshape(DEPTH, BATCH, 6, D_MODEL)


def _inproj_lru_kernel(x_ref, mod_ref, g_ref, w_ref, wc_ref, bc_ref, wax_ref, bax_ref, lam_ref, gm_ref,
                       zq_ref, zg_ref, hl_ref, zl_ref, ext_ref, hc_ref):
    L = L_MIX
    TIE_LAG = 4
    j = pl.program_id(0)
    slot = j % 2

    @pl.when(j == 0)
    def _():
        zl_ref[...] = jnp.zeros_like(zl_ref)
        ext_ref[...] = jnp.zeros_like(ext_ref)
        hc_ref[...] = jnp.zeros_like(hc_ref)

    keep = j % (SEQ // L) != 1
    ext_ref[:SUBLANES, :] = jnp.where(keep, ext_ref[:SUBLANES, :], 0.0)
    hc_ref[...] = jnp.where(keep, hc_ref[...], 0.0)
    xc = _causal_conv(zl_ref[1 - slot, :, :D_LRU].astype(F32), ext_ref, wc_ref[...], bc_ref[...])

    mod = mod_ref[...]
    h = _mod_norm(x_ref[...], g_ref[...], mod[1:2], mod[0:1]).astype(BF16)
    n_q, n_l = 4 * D_MLSTM, 2 * D_LRU
    todo = list(range(0, n_q + n_l + LANES, PROJ_CHUNK))
    zeros = []

    def project(n):
        for lo in todo[:n]:
            hi = min(lo + PROJ_CHUNK, n_q + n_l + LANES)
            z = jnp.dot(h, w_ref[:, lo:hi], preferred_element_type=F32)
            zeros.append(_zero_tile(z))
            if hi <= n_q:
                zq_ref[:, lo:hi] = z.astype(BF16)
            elif hi <= n_q + n_l:
                zl_ref[slot, :, lo - n_q:hi - n_q] = z.astype(BF16)
            else:
                zg_ref[...] = z
        del todo[:n]

    project(2)
    xc = _after(xc, zeros, TIE_LAG)
    gates = jnp.dot(xc.astype(BF16), wax_ref[...], preferred_element_type=F32) + bax_ref[...]
    r = _sigmoid(gates[:, :D_LRU])
    i = _sigmoid(gates[:, D_LRU:])
    project(2)
    r = _after(r, zeros, TIE_LAG)
    lam = lam_ref[...]
    sp = jnp.maximum(-lam, 0.0) + jnp.log1p(jnp.exp(-jnp.abs(lam)))
    log_a = -LRU_C * r * sp
    a = jnp.exp(log_a)
    y2 = 2.0 * log_a
    series = -y2 * (1.0 + y2 * (0.5 + y2 * (1.0 / 6.0 + y2 * (1.0 / 24.0))))
    u = jnp.sqrt(jnp.where(y2 > -0.01, series, 1.0 - a * a)) * (i * xc)
    project(2)
    u = _after(u, zeros, TIE_LAG)

    r8 = lax.broadcasted_iota(jnp.int32, (L, D_LRU), 0) & (SUBLANES - 1)
    for s in (1, 2, 4):
        a_sh = pltpu.roll(a, s, 0)
        u_sh = pltpu.roll(u, s, 0)
        valid = r8 >= s
        u = jnp.where(valid, a * u_sh + u, u)
        a = jnp.where(valid, a * a_sh, a)
        project(2)
        u = _after(u, zeros, TIE_LAG)
    gl = zl_ref[1 - slot, :, D_LRU:].astype(F32)
    gelu = 0.5 * gl * (1.0 + jnp.tanh(0.7978845608028654 * (gl + 0.044715 * gl * gl * gl)))
    project(len(todo))
    hc = hc_ref[...]
    enter = []
    for gi in range(L // SUBLANES):
        enter.append(hc)
        last = gi * SUBLANES + SUBLANES - 1
        hc = u[last:last + 1] + a[last:last + 1] * hc
    hc_ref[...] = hc
    hseq = jnp.concatenate(
        [u[gi * SUBLANES:(gi + 1) * SUBLANES] + a[gi * SUBLANES:(gi + 1) * SUBLANES] * enter[gi]
         for gi in range(L // SUBLANES)], axis=0)
    hl = hseq * gelu
    hl = hl * lax.rsqrt(jnp.mean(hl * hl, axis=-1, keepdims=True) + EPS) * gm_ref[...]
    hl_ref[...] = hl.astype(BF16)


def _inproj_lru(x, mod, g, w, w_conv, b_conv, wax, bax, lam, g_mix):
    L = L_MIX
    nc = SEQ // L
    n = BATCH * nc
    proj = lambda j: (jnp.minimum(j, n - 1), 0)
    mixed = lambda j: (jnp.maximum(j - 1, 0), 0)
    const = lambda shape: pl.BlockSpec(shape, lambda j: (0, 0))
    return pl.pallas_call(
        _inproj_lru_kernel,
        grid=(n + 1,),
        in_specs=[
            pl.BlockSpec((L, D_MODEL), proj),
            pl.BlockSpec((None, 6, D_MODEL), lambda j: (jnp.minimum(j, n - 1) // nc, 0, 0)),
            const((1, D_MODEL)), const(w.shape),
            const((CONV_W, D_LRU)), const((1, D_LRU)), const((D_LRU, 2 * D_LRU)),
            const((1, 2 * D_LRU)), const((1, D_LRU)), const((1, D_LRU)),
        ],
        out_specs=[
            pl.BlockSpec((L, 4 * D_MLSTM), proj),
            pl.BlockSpec((L, LANES), proj),
            pl.BlockSpec((L, D_LRU), mixed),
        ],
        out_shape=[
            jax.ShapeDtypeStruct((TOKENS, 4 * D_MLSTM), BF16),
            jax.ShapeDtypeStruct((TOKENS, LANES), F32),
            jax.ShapeDtypeStruct((TOKENS, D_LRU), BF16),
        ],
        scratch_shapes=[
            pltpu.VMEM((2, L, 2 * D_LRU), BF16),
            pltpu.VMEM((SUBLANES + L, D_LRU), F32),
            pltpu.VMEM((1, D_LRU), F32),
        ],
        compiler_params=pltpu.CompilerParams(
            dimension_semantics=("arbitrary",), vmem_limit_bytes=VMEM_LIMIT),
        name="inproj_rglru",
    )(x, mod, g, w, w_conv, b_conv, wax, bax, lam, g_mix)


def _mlstm_outproj_kernel(zq_ref, zg_ref, wc_ref, bc_ref, bg_ref, gm_ref, tri_ref,
                          hl_ref, x_ref, mod_ref, wm_ref, wl_ref, o_ref,
                          ct_ref, m_ref, ext_ref, hm_ref):
    L = L_MIX
    j = pl.program_id(0)
    slot = j % 2

    @pl.when(j == 0)
    def _():
        hm_ref[...] = jnp.zeros_like(hm_ref)

    @pl.when(j % (SEQ // L) == 0)
    def _():
        ct_ref[...] = jnp.zeros_like(ct_ref)
        m_ref[...] = jnp.zeros_like(m_ref)
        ext_ref[:SUBLANES, :] = jnp.zeros((SUBLANES, ext_ref.shape[1]), F32)

    todo = list(range(0, D_MODEL, PROJ_CHUNK))
    zeros = []

    def project(n):
        for lo in todo[:n]:
            cs = slice(lo, lo + PROJ_CHUNK)
            y = (jnp.dot(hm_ref[1 - slot], wm_ref[:, cs], preferred_element_type=F32)
                 + jnp.dot(hl_ref[...], wl_ref[:, cs], preferred_element_type=F32))
            zeros.append(_zero_tile(y))
            o_ref[:, cs] = x_ref[:, cs] + mod_ref[2:3, cs] * y
        del todo[:n]

    project(1)
    qk = _causal_conv(zq_ref[:, :2 * D_MLSTM].astype(F32), ext_ref, wc_ref[...], bc_ref[...])
    qk = _after(qk, zeros)
    project(1)
    qk = qk * _sigmoid(qk)
    q = _after(qk[:, :D_MLSTM], zeros).astype(BF16)
    k = (qk[:, D_MLSTM:] * (DH ** -0.5)).astype(BF16)
    project(1)

    g = zg_ref[...] + bg_ref[...]
    lf = jnp.minimum(g, 0.0) - jnp.log1p(jnp.exp(-jnp.abs(g)))
    tri = tri_ref[...]
    hi = lf.astype(BF16)
    r1 = lf - hi.astype(F32)
    mid = r1.astype(BF16)
    lo = (r1 - mid.astype(F32)).astype(BF16)
    bcum = (jnp.dot(tri, hi, preferred_element_type=F32)
            + jnp.dot(tri, mid, preferred_element_type=F32)
            + jnp.dot(tri, lo, preferred_element_type=F32))
    lane = lax.broadcasted_iota(jnp.int32, (L, LANES), 1)
    cols = _after(jnp.where(lane < N_HEADS, g, bcum), zeros)
    project(len(todo))
    rows = jnp.transpose(cols)

    t_idx = lax.broadcasted_iota(jnp.int32, (L, L), 0)
    s_idx = lax.broadcasted_iota(jnp.int32, (L, L), 1)
    causal = s_idx <= t_idx
    ones_col = (lax.broadcasted_iota(jnp.int32, (L, DH), 1) == 0).astype(BF16)
    gm = gm_ref[...]

    heads = range(N_HEADS)
    sls = [slice(h * DH, (h + 1) * DH) for h in heads]
    ic = [cols[:, h:h + 1] for h in heads]
    bc = [cols[:, N_HEADS + h:N_HEADS + h + 1] for h in heads]
    ir = [rows[h:h + 1, :] for h in heads]
    br = [rows[N_HEADS + h:N_HEADS + h + 1, :] for h in heads]
    b_last = [br[h][:, L - 1:L] for h in heads]
    m_prev = [m_ref[h:h + 1, 0:1] for h in heads]
    qh = [q[:, sls[h]] for h in heads]
    kh = [k[:, sls[h]] for h in heads]
    vaug = [jnp.concatenate([zq_ref[:, 2 * D_MLSTM + h * DH:2 * D_MLSTM + (h + 1) * DH], ones_col], axis=1)
            for h in heads]
    ct = [ct_ref[h] for h in heads]

    dm = [jnp.where(causal, bc[h] - br[h] + ir[h], -jnp.inf) for h in heads]
    dm[0] = _after(dm[0], zeros)
    inter_log = [bc[h] + m_prev[h] for h in heads]
    sc = [lax.dot_general(qh[h], kh[h], (((1,), (1,)), ((), ())), preferred_element_type=F32) for h in heads]
    inter = [jnp.dot(qh[h], ct[h].astype(BF16), preferred_element_type=F32) for h in heads]
    m_t = [jnp.maximum(inter_log[h], jnp.max(dm[h], axis=1, keepdims=True)) for h in heads]

    w_end = [b_last[h] - bc[h] + ic[h] for h in heads]
    m_loc = [jnp.max(w_end[h], axis=0, keepdims=True) for h in heads]
    m_new = [jnp.maximum(b_last[h] + m_prev[h], m_loc[h]) for h in heads]
    ev = [(jnp.exp(w_end[h] - m_loc[h]) * vaug[h].astype(F32)).astype(BF16) for h in heads]
    c_loc = [lax.dot_general(kh[h], ev[h], (((0,), (0,)), ((), ())), preferred_element_type=F32)
             for h in heads]

    p = [(sc[h] * jnp.exp(dm[h] - m_t[h])).astype(BF16) for h in heads]
    out = [jnp.dot(p[h], vaug[h], preferred_element_type=F32) + jnp.exp(inter_log[h] - m_t[h]) * inter[h]
           for h in heads]
    for h in heads:
        ct_ref[h] = (jnp.exp(b_last[h] + m_prev[h] - m_new[h]) * ct[h]
                     + jnp.exp(m_loc[h] - m_new[h]) * c_loc[h])
        m_ref[h:h + 1, :] = jnp.broadcast_to(m_new[h], (1, LANES))
    hh = [out[h][:, :DH] / jnp.maximum(jnp.abs(out[h][:, DH:DH + 1]), jnp.exp(-m_t[h])) for h in heads]
    hn = [hh[h] * lax.rsqrt(jnp.mean(hh[h] * hh[h], axis=-1, keepdims=True) + EPS) * gm[:, sls[h]]
          for h in heads]
    for h in heads:
        og = zq_ref[:, 3 * D_MLSTM + h * DH:3 * D_MLSTM + (h + 1) * DH].astype(F32)
        hm_ref[slot, :, sls[h]] = (hn[h] * _sigmoid(og)).astype(BF16)


def _mlstm_outproj(zq, zg, w_conv, b_conv, b_gates, g_mix, tri, hl, x, mod, wm, wl):
    L = L_MIX
    nc = SEQ // L
    n = BATCH * nc
    mixed = lambda j: (jnp.minimum(j, n - 1), 0)
    projected = lambda j: (jnp.maximum(j - 1, 0), 0)
    const = lambda shape: pl.BlockSpec(shape, lambda j: (0, 0))
    return pl.pallas_call(
        _mlstm_outproj_kernel,
        grid=(n + 1,),
        in_specs=[
            pl.BlockSpec((L, 4 * D_MLSTM), mixed),
            pl.BlockSpec((L, LANES), mixed),
            const((CONV_W, 2 * D_MLSTM)), const((1, 2 * D_MLSTM)), const((1, LANES)), const((1, D_MLSTM)),
            const((L, L)),
            pl.BlockSpec((L, D_LRU), projected),
            pl.BlockSpec((L, D_MODEL), projected),
            pl.BlockSpec((None, 6, D_MODEL), lambda j: (jnp.maximum(j - 1, 0) // nc, 0, 0)),
            const((D_MLSTM, D_MODEL)), const((D_LRU, D_MODEL)),
        ],
        out_specs=pl.BlockSpec((L, D_MODEL), projected),
        out_shape=jax.ShapeDtypeStruct((TOKENS, D_MODEL), F32),
        scratch_shapes=[
            pltpu.VMEM((N_HEADS, DH, 2 * DH), F32),
            pltpu.VMEM((SUBLANES, LANES), F32),
            pltpu.VMEM((SUBLANES + L, 2 * D_MLSTM), F32),
            pltpu.VMEM((2, L, D_MLSTM), BF16),
        ],
        compiler_params=pltpu.CompilerParams(
            dimension_semantics=("arbitrary",), vmem_limit_bytes=VMEM_LIMIT),
        name="mlstm_outproj",
    )(zq, zg, w_conv, b_conv, b_gates, g_mix, tri, hl, x, mod, wm, wl)


def _swiglu(hb, wg_ref, wu_ref, wd_ref):
    g = jnp.dot(hb, wg_ref[...], preferred_element_type=F32)
    u = jnp.dot(hb, wu_ref[...], preferred_element_type=F32)
    act = g * _sigmoid(g) * u
    return jnp.dot(act.astype(BF16), wd_ref[...], preferred_element_type=F32)


def _ffn_kernel(x_ref, mod_ref, g_ref, wg_ref, wu_ref, wd_ref, o_ref):
    x = x_ref[...]
    hb = _mod_norm(x, g_ref[...], mod_ref[4:5, :], mod_ref[3:4, :]).astype(BF16)
    o_ref[...] = x + mod_ref[5:6, :] * _swiglu(hb, wg_ref, wu_ref, wd_ref)


def _ffn(x, mod, g, wg, wu, wd):
    tm = TM_FFN
    per_b = SEQ // tm
    return pl.pallas_call(
        _ffn_kernel,
        grid=(TOKENS // tm,),
        in_specs=[
            pl.BlockSpec((tm, D_MODEL), lambda i: (i, 0)),
            pl.BlockSpec((None, 6, D_MODEL), lambda i: (i // per_b, 0, 0)),
            pl.BlockSpec((1, D_MODEL), lambda i: (0, 0)),
            pl.BlockSpec((D_MODEL, D_FF), lambda i: (0, 0)),
            pl.BlockSpec((D_MODEL, D_FF), lambda i: (0, 0)),
            pl.BlockSpec((D_FF, D_MODEL), lambda i: (0, 0)),
        ],
        out_specs=pl.BlockSpec((tm, D_MODEL), lambda i: (i, 0)),
        out_shape=jax.ShapeDtypeStruct((TOKENS, D_MODEL), F32),
        compiler_params=pltpu.CompilerParams(
            dimension_semantics=("arbitrary",), vmem_limit_bytes=VMEM_LIMIT),
        name="ffn",
    )(x, mod, g, wg, wu, wd)


def _router_kernel(x_ref, mod_ref, g_ref, wrh_ref, wrl_ref, br_ref, stri_ref, upper_ref,
                   h_ref, meta_ref, tab_ref, carry_ref, lg_ref):
    i = pl.program_id(0)
    slot = i % 2

    @pl.when(i == 0)
    def _():
        carry_ref[...] = jnp.zeros_like(carry_ref)
        lg_ref[...] = jnp.zeros_like(lg_ref)

    tm = x_ref.shape[0]
    lane = lax.broadcasted_iota(jnp.int32, (tm, LANES), 1)
    logits = jnp.where(lane < N_EXPERTS, lg_ref[1 - slot], -jnp.inf)
    m1 = jnp.max(logits, axis=-1, keepdims=True)

    h = _mod_norm(x_ref[...], g_ref[...], mod_ref[4:5, :], mod_ref[3:4, :])
    hb = h.astype(BF16)
    h_ref[...] = hb
    i1 = jnp.min(jnp.where(logits == m1, lane, LANES), axis=-1, keepdims=True)

    hlo = (h - hb.astype(F32)).astype(BF16)
    wrh = wrh_ref[...]
    new_logits = jnp.dot(hb, wrh, preferred_element_type=F32)
    rest = jnp.where(lane == i1, -jnp.inf, logits)
    m2 = jnp.max(rest, axis=-1, keepdims=True)
    new_logits = new_logits + jnp.dot(hlo, wrh, preferred_element_type=F32)
    i2 = jnp.min(jnp.where(rest == m2, lane, LANES), axis=-1, keepdims=True)
    new_logits = new_logits + jnp.dot(hb, wrl_ref[...], preferred_element_type=F32)
    lg_ref[slot] = new_logits + br_ref[...]

    e2 = jnp.exp(m2 - m1)
    p1 = 1.0 / (1.0 + e2)
    p2 = e2 * p1
    ind = jnp.where((lane == i1) | (lane == i2), 1.0, 0.0)
    carry = carry_ref[0:1, :]
    aligned = jnp.floor(carry * (1.0 / SUBLANES)) * SUBLANES
    lead = carry - aligned
    filled = lead + jnp.sum(ind, axis=0, keepdims=True)
    full = jnp.floor(filled * (1.0 / SUBLANES)) * SUBLANES
    n_e = jnp.floor((filled + (SUBLANES - 1)) * (1.0 / SUBLANES)) * SUBLANES
    lrank = jnp.dot(stri_ref[...], ind.astype(BF16), preferred_element_type=F32)
    loff = jnp.dot(jnp.broadcast_to(n_e, (SUBLANES, LANES)).astype(BF16), upper_ref[...],
                   preferred_element_type=F32)[0:1, :]
    row = lrank + loff + lead
    d1 = jnp.sum(jnp.where(lane == i1, row, 0.0), axis=-1, keepdims=True)
    d2 = jnp.sum(jnp.where(lane == i2, row, 0.0), axis=-1, keepdims=True)
    vals = (i1.astype(F32), i2.astype(F32), d1, d2, p1, p2)
    meta = jnp.zeros((tm, LANES), F32)
    for n, v in enumerate(vals):
        meta = jnp.where(lane == n, v, meta)
    meta_ref[...] = meta
    srow = lax.broadcasted_iota(jnp.int32, (SUBLANES, LANES), 0)
    carry = carry + jnp.where(i > 0, filled - lead, 0.0)
    tab = jnp.zeros((SUBLANES, LANES), F32)
    for n, v in enumerate((n_e, aligned, loff, full, carry)):
        tab = jnp.where(srow == n, v, tab)
    tab_ref[...] = tab
    carry_ref[0:1, :] = carry


def _router(x, mod, g, wrh, wrl, br, stri, upper):
    tm = TM_ROUTE
    n = TOKENS // tm
    per_b = SEQ // tm
    row = lambda shape: pl.BlockSpec(shape, lambda i: (0, 0))
    normed = lambda i: jnp.minimum(i, n - 1)
    ranked = lambda i: jnp.maximum(i - 1, 0)
    return pl.pallas_call(
        _router_kernel,
        grid=(n + 1,),
        in_specs=[
            pl.BlockSpec((tm, D_MODEL), lambda i: (normed(i), 0)),
            pl.BlockSpec((None, 6, D_MODEL), lambda i: (normed(i) // per_b, 0, 0)),
            row((1, D_MODEL)), row((D_MODEL, LANES)), row((D_MODEL, LANES)), row((1, LANES)),
            row((tm, tm)), row((LANES, LANES)),
        ],
        out_specs=[
            pl.BlockSpec((tm, D_MODEL), lambda i: (normed(i), 0)),
            pl.BlockSpec((tm, LANES), lambda i: (ranked(i), 0)),
            pl.BlockSpec((None, SUBLANES, LANES), lambda i: (ranked(i), 0, 0)),
        ],
        out_shape=[
            jax.ShapeDtypeStruct((TOKENS, D_MODEL), BF16),
            jax.ShapeDtypeStruct((TOKENS, LANES), F32),
            jax.ShapeDtypeStruct((n, SUBLANES, LANES), F32),
        ],
        scratch_shapes=[pltpu.VMEM((SUBLANES, LANES), F32), pltpu.VMEM((2, tm, LANES), F32)],
        compiler_params=pltpu.CompilerParams(
            dimension_semantics=("arbitrary",), vmem_limit_bytes=VMEM_LIMIT),
        name="router",
    )(x, mod, g, wrh, wrl, br, stri, upper)


BLOCK_ROWS = 2 * TM_ROUTE + LANES
RANGE_PIECES = tuple(1 << b for b in range(TM_ROUTE.bit_length() - 1, 2, -1))
assert TM_SLOT <= 2 * RANGE_PIECES[0]


def _range_pieces(n, loc, slot, block_ref, slots_ref, sem, to_slots, visit):
    loc = pl.multiple_of(loc, SUBLANES)
    slot = pl.multiple_of(slot, SUBLANES)
    for piece in RANGE_PIECES:
        has = (n & piece) != 0
        src = block_ref.at[pl.ds(loc, piece)]
        dst = slots_ref.at[pl.ds(slot, piece)]
        if not to_slots:
            src, dst = dst, src

        @pl.when(has)
        def _():
            visit(pltpu.make_async_copy(src, dst, sem))

        step = jnp.where(has, piece, 0)
        loc = pl.multiple_of(loc + step, SUBLANES)
        slot = pl.multiple_of(slot + step, SUBLANES)


def _chunk_copies(chunk, n_ref, loc_ref, slot_ref, block_ref, slots_ref, sem, to_slots, visit):
    for e in range(N_EXPERTS):
        i = chunk * N_EXPERTS + e
        _range_pieces(n_ref[i], loc_ref[i], slot_ref[i], block_ref, slots_ref, sem, to_slots, visit)


def _start(cp):
    cp.start()


def _wait(cp):
    cp.wait()


def _choice_onehots(meta, width):
    col = lax.broadcasted_iota(jnp.int32, (meta.shape[0], width), 1)
    d1 = meta[:, 2:3].astype(jnp.int32)
    d2 = meta[:, 3:4].astype(jnp.int32)
    return col == d1, col == d2


def _zero_copies(padn_ref, pads_ref, used_ref, zero_ref, xs_ref, sem, visit):
    for e in range(N_EXPERTS):
        _range_pieces(padn_ref[e], 0, pads_ref[e], zero_ref, xs_ref, sem, True, visit)
    for t in range(2 * TOKENS // TM_SLOT, N_SLOTS // TM_SLOT):
        @pl.when(t * TM_SLOT >= used_ref[0])
        def _():
            visit(pltpu.make_async_copy(zero_ref, xs_ref.at[pl.ds(t * TM_SLOT, TM_SLOT)], sem))


def _dispatch_kernel(n_ref, loc_ref, slot_ref, full_ref, padn_ref, pads_ref, used_ref, h_ref, meta_ref,
                     xs_ref, block_ref, zero_ref, tail_ref, sem, zsem):
    i = pl.program_id(0)
    last = pl.num_programs(0) - 1
    cur = i % 2
    zeros = (padn_ref, pads_ref, used_ref, zero_ref, xs_ref, zsem)

    @pl.when(i == 0)
    def _():
        zero_ref[...] = jnp.zeros_like(zero_ref)
        tail_ref[...] = jnp.zeros_like(tail_ref)
        _zero_copies(*zeros, _start)

    rows_t = jnp.transpose(meta_ref[...])
    brow = lax.broadcasted_iota(jnp.int32, (BLOCK_ROWS, rows_t.shape[1]), 0)
    sel = ((brow == rows_t[2:3, :].astype(jnp.int32)) | (brow == rows_t[3:4, :].astype(jnp.int32))).astype(BF16)
    block_ref[cur] = jnp.dot(sel, h_ref[...], preferred_element_type=F32)
    blk = block_ref.at[cur]
    for e in range(N_EXPERTS):
        t = i * N_EXPERTS + e
        first = pl.ds(pl.multiple_of(loc_ref[t], SUBLANES), SUBLANES)
        blk[first, :] = blk[first, :] + tail_ref[e]
        partial = full_ref[t] < n_ref[t]
        lastt = pl.ds(pl.multiple_of(loc_ref[t] + full_ref[t], SUBLANES), SUBLANES)
        tail_ref[e] = jnp.where(partial, blk[lastt, :], 0.0)
    tabs = (n_ref, loc_ref, slot_ref)

    @pl.when(i > 0)
    def _():
        _chunk_copies(i - 1, *tabs, block_ref.at[1 - cur], xs_ref, sem.at[1 - cur], True, _wait)

    _chunk_copies(i, *tabs, block_ref.at[cur], xs_ref, sem.at[cur], True, _start)

    @pl.when(i == last)
    def _():
        _chunk_copies(i, *tabs, block_ref.at[cur], xs_ref, sem.at[cur], True, _wait)
        _zero_copies(*zeros, _wait)


def _dispatch(tabs, full, pad_tabs, h, meta):
    tm = TM_ROUTE
    return pl.pallas_call(
        _dispatch_kernel,
        grid_spec=pltpu.PrefetchScalarGridSpec(
            num_scalar_prefetch=7,
            grid=(TOKENS // tm,),
            in_specs=[
                pl.BlockSpec((tm, D_MODEL), lambda i, *_: (i, 0)),
                pl.BlockSpec((tm, LANES), lambda i, *_: (i, 0)),
            ],
            out_specs=pl.BlockSpec(memory_space=pl.ANY),
            scratch_shapes=[pltpu.VMEM((2, BLOCK_ROWS, D_MODEL), F32),
                            pltpu.VMEM((TM_SLOT, D_MODEL), F32),
                            pltpu.VMEM((N_EXPERTS, SUBLANES, D_MODEL), F32),
                            pltpu.SemaphoreType.DMA((2,)), pltpu.SemaphoreType.DMA],
        ),
        out_shape=jax.ShapeDtypeStruct((N_SLOTS, D_MODEL), F32),
        compiler_params=pltpu.CompilerParams(
            dimension_semantics=("arbitrary",), vmem_limit_bytes=VMEM_LIMIT),
        name="dispatch",
    )(*tabs, full, *pad_tabs, h, meta)


def _experts_kernel(te_ref, tv_ref, ts_ref, xs_ref, wg_ref, wu_ref, wd_ref, o_ref):
    del te_ref, ts_ref
    valid = tv_ref[pl.program_id(0)] != 0

    @pl.when(valid)
    def _():
        o_ref[...] = _swiglu(xs_ref[...].astype(BF16), wg_ref, wu_ref, wd_ref)

    @pl.when(jnp.logical_not(valid))
    def _():
        o_ref[...] = jnp.zeros_like(o_ref)


def _experts(tile_e, tile_v, tile_src, xs, wg, wu, wd):
    tm = TM_SLOT
    return pl.pallas_call(
        _experts_kernel,
        grid_spec=pltpu.PrefetchScalarGridSpec(
            num_scalar_prefetch=3,
            grid=(N_SLOTS // tm,),
            in_specs=[
                pl.BlockSpec((tm, D_MODEL), lambda i, te, tv, ts: (ts[i], 0)),
                pl.BlockSpec((None, D_MODEL, D_FF), lambda i, te, tv, ts: (te[i], 0, 0)),
                pl.BlockSpec((None, D_MODEL, D_FF), lambda i, te, tv, ts: (te[i], 0, 0)),
                pl.BlockSpec((None, D_FF, D_MODEL), lambda i, te, tv, ts: (te[i], 0, 0)),
            ],
            out_specs=pl.BlockSpec((tm, D_MODEL), lambda i, te, tv, ts: (i, 0)),
        ),
        out_shape=jax.ShapeDtypeStruct((N_SLOTS, D_MODEL), F32),
        compiler_params=pltpu.CompilerParams(
            dimension_semantics=("arbitrary",), vmem_limit_bytes=VMEM_LIMIT),
        name="experts",
    )(tile_e, tile_v, tile_src, xs, wg, wu, wd)


def _combine_kernel(n_ref, loc_ref, slot_ref, x_ref, mod_ref, meta_ref, gf_ref, ys_ref, o_ref,
                    block_ref, sem):
    i = pl.program_id(0)
    cur = i % 2
    tabs = (n_ref, loc_ref, slot_ref)

    @pl.when(i == 0)
    def _():
        block_ref[...] = jnp.zeros_like(block_ref)
        _chunk_copies(0, *tabs, block_ref.at[0], ys_ref, sem.at[0], False, _start)

    @pl.when(i + 1 < pl.num_programs(0))
    def _():
        _chunk_copies(i + 1, *tabs, block_ref.at[1 - cur], ys_ref, sem.at[1 - cur], False, _start)

    _chunk_copies(i, *tabs, block_ref.at[cur], ys_ref, sem.at[cur], False, _wait)
    meta = meta_ref[...]
    a1, a2 = _choice_onehots(meta, BLOCK_ROWS)
    blk = block_ref[cur].astype(BF16)
    y = (meta[:, 4:5] * jnp.dot(a1.astype(BF16), blk, preferred_element_type=F32)
         + meta[:, 5:6] * jnp.dot(a2.astype(BF16), blk, preferred_element_type=F32))
    y = x_ref[...] + mod_ref[5:6, :] * y
    o_ref[...] = y * lax.rsqrt(jnp.mean(y * y, axis=-1, keepdims=True) + EPS) * gf_ref[...]


def _combine(tabs, x, mod, meta, g_final, ys):
    tm = TM_ROUTE
    per_b = SEQ // tm
    return pl.pallas_call(
        _combine_kernel,
        grid_spec=pltpu.PrefetchScalarGridSpec(
            num_scalar_prefetch=3,
            grid=(TOKENS // tm,),
            in_specs=[
                pl.BlockSpec((tm, D_MODEL), lambda i, *_: (i, 0)),
                pl.BlockSpec((None, 6, D_MODEL), lambda i, *_: (i // per_b, 0, 0)),
                pl.BlockSpec((tm, LANES), lambda i, *_: (i, 0)),
                pl.BlockSpec((1, D_MODEL), lambda i, *_: (0, 0)),
                pl.BlockSpec(memory_space=pl.ANY),
            ],
            out_specs=pl.BlockSpec((tm, D_MODEL), lambda i, *_: (i, 0)),
            scratch_shapes=[pltpu.VMEM((2, BLOCK_ROWS, D_MODEL), F32), pltpu.SemaphoreType.DMA((2,))],
        ),
        out_shape=jax.ShapeDtypeStruct((TOKENS, D_MODEL), F32),
        compiler_params=pltpu.CompilerParams(
            dimension_semantics=("arbitrary",), vmem_limit_bytes=VMEM_LIMIT),
        name="combine",
    )(*tabs, x, mod, meta, g_final, ys)


def _moe(x, mod, g, w_router, b_router, wg, wu, wd, g_final):
    wr = _pad_lanes(w_router)
    wrh = wr.astype(BF16)
    wrl = (wr - wrh.astype(F32)).astype(BF16)
    stri = (lax.broadcasted_iota(jnp.int32, (TM_ROUTE, TM_ROUTE), 1)
            < lax.broadcasted_iota(jnp.int32, (TM_ROUTE, TM_ROUTE), 0)).astype(BF16)
    upper = (lax.broadcasted_iota(jnp.int32, (LANES, LANES), 0)
             < lax.broadcasted_iota(jnp.int32, (LANES, LANES), 1)).astype(BF16)
    h, meta, tab = _router(x, mod, g, wrh, wrl, _pad_lanes(b_router[None]), stri, upper)
    tab = tab[:, :5, :N_EXPERTS].astype(jnp.int32)
    n_ce, tile_ce, loc_ce, full_ce = tab[:, 0], tab[:, 1], tab[:, 2], tab[:, 3]
    counts = tab[-1, 4]
    padded = (counts + TM_SLOT - 1) // TM_SLOT * TM_SLOT
    ends = jnp.cumsum(padded)
    slot_ce = (ends - padded)[None, :] + tile_ce
    tabs = (n_ce.reshape(-1), loc_ce.reshape(-1), slot_ce.reshape(-1))
    tiled = (counts + SUBLANES - 1) // SUBLANES * SUBLANES
    pad_tabs = (padded - tiled, ends - padded + tiled, ends[-1:])
    tile = jnp.arange(N_SLOTS // TM_SLOT, dtype=jnp.int32)
    tile_e = jnp.minimum(jnp.sum(tile[:, None] * TM_SLOT >= ends[None, :], axis=1), N_EXPERTS - 1)
    tile_v = (tile * TM_SLOT < ends[-1]).astype(jnp.int32)
    tile_src = jnp.minimum(tile, ends[-1] // TM_SLOT - 1)
    xs = _dispatch(tabs, full_ce.reshape(-1), pad_tabs, h, meta)
    ys = _experts(tile_e.astype(jnp.int32), tile_v, tile_src, xs, wg, wu, wd)
    return _combine(tabs, x, mod, meta, g_final, ys)


def _block_diag(w):
    eye = jnp.eye(N_LRU_BLOCKS, dtype=w.dtype)
    return jnp.einsum('nde,nm->ndme', w, eye).reshape(D_LRU, D_LRU)


def _pad_lanes(a):
    return jnp.zeros(a.shape[:-1] + (LANES,), a.dtype).at[..., :a.shape[-1]].set(a)


def kernel(x, c, w_ada, b_ada, g_norm_mix, g_norm_ffn, w_in, w_conv_qk, b_conv_qk, b_gates,
           w_conv_lru, b_conv_lru, w_lru_a, b_lru_a, w_lru_x, b_lru_x, lru_lambda, g_mix_out, w_out,
           w_ff_gate, w_ff_up, w_ff_down, w_router, b_router, w_exp_gate, w_exp_up, w_exp_down, g_final):
    assert DEPTH == 2
    mods = _ada(c, w_ada, b_ada)
    tri = (lax.broadcasted_iota(jnp.int32, (L_MIX, L_MIX), 1)
           <= lax.broadcasted_iota(jnp.int32, (L_MIX, L_MIX), 0)).astype(BF16)
    xt = x.reshape(TOKENS, D_MODEL)
    for l in range(DEPTH):
        mod = mods[l]
        wi = w_in[l]
        n_q = 4 * D_MLSTM
        wcat = jnp.concatenate([wi[:, :n_q], wi[:, n_q + 2 * N_HEADS:],
                                _pad_lanes(wi[:, n_q:n_q + 2 * N_HEADS])], axis=1).astype(BF16)
        wax = jnp.concatenate([_block_diag(w_lru_a[l]), _block_diag(w_lru_x[l])], axis=1).astype(BF16)
        bax = jnp.concatenate([b_lru_a[l], b_lru_x[l]])[None]
        zq, zg, hl = _inproj_lru(xt, mod, g_norm_mix[l][None], wcat,
                                 w_conv_lru[l], b_conv_lru[l][None], wax, bax, lru_lambda[l][None],
                                 g_mix_out[l][None, D_MLSTM:])
        wo = w_out[l].astype(BF16)
        xt = _mlstm_outproj(zq, zg, w_conv_qk[l], b_conv_qk[l][None], _pad_lanes(b_gates[l][None]),
                            g_mix_out[l][None, :D_MLSTM], tri, hl, xt, mod, wo[:D_MLSTM], wo[D_MLSTM:])
        jj = l // 2
        if l % 2 == 0:
            xt = _ffn(xt, mod, g_norm_ffn[l][None], w_ff_gate[jj].astype(BF16),
                      w_ff_up[jj].astype(BF16), w_ff_down[jj].astype(BF16))
        else:
            xt = _moe(xt, mod, g_norm_ffn[l][None], w_router[jj], b_router[jj],
                      w_exp_gate[jj].astype(BF16), w_exp_up[jj].astype(BF16),
                      w_exp_down[jj].astype(BF16), g_final[None])
    return xt.reshape(BATCH, SEQ, D_MODEL)
```

```python
import jax
import jax.numpy as jnp
from jax import lax
from jax.experimental import pallas as pl
from jax.experimental.pallas import tpu as pltpu

F32 = jnp.float32
BF16 = jnp.bfloat16

D_MODEL = 1024
BATCH = 4
SEQ = 8192
TOKENS = BATCH * SEQ
DEPTH = 2
D_MLSTM = 512
N_HEADS = 4
DH = 128
D_LRU = 512
N_LRU_BLOCKS = 8
DB_LRU = 64
CONV_W = 4
LRU_C = 8.0
D_FF = 2816
N_EXPERTS = 8
EPS = 1e-6

LANES = 128
SUBLANES = 8
VMEM_LIMIT = 56 * 1024 * 1024

L_MIX = 256
PROJ_CHUNK = 256
TM_FFN = 512
TM_ROUTE = 256
TM_SLOT = 256
N_SLOTS = 2 * TOKENS + N_EXPERTS * TM_SLOT


def _sigmoid(x):
    return 1.0 / (1.0 + jnp.exp(-x))


def _mod_norm(x, g, scale, shift):
    ms = jnp.mean(x * x, axis=-1, keepdims=True)
    return (x * lax.rsqrt(ms + EPS)) * g * (1.0 + scale) + shift


def _causal_conv(x, ext_ref, w, b):
    L = x.shape[0]
    ext_ref[SUBLANES:, :] = x
    acc = b + w[CONV_W - 1:CONV_W] * x
    for s in range(1, CONV_W):
        acc = acc + w[CONV_W - 1 - s:CONV_W - s] * ext_ref[SUBLANES - s:SUBLANES - s + L, :]
    ext_ref[:SUBLANES, :] = x[L - SUBLANES:, :]
    return acc


def _zero_tile(z):
    bits = lax.bitcast_convert_type(z[-SUBLANES:, -LANES:], jnp.uint32)
    return ((bits >> 16) >> 16).astype(F32)


def _after(v, zeros, lag=0):
    if len(zeros) <= lag:
        return v
    tile = v[:SUBLANES, :LANES]
    while len(zeros) > lag:
        tile = tile + zeros.pop(0)
    head = tile if v.shape[1] == LANES else jnp.concatenate([tile, v[:SUBLANES, LANES:]], axis=1)
    return jnp.concatenate([head, v[SUBLANES:]], axis=0)


def _ada_kernel(c_ref, w_ref, b_ref, o_ref):
    c = c_ref[...]
    cs = c * _sigmoid(c)
    o_ref[...] = jnp.dot(cs, w_ref[...], preferred_element_type=F32) + b_ref[...]


def _ada(c, w_ada, b_ada):
    tn = 1536
    c8 = jnp.zeros((SUBLANES, D_MODEL), F32).at[:BATCH].set(c)
    out = pl.pallas_call(
        _ada_kernel,
        grid=(DEPTH, 6 * D_MODEL // tn),
        in_specs=[
            pl.BlockSpec((SUBLANES, D_MODEL), lambda l, n: (0, 0)),
            pl.BlockSpec((None, D_MODEL, tn), lambda l, n: (l, 0, n)),
            pl.BlockSpec((None, 1, tn), lambda l, n: (l, 0, n)),
        ],
        out_specs=pl.BlockSpec((None, SUBLANES, tn), lambda l, n: (l, 0, n)),
        out_shape=jax.ShapeDtypeStruct((DEPTH, SUBLANES, 6 * D_MODEL), F32),
        compiler_params=pltpu.CompilerParams(
            dimension_semantics=("arbitrary", "arbitrary"), vmem_limit_bytes=VMEM_LIMIT),
        name="ada",
    )(c8, w_ada, b_ada.reshape(DEPTH, 1, 6 * D_MODEL))
    return out[:, :BATCH].reshape(DEPTH, BATCH, 6, D_MODEL)


def _inproj_lru_kernel(x_ref, mod_ref, g_ref, w_ref, wc_ref, bc_ref, wax_ref, bax_ref, lam_ref, gm_ref,
                       zq_ref, zg_ref, hl_ref, zl_ref, ext_ref, hc_ref):
    L = L_MIX
    TIE_LAG = 4
    j = pl.program_id(0)
    slot = j % 2

    @pl.when(j == 0)
    def _():
        zl_ref[...] = jnp.zeros_like(zl_ref)
        ext_ref[...] = jnp.zeros_like(ext_ref)
        hc_ref[...] = jnp.zeros_like(hc_ref)

    keep = j % (SEQ // L) != 1
    ext_ref[:SUBLANES, :] = jnp.where(keep, ext_ref[:SUBLANES, :], 0.0)
    hc_ref[...] = jnp.where(keep, hc_ref[...], 0.0)
    xc = _causal_conv(zl_ref[1 - slot, :, :D_LRU].astype(F32), ext_ref, wc_ref[...], bc_ref[...])

    mod = mod_ref[...]
    h = _mod_norm(x_ref[...], g_ref[...], mod[1:2], mod[0:1]).astype(BF16)
    n_q, n_l = 4 * D_MLSTM, 2 * D_LRU
    todo = list(range(0, n_q + n_l + LANES, PROJ_CHUNK))
    zeros = []

    def project(n):
        for lo in todo[:n]:
            hi = min(lo + PROJ_CHUNK, n_q + n_l + LANES)
            z = jnp.dot(h, w_ref[:, lo:hi], preferred_element_type=F32)
            zeros.append(_zero_tile(z))
            if hi <= n_q:
                zq_ref[:, lo:hi] = z.astype(BF16)
            elif hi <= n_q + n_l:
                zl_ref[slot, :, lo - n_q:hi - n_q] = z.astype(BF16)
            else:
                zg_ref[...] = z
        del todo[:n]

    project(3)
    xc = _after(xc, zeros, TIE_LAG)
    gates = jnp.dot(xc.astype(BF16), wax_ref[...], preferred_element_type=F32) + bax_ref[...]
    r = _sigmoid(gates[:, :D_LRU])
    i = _sigmoid(gates[:, D_LRU:])
    project(5)
    r = _after(r, zeros, TIE_LAG)
    lam = lam_ref[...]
    sp = jnp.maximum(-lam, 0.0) + jnp.log1p(jnp.exp(-jnp.abs(lam)))
    log_a = -LRU_C * r * sp
    a = jnp.exp(log_a)
    y2 = 2.0 * log_a
    series = -y2 * (1.0 + y2 * (0.5 + y2 * (1.0 / 6.0 + y2 * (1.0 / 24.0))))
    u = jnp.sqrt(jnp.where(y2 > -0.01, series, 1.0 - a * a)) * (i * xc)
    project(3)
    u = _after(u, zeros, TIE_LAG)

    r8 = lax.broadcasted_iota(jnp.int32, (L, D_LRU), 0) & (SUBLANES - 1)
    for s in (1, 2, 4):
        a_sh = pltpu.roll(a, s, 0)
        u_sh = pltpu.roll(u, s, 0)
        valid = r8 >= s
        u = jnp.where(valid, a * u_sh + u, u)
        a = jnp.where(valid, a * a_sh, a)
        u = _after(u, zeros, TIE_LAG)
    gl = zl_ref[1 - slot, :, D_LRU:].astype(F32)
    gelu = 0.5 * gl * (1.0 + jnp.tanh(0.7978845608028654 * (gl + 0.044715 * gl * gl * gl)))
    project(len(todo))
    hc = hc_ref[...]
    enter = []
    for gi in range(L // SUBLANES):
        enter.append(hc)
        last = gi * SUBLANES + SUBLANES - 1
        hc = u[last:last + 1] + a[last:last + 1] * hc
    hc_ref[...] = hc
    hseq = jnp.concatenate(
        [u[gi * SUBLANES:(gi + 1) * SUBLANES] + a[gi * SUBLANES:(gi + 1) * SUBLANES] * enter[gi]
         for gi in range(L // SUBLANES)], axis=0)
    hl = hseq * gelu
    hl = hl * lax.rsqrt(jnp.mean(hl * hl, axis=-1, keepdims=True) + EPS) * gm_ref[...]
    hl_ref[...] = hl.astype(BF16)


def _inproj_lru(x, mod, g, w, w_conv, b_conv, wax, bax, lam, g_mix):
    L = L_MIX
    nc = SEQ // L
    n = BATCH * nc
    proj = lambda j: (jnp.minimum(j, n - 1), 0)
    mixed = lambda j: (jnp.maximum(j - 1, 0), 0)
    const = lambda shape: pl.BlockSpec(shape, lambda j: (0, 0))
    return pl.pallas_call(
        _inproj_lru_kernel,
        grid=(n + 1,),
        in_specs=[
            pl.BlockSpec((L, D_MODEL), proj),
            pl.BlockSpec((None, 6, D_MODEL), lambda j: (jnp.minimum(j, n - 1) // nc, 0, 0)),
            const((1, D_MODEL)), const(w.shape),
            const((CONV_W, D_LRU)), const((1, D_LRU)), const((D_LRU, 2 * D_LRU)),
            const((1, 2 * D_LRU)), const((1, D_LRU)), const((1, D_LRU)),
        ],
        out_specs=[
            pl.BlockSpec((L, 4 * D_MLSTM), proj),
            pl.BlockSpec((L, LANES), proj),
            pl.BlockSpec((L, D_LRU), mixed),
        ],
        out_shape=[
            jax.ShapeDtypeStruct((TOKENS, 4 * D_MLSTM), BF16),
            jax.ShapeDtypeStruct((TOKENS, LANES), F32),
            jax.ShapeDtypeStruct((TOKENS, D_LRU), BF16),
        ],
        scratch_shapes=[
            pltpu.VMEM((2, L, 2 * D_LRU), BF16),
            pltpu.VMEM((SUBLANES + L, D_LRU), F32),
            pltpu.VMEM((1, D_LRU), F32),
        ],
        compiler_params=pltpu.CompilerParams(
            dimension_semantics=("arbitrary",), vmem_limit_bytes=VMEM_LIMIT),
        name="inproj_rglru",
    )(x, mod, g, w, w_conv, b_conv, wax, bax, lam, g_mix)


def _mlstm_outproj_kernel(zq_ref, zg_ref, wc_ref, bc_ref, bg_ref, gm_ref, tri_ref,
                          hl_ref, x_ref, mod_ref, wm_ref, wl_ref, o_ref,
                          ct_ref, m_ref, ext_ref, hm_ref):
    L = L_MIX
    j = pl.program_id(0)
    slot = j % 2

    @pl.when(j == 0)
    def _():
        hm_ref[...] = jnp.zeros_like(hm_ref)

    @pl.when(j % (SEQ // L) == 0)
    def _():
        ct_ref[...] = jnp.zeros_like(ct_ref)
        m_ref[...] = jnp.zeros_like(m_ref)
        ext_ref[:SUBLANES, :] = jnp.zeros((SUBLANES, ext_ref.shape[1]), F32)

    todo = list(range(0, D_MODEL, PROJ_CHUNK))
    zeros = []

    def project(n):
        for lo in todo[:n]:
            cs = slice(lo, lo + PROJ_CHUNK)
            y = (jnp.dot(hm_ref[1 - slot], wm_ref[:, cs], preferred_element_type=F32)
                 + jnp.dot(hl_ref[...], wl_ref[:, cs], preferred_element_type=F32))
            zeros.append(_zero_tile(y))
            o_ref[:, cs] = x_ref[:, cs] + mod_ref[2:3, cs] * y
        del todo[:n]

    project(1)
    qk = _causal_conv(zq_ref[:, :2 * D_MLSTM].astype(F32), ext_ref, wc_ref[...], bc_ref[...])
    qk = _after(qk, zeros)
    project(1)
    qk = qk * _sigmoid(qk)
    q = _after(qk[:, :D_MLSTM], zeros).astype(BF16)
    k = (qk[:, D_MLSTM:] * (DH ** -0.5)).astype(BF16)
    project(1)

    g = zg_ref[...] + bg_ref[...]
    lf = jnp.minimum(g, 0.0) - jnp.log1p(jnp.exp(-jnp.abs(g)))
    tri = tri_ref[...]
    hi = lf.astype(BF16)
    r1 = lf - hi.astype(F32)
    mid = r1.astype(BF16)
    lo = (r1 - mid.astype(F32)).astype(BF16)
    bcum = (jnp.dot(tri, hi, preferred_element_type=F32)
            + jnp.dot(tri, mid, preferred_element_type=F32)
            + jnp.dot(tri, lo, preferred_element_type=F32))
    lane = lax.broadcasted_iota(jnp.int32, (L, LANES), 1)
    cols = _after(jnp.where(lane < N_HEADS, g, bcum), zeros)
    project(len(todo))
    rows = jnp.transpose(cols)

    t_idx = lax.broadcasted_iota(jnp.int32, (L, L), 0)
    s_idx = lax.broadcasted_iota(jnp.int32, (L, L), 1)
    causal = s_idx <= t_idx
    ones_col = (lax.broadcasted_iota(jnp.int32, (L, DH), 1) == 0).astype(BF16)
    gm = gm_ref[...]

    heads = range(N_HEADS)
    sls = [slice(h * DH, (h + 1) * DH) for h in heads]
    ic = [cols[:, h:h + 1] for h in heads]
    bc = [cols[:, N_HEADS + h:N_HEADS + h + 1] for h in heads]
    ir = [rows[h:h + 1, :] for h in heads]
    br = [rows[N_HEADS + h:N_HEADS + h + 1, :] for h in heads]
    b_last = [br[h][:, L - 1:L] for h in heads]
    m_prev = [m_ref[h:h + 1, 0:1] for h in heads]
    qh = [q[:, sls[h]] for h in heads]
    kh = [k[:, sls[h]] for h in heads]
    vaug = [jnp.concatenate([zq_ref[:, 2 * D_MLSTM + h * DH:2 * D_MLSTM + (h + 1) * DH], ones_col], axis=1)
            for h in heads]
    ct = [ct_ref[h] for h in heads]

    dm = [jnp.where(causal, bc[h] - br[h] + ir[h], -jnp.inf) for h in heads]
    dm[0] = _after(dm[0], zeros)
    inter_log = [bc[h] + m_prev[h] for h in heads]
    sc = [lax.dot_general(qh[h], kh[h], (((1,), (1,)), ((), ())), preferred_element_type=F32) for h in heads]
    inter = [jnp.dot(qh[h], ct[h].astype(BF16), preferred_element_type=F32) for h in heads]
    m_t = [jnp.maximum(inter_log[h], jnp.max(dm[h], axis=1, keepdims=True)) for h in heads]

    w_end = [b_last[h] - bc[h] + ic[h] for h in heads]
    m_loc = [jnp.max(w_end[h], axis=0, keepdims=True) for h in heads]
    m_new = [jnp.maximum(b_last[h] + m_prev[h], m_loc[h]) for h in heads]
    ev = [(jnp.exp(w_end[h] - m_loc[h]) * vaug[h].astype(F32)).astype(BF16) for h in heads]
    c_loc = [lax.dot_general(kh[h], ev[h], (((0,), (0,)), ((), ())), preferred_element_type=F32)
             for h in heads]

    p = [(sc[h] * jnp.exp(dm[h] - m_t[h])).astype(BF16) for h in heads]
    out = [jnp.dot(p[h], vaug[h], preferred_element_type=F32) + jnp.exp(inter_log[h] - m_t[h]) * inter[h]
           for h in heads]
    for h in heads:
        ct_ref[h] = (jnp.exp(b_last[h] + m_prev[h] - m_new[h]) * ct[h]
                     + jnp.exp(m_loc[h] - m_new[h]) * c_loc[h])
        m_ref[h:h + 1, :] = jnp.broadcast_to(m_new[h], (1, LANES))
    hh = [out[h][:, :DH] / jnp.maximum(jnp.abs(out[h][:, DH:DH + 1]), jnp.exp(-m_t[h])) for h in heads]
    hn = [hh[h] * lax.rsqrt(jnp.mean(hh[h] * hh[h], axis=-1, keepdims=True) + EPS) * gm[:, sls[h]]
          for h in heads]
    for h in heads:
        og = zq_ref[:, 3 * D_MLSTM + h * DH:3 * D_MLSTM + (h + 1) * DH].astype(F32)
        hm_ref[slot, :, sls[h]] = (hn[h] * _sigmoid(og)).astype(BF16)


def _mlstm_outproj(zq, zg, w_conv, b_conv, b_gates, g_mix, tri, hl, x, mod, wm, wl):
    L = L_MIX
    nc = SEQ // L
    n = BATCH * nc
    mixed = lambda j: (jnp.minimum(j, n - 1), 0)
    projected = lambda j: (jnp.maximum(j - 1, 0), 0)
    const = lambda shape: pl.BlockSpec(shape, lambda j: (0, 0))
    return pl.pallas_call(
        _mlstm_outproj_kernel,
        grid=(n + 1,),
        in_specs=[
            pl.BlockSpec((L, 4 * D_MLSTM), mixed),
            pl.BlockSpec((L, LANES), mixed),
            const((CONV_W, 2 * D_MLSTM)), const((1, 2 * D_MLSTM)), const((1, LANES)), const((1, D_MLSTM)),
            const((L, L)),
            pl.BlockSpec((L, D_LRU), projected),
            pl.BlockSpec((L, D_MODEL), projected),
            pl.BlockSpec((None, 6, D_MODEL), lambda j: (jnp.maximum(j - 1, 0) // nc, 0, 0)),
            const((D_MLSTM, D_MODEL)), const((D_LRU, D_MODEL)),
        ],
        out_specs=pl.BlockSpec((L, D_MODEL), projected),
        out_shape=jax.ShapeDtypeStruct((TOKENS, D_MODEL), F32),
        scratch_shapes=[
            pltpu.VMEM((N_HEADS, DH, 2 * DH), F32),
            pltpu.VMEM((SUBLANES, LANES), F32),
            pltpu.VMEM((SUBLANES + L, 2 * D_MLSTM), F32),
            pltpu.VMEM((2, L, D_MLSTM), BF16),
        ],
        compiler_params=pltpu.CompilerParams(
            dimension_semantics=("arbitrary",), vmem_limit_bytes=VMEM_LIMIT),
        name="mlstm_outproj",
    )(zq, zg, w_conv, b_conv, b_gates, g_mix, tri, hl, x, mod, wm, wl)


def _swiglu(hb, wg_ref, wu_ref, wd_ref):
    g = jnp.dot(hb, wg_ref[...], preferred_element_type=F32)
    u = jnp.dot(hb, wu_ref[...], preferred_element_type=F32)
    act = g * _sigmoid(g) * u
    return jnp.dot(act.astype(BF16), wd_ref[...], preferred_element_type=F32)


def _ffn_kernel(x_ref, mod_ref, g_ref, wg_ref, wu_ref, wd_ref, o_ref):
    x = x_ref[...]
    hb = _mod_norm(x, g_ref[...], mod_ref[4:5, :], mod_ref[3:4, :]).astype(BF16)
    o_ref[...] = x + mod_ref[5:6, :] * _swiglu(hb, wg_ref, wu_ref, wd_ref)


def _ffn(x, mod, g, wg, wu, wd):
    tm = TM_FFN
    per_b = SEQ // tm
    return pl.pallas_call(
        _ffn_kernel,
        grid=(TOKENS // tm,),
        in_specs=[
            pl.BlockSpec((tm, D_MODEL), lambda i: (i, 0)),
            pl.BlockSpec((None, 6, D_MODEL), lambda i: (i // per_b, 0, 0)),
            pl.BlockSpec((1, D_MODEL), lambda i: (0, 0)),
            pl.BlockSpec((D_MODEL, D_FF), lambda i: (0, 0)),
            pl.BlockSpec((D_MODEL, D_FF), lambda i: (0, 0)),
            pl.BlockSpec((D_FF, D_MODEL), lambda i: (0, 0)),
        ],
        out_specs=pl.BlockSpec((tm, D_MODEL), lambda i: (i, 0)),
        out_shape=jax.ShapeDtypeStruct((TOKENS, D_MODEL), F32),
        compiler_params=pltpu.CompilerParams(
            dimension_semantics=("arbitrary",), vmem_limit_bytes=VMEM_LIMIT),
        name="ffn",
    )(x, mod, g, wg, wu, wd)


def _router_kernel(x_ref, mod_ref, g_ref, wrh_ref, wrl_ref, br_ref, stri_ref, upper_ref,
                   h_ref, meta_ref, tab_ref, carry_ref, lg_ref):
    i = pl.program_id(0)
    slot = i % 2

    @pl.when(i == 0)
    def _():
        carry_ref[...] = jnp.zeros_like(carry_ref)
        lg_ref[...] = jnp.zeros_like(lg_ref)

    tm = x_ref.shape[0]
    lane = lax.broadcasted_iota(jnp.int32, (tm, LANES), 1)
    logits = jnp.where(lane < N_EXPERTS, lg_ref[1 - slot], -jnp.inf)
    m1 = jnp.max(logits, axis=-1, keepdims=True)

    h = _mod_norm(x_ref[...], g_ref[...], mod_ref[4:5, :], mod_ref[3:4, :])
    hb = h.astype(BF16)
    h_ref[...] = hb
    i1 = jnp.min(jnp.where(logits == m1, lane, LANES), axis=-1, keepdims=True)

    hlo = (h - hb.astype(F32)).astype(BF16)
    wrh = wrh_ref[...]
    new_logits = jnp.dot(hb, wrh, preferred_element_type=F32)
    rest = jnp.where(lane == i1, -jnp.inf, logits)
    m2 = jnp.max(rest, axis=-1, keepdims=True)
    new_logits = new_logits + jnp.dot(hlo, wrh, preferred_element_type=F32)
    i2 = jnp.min(jnp.where(rest == m2, lane, LANES), axis=-1, keepdims=True)
    new_logits = new_logits + jnp.dot(hb, wrl_ref[...], preferred_element_type=F32)
    lg_ref[slot] = new_logits + br_ref[...]

    e2 = jnp.exp(m2 - m1)
    p1 = 1.0 / (1.0 + e2)
    p2 = e2 * p1
    ind = jnp.where((lane == i1) | (lane == i2), 1.0, 0.0)
    carry = carry_ref[0:1, :]
    aligned = jnp.floor(carry * (1.0 / SUBLANES)) * SUBLANES
    lead = carry - aligned
    filled = lead + jnp.sum(ind, axis=0, keepdims=True)
    full = jnp.floor(filled * (1.0 / SUBLANES)) * SUBLANES
    n_e = jnp.floor((filled + (SUBLANES - 1)) * (1.0 / SUBLANES)) * SUBLANES
    lrank = jnp.dot(stri_ref[...], ind.astype(BF16), preferred_element_type=F32)
    loff = jnp.dot(jnp.broadcast_to(n_e, (SUBLANES, LANES)).astype(BF16), upper_ref[...],
                   preferred_element_type=F32)[0:1, :]
    row = lrank + loff + lead
    d1 = jnp.sum(jnp.where(lane == i1, row, 0.0), axis=-1, keepdims=True)
    d2 = jnp.sum(jnp.where(lane == i2, row, 0.0), axis=-1, keepdims=True)
    vals = (i1.astype(F32), i2.astype(F32), d1, d2, p1, p2)
    meta = jnp.zeros((tm, LANES), F32)
    for n, v in enumerate(vals):
        meta = jnp.where(lane == n, v, meta)
    meta_ref[...] = meta
    srow = lax.broadcasted_iota(jnp.int32, (SUBLANES, LANES), 0)
    carry = carry + jnp.where(i > 0, filled - lead, 0.0)
    tab = jnp.zeros((SUBLANES, LANES), F32)
    for n, v in enumerate((n_e, aligned, loff, full, carry)):
        tab = jnp.where(srow == n, v, tab)
    tab_ref[...] = tab
    carry_ref[0:1, :] = carry


def _router(x, mod, g, wrh, wrl, br, stri, upper):
    tm = TM_ROUTE
    n = TOKENS // tm
    per_b = SEQ // tm
    row = lambda shape: pl.BlockSpec(shape, lambda i: (0, 0))
    normed = lambda i: jnp.minimum(i, n - 1)
    ranked = lambda i: jnp.maximum(i - 1, 0)
    return pl.pallas_call(
        _router_kernel,
        grid=(n + 1,),
        in_specs=[
            pl.BlockSpec((tm, D_MODEL), lambda i: (normed(i), 0)),
            pl.BlockSpec((None, 6, D_MODEL), lambda i: (normed(i) // per_b, 0, 0)),
            row((1, D_MODEL)), row((D_MODEL, LANES)), row((D_MODEL, LANES)), row((1, LANES)),
            row((tm, tm)), row((LANES, LANES)),
        ],
        out_specs=[
            pl.BlockSpec((tm, D_MODEL), lambda i: (normed(i), 0)),
            pl.BlockSpec((tm, LANES), lambda i: (ranked(i), 0)),
            pl.BlockSpec((None, SUBLANES, LANES), lambda i: (ranked(i), 0, 0)),
        ],
        out_shape=[
            jax.ShapeDtypeStruct((TOKENS, D_MODEL), BF16),
            jax.ShapeDtypeStruct((TOKENS, LANES), F32),
            jax.ShapeDtypeStruct((n, SUBLANES, LANES), F32),
        ],
        scratch_shapes=[pltpu.VMEM((SUBLANES, LANES), F32), pltpu.VMEM((2, tm, LANES), F32)],
        compiler_params=pltpu.CompilerParams(
            dimension_semantics=("arbitrary",), vmem_limit_bytes=VMEM_LIMIT),
        name="router",
    )(x, mod, g, wrh, wrl, br, stri, upper)


BLOCK_ROWS = 2 * TM_ROUTE + LANES
RANGE_PIECES = tuple(1 << b for b in range(TM_ROUTE.bit_length() - 1, 2, -1))
assert TM_SLOT <= 2 * RANGE_PIECES[0]


def _range_pieces(n, loc, slot, block_ref, slots_ref, sem, to_slots, visit):
    loc = pl.multiple_of(loc, SUBLANES)
    slot = pl.multiple_of(slot, SUBLANES)
    for piece in RANGE_PIECES:
        has = (n & piece) != 0
        src = block_ref.at[pl.ds(loc, piece)]
        dst = slots_ref.at[pl.ds(slot, piece)]
        if not to_slots:
            src, dst = dst, src

        @pl.when(has)
        def _():
            visit(pltpu.make_async_copy(src, dst, sem))

        step = jnp.where(has, piece, 0)
        loc = pl.multiple_of(loc + step, SUBLANES)
        slot = pl.multiple_of(slot + step, SUBLANES)


def _chunk_copies(chunk, n_ref, loc_ref, slot_ref, block_ref, slots_ref, sem, to_slots, visit):
    for e in range(N_EXPERTS):
        i = chunk * N_EXPERTS + e
        _range_pieces(n_ref[i], loc_ref[i], slot_ref[i], block_ref, slots_ref, sem, to_slots, visit)


def _start(cp):
    cp.start()


def _wait(cp):
    cp.wait()


def _choice_onehots(meta, width):
    col = lax.broadcasted_iota(jnp.int32, (meta.shape[0], width), 1)
    d1 = meta[:, 2:3].astype(jnp.int32)
    d2 = meta[:, 3:4].astype(jnp.int32)
    return col == d1, col == d2


def _zero_copies(padn_ref, pads_ref, used_ref, zero_ref, xs_ref, sem, visit):
    for e in range(N_EXPERTS):
        _range_pieces(padn_ref[e], 0, pads_ref[e], zero_ref, xs_ref, sem, True, visit)
    for t in range(2 * TOKENS // TM_SLOT, N_SLOTS // TM_SLOT):
        @pl.when(t * TM_SLOT >= used_ref[0])
        def _():
            visit(pltpu.make_async_copy(zero_ref, xs_ref.at[pl.ds(t * TM_SLOT, TM_SLOT)], sem))


def _dispatch_kernel(n_ref, loc_ref, slot_ref, full_ref, padn_ref, pads_ref, used_ref, h_ref, meta_ref,
                     xs_ref, block_ref, zero_ref, tail_ref, sem, zsem):
    i = pl.program_id(0)
    last = pl.num_programs(0) - 1
    cur = i % 2
    zeros = (padn_ref, pads_ref, used_ref, zero_ref, xs_ref, zsem)

    @pl.when(i == 0)
    def _():
        zero_ref[...] = jnp.zeros_like(zero_ref)
        tail_ref[...] = jnp.zeros_like(tail_ref)
        _zero_copies(*zeros, _start)

    rows_t = jnp.transpose(meta_ref[...])
    brow = lax.broadcasted_iota(jnp.int32, (BLOCK_ROWS, rows_t.shape[1]), 0)
    sel = ((brow == rows_t[2:3, :].astype(jnp.int32)) | (brow == rows_t[3:4, :].astype(jnp.int32))).astype(BF16)
    block_ref[cur] = jnp.dot(sel, h_ref[...], preferred_element_type=F32)
    blk = block_ref.at[cur]
    for e in range(N_EXPERTS):
        t = i * N_EXPERTS + e
        first = pl.ds(pl.multiple_of(loc_ref[t], SUBLANES), SUBLANES)
        blk[first, :] = blk[first, :] + tail_ref[e]
        partial = full_ref[t] < n_ref[t]
        lastt = pl.ds(pl.multiple_of(loc_ref[t] + full_ref[t], SUBLANES), SUBLANES)
        tail_ref[e] = jnp.where(partial, blk[lastt, :], 0.0)
    tabs = (n_ref, loc_ref, slot_ref)

    @pl.when(i > 0)
    def _():
        _chunk_copies(i - 1, *tabs, block_ref.at[1 - cur], xs_ref, sem.at[1 - cur], True, _wait)

    _chunk_copies(i, *tabs, block_ref.at[cur], xs_ref, sem.at[cur], True, _start)

    @pl.when(i == last)
    def _():
        _chunk_copies(i, *tabs, block_ref.at[cur], xs_ref, sem.at[cur], True, _wait)
        _zero_copies(*zeros, _wait)


def _dispatch(tabs, full, pad_tabs, h, meta):
    tm = TM_ROUTE
    return pl.pallas_call(
        _dispatch_kernel,
        grid_spec=pltpu.PrefetchScalarGridSpec(
            num_scalar_prefetch=7,
            grid=(TOKENS // tm,),
            in_specs=[
                pl.BlockSpec((tm, D_MODEL), lambda i, *_: (i, 0)),
                pl.BlockSpec((tm, LANES), lambda i, *_: (i, 0)),
            ],
            out_specs=pl.BlockSpec(memory_space=pl.ANY),
            scratch_shapes=[pltpu.VMEM((2, BLOCK_ROWS, D_MODEL), F32),
                            pltpu.VMEM((TM_SLOT, D_MODEL), F32),
                            pltpu.VMEM((N_EXPERTS, SUBLANES, D_MODEL), F32),
                            pltpu.SemaphoreType.DMA((2,)), pltpu.SemaphoreType.DMA],
        ),
        out_shape=jax.ShapeDtypeStruct((N_SLOTS, D_MODEL), F32),
        compiler_params=pltpu.CompilerParams(
            dimension_semantics=("arbitrary",), vmem_limit_bytes=VMEM_LIMIT),
        name="dispatch",
    )(*tabs, full, *pad_tabs, h, meta)


def _experts_kernel(te_ref, tv_ref, ts_ref, xs_ref, wg_ref, wu_ref, wd_ref, o_ref):
    del te_ref, ts_ref
    valid = tv_ref[pl.program_id(0)] != 0

    @pl.when(valid)
    def _():
        o_ref[...] = _swiglu(xs_ref[...].astype(BF16), wg_ref, wu_ref, wd_ref)

    @pl.when(jnp.logical_not(valid))
    def _():
        o_ref[...] = jnp.zeros_like(o_ref)


def _experts(tile_e, tile_v, tile_src, xs, wg, wu, wd):
    tm = TM_SLOT
    return pl.pallas_call(
        _experts_kernel,
        grid_spec=pltpu.PrefetchScalarGridSpec(
            num_scalar_prefetch=3,
            grid=(N_SLOTS // tm,),
            in_specs=[
                pl.BlockSpec((tm, D_MODEL), lambda i, te, tv, ts: (ts[i], 0)),
                pl.BlockSpec((None, D_MODEL, D_FF), lambda i, te, tv, ts: (te[i], 0, 0)),
                pl.BlockSpec((None, D_MODEL, D_FF), lambda i, te, tv, ts: (te[i], 0, 0)),
                pl.BlockSpec((None, D_FF, D_MODEL), lambda i, te, tv, ts: (te[i], 0, 0)),
            ],
            out_specs=pl.BlockSpec((tm, D_MODEL), lambda i, te, tv, ts: (i, 0)),
        ),
        out_shape=jax.ShapeDtypeStruct((N_SLOTS, D_MODEL), F32),
        compiler_params=pltpu.CompilerParams(
            dimension_semantics=("arbitrary",), vmem_limit_bytes=VMEM_LIMIT),
        name="experts",
    )(tile_e, tile_v, tile_src, xs, wg, wu, wd)


def _combine_kernel(n_ref, loc_ref, slot_ref, x_ref, mod_ref, meta_ref, gf_ref, ys_ref, o_ref,
                    block_ref, sem):
    i = pl.program_id(0)
    cur = i % 2
    tabs = (n_ref, loc_ref, slot_ref)

    @pl.when(i == 0)
    def _():
        block_ref[...] = jnp.zeros_like(block_ref)
        _chunk_copies(0, *tabs, block_ref.at[0], ys_ref, sem.at[0], False, _start)

    @pl.when(i + 1 < pl.num_programs(0))
    def _():
        _chunk_copies(i + 1, *tabs, block_ref.at[1 - cur], ys_ref, sem.at[1 - cur], False, _start)

    _chunk_copies(i, *tabs, block_ref.at[cur], ys_ref, sem.at[cur], False, _wait)
    meta = meta_ref[...]
    a1, a2 = _choice_onehots(meta, BLOCK_ROWS)
    blk = block_ref[cur].astype(BF16)
    y = (meta[:, 4:5] * jnp.dot(a1.astype(BF16), blk, preferred_element_type=F32)
         + meta[:, 5:6] * jnp.dot(a2.astype(BF16), blk, preferred_element_type=F32))
    y = x_ref[...] + mod_ref[5:6, :] * y
    o_ref[...] = y * lax.rsqrt(jnp.mean(y * y, axis=-1, keepdims=True) + EPS) * gf_ref[...]


def _combine(tabs, x, mod, meta, g_final, ys):
    tm = TM_ROUTE
    per_b = SEQ // tm
    return pl.pallas_call(
        _combine_kernel,
        grid_spec=pltpu.PrefetchScalarGridSpec(
            num_scalar_prefetch=3,
            grid=(TOKENS // tm,),
            in_specs=[
                pl.BlockSpec((tm, D_MODEL), lambda i, *_: (i, 0)),
                pl.BlockSpec((None, 6, D_MODEL), lambda i, *_: (i // per_b, 0, 0)),
                pl.BlockSpec((tm, LANES), lambda i, *_: (i, 0)),
                pl.BlockSpec((1, D_MODEL), lambda i, *_: (0, 0)),
                pl.BlockSpec(memory_space=pl.ANY),
            ],
            out_specs=pl.BlockSpec((tm, D_MODEL), lambda i, *_: (i, 0)),
            scratch_shapes=[pltpu.VMEM((2, BLOCK_ROWS, D_MODEL), F32), pltpu.SemaphoreType.DMA((2,))],
        ),
        out_shape=jax.ShapeDtypeStruct((TOKENS, D_MODEL), F32),
        compiler_params=pltpu.CompilerParams(
            dimension_semantics=("arbitrary",), vmem_limit_bytes=VMEM_LIMIT),
        name="combine",
    )(*tabs, x, mod, meta, g_final, ys)


def _moe(x, mod, g, w_router, b_router, wg, wu, wd, g_final):
    wr = _pad_lanes(w_router)
    wrh = wr.astype(BF16)
    wrl = (wr - wrh.astype(F32)).astype(BF16)
    stri = (lax.broadcasted_iota(jnp.int32, (TM_ROUTE, TM_ROUTE), 1)
            < lax.broadcasted_iota(jnp.int32, (TM_ROUTE, TM_ROUTE), 0)).astype(BF16)
    upper = (lax.broadcasted_iota(jnp.int32, (LANES, LANES), 0)
             < lax.broadcasted_iota(jnp.int32, (LANES, LANES), 1)).astype(BF16)
    h, meta, tab = _router(x, mod, g, wrh, wrl, _pad_lanes(b_router[None]), stri, upper)
    tab = tab[:, :5, :N_EXPERTS].astype(jnp.int32)
    n_ce, tile_ce, loc_ce, full_ce = tab[:, 0], tab[:, 1], tab[:, 2], tab[:, 3]
    counts = tab[-1, 4]
    padded = (counts + TM_SLOT - 1) // TM_SLOT * TM_SLOT
    ends = jnp.cumsum(padded)
    slot_ce = (ends - padded)[None, :] + tile_ce
    tabs = (n_ce.reshape(-1), loc_ce.reshape(-1), slot_ce.reshape(-1))
    tiled = (counts + SUBLANES - 1) // SUBLANES * SUBLANES
    pad_tabs = (padded - tiled, ends - padded + tiled, ends[-1:])
    tile = jnp.arange(N_SLOTS // TM_SLOT, dtype=jnp.int32)
    tile_e = jnp.minimum(jnp.sum(tile[:, None] * TM_SLOT >= ends[None, :], axis=1), N_EXPERTS - 1)
    tile_v = (tile * TM_SLOT < ends[-1]).astype(jnp.int32)
    tile_src = jnp.minimum(tile, ends[-1] // TM_SLOT - 1)
    xs = _dispatch(tabs, full_ce.reshape(-1), pad_tabs, h, meta)
    ys = _experts(tile_e.astype(jnp.int32), tile_v, tile_src, xs, wg, wu, wd)
    return _combine(tabs, x, mod, meta, g_final, ys)


def _block_diag(w):
    eye = jnp.eye(N_LRU_BLOCKS, dtype=w.dtype)
    return jnp.einsum('nde,nm->ndme', w, eye).reshape(D_LRU, D_LRU)


def _pad_lanes(a):
    return jnp.zeros(a.shape[:-1] + (LANES,), a.dtype).at[..., :a.shape[-1]].set(a)


def kernel(x, c, w_ada, b_ada, g_norm_mix, g_norm_ffn, w_in, w_conv_qk, b_conv_qk, b_gates,
           w_conv_lru, b_conv_lru, w_lru_a, b_lru_a, w_lru_x, b_lru_x, lru_lambda, g_mix_out, w_out,
           w_ff_gate, w_ff_up, w_ff_down, w_router, b_router, w_exp_gate, w_exp_up, w_exp_down, g_final):
    assert DEPTH == 2
    mods = _ada(c, w_ada, b_ada)
    tri = (lax.broadcasted_iota(jnp.int32, (L_MIX, L_MIX), 1)
           <= lax.broadcasted_iota(jnp.int32, (L_MIX, L_MIX), 0)).astype(BF16)
    xt = x.reshape(TOKENS, D_MODEL)
    for l in range(DEPTH):
        mod = mods[l]
        wi = w_in[l]
        n_q = 4 * D_MLSTM
        wcat = jnp.concatenate([wi[:, :n_q], wi[:, n_q + 2 * N_HEADS:],
                                _pad_lanes(wi[:, n_q:n_q + 2 * N_HEADS])], axis=1).astype(BF16)
        wax = jnp.concatenate([_block_diag(w_lru_a[l]), _block_diag(w_lru_x[l])], axis=1).astype(BF16)
        bax = jnp.concatenate([b_lru_a[l], b_lru_x[l]])[None]
        zq, zg, hl = _inproj_lru(xt, mod, g_norm_mix[l][None], wcat,
                                 w_conv_lru[l], b_conv_lru[l][None], wax, bax, lru_lambda[l][None],
                                 g_mix_out[l][None, D_MLSTM:])
        wo = w_out[l].astype(BF16)
        xt = _mlstm_outproj(zq, zg, w_conv_qk[l], b_conv_qk[l][None], _pad_lanes(b_gates[l][None]),
                            g_mix_out[l][None, :D_MLSTM], tri, hl, xt, mod, wo[:D_MLSTM], wo[D_MLSTM:])
        jj = l // 2
        if l % 2 == 0:
            xt = _ffn(xt, mod, g_norm_ffn[l][None], w_ff_gate[jj].astype(BF16),
                      w_ff_up[jj].astype(BF16), w_ff_down[jj].astype(BF16))
        else:
            xt = _moe(xt, mod, g_norm_ffn[l][None], w_router[jj], b_router[jj],
                      w_exp_gate[jj].astype(BF16), w_exp_up[jj].astype(BF16),
                      w_exp_down[jj].astype(BF16), g_final[None])
    return xt.reshape(BATCH, SEQ, D_MODEL)
```
